```python
import math
import jax, jax.numpy as jnp
from jax import lax
import numpy as np

D_MODEL = 1024
BATCH = 8
SEQ = 4096
DEPTH = 1

MEM_TOKENS = 256
RMS_EPS = 1e-5

D_MIX = D_MODEL
HEAD_DIM = 64
ATTN_WIDTH = D_MIX // 2
N_Q_HEADS = ATTN_WIDTH // HEAD_DIM
N_KV_HEADS = 2
Q_PER_KV = N_Q_HEADS // N_KV_HEADS
WINDOW = 128
ATTN_BLOCK = 128
ROT_DIM = HEAD_DIM // 4
ROPE_THETA = 500000.0
MAX_POS_OFFSET = 2048

SSD_WIDTH = D_MIX - ATTN_WIDTH
SSD_HEAD_DIM = 64
SSD_HEADS = SSD_WIDTH // SSD_HEAD_DIM
SSD_GROUPS = 2
SSD_HEADS_PER_GROUP = SSD_HEADS // SSD_GROUPS
SSD_STATE = 128
SSD_CONV = 4
SSD_CHUNK = 128
SSD_CONV_DIM = SSD_WIDTH + 2 * SSD_GROUPS * SSD_STATE
DT_MIN = 0.001
DT_MAX = 0.1

SPLIT_POINTS = (
    ATTN_WIDTH,
    ATTN_WIDTH + N_KV_HEADS * HEAD_DIM,
    ATTN_WIDTH + 2 * N_KV_HEADS * HEAD_DIM,
    ATTN_WIDTH + 2 * N_KV_HEADS * HEAD_DIM + SSD_WIDTH,
    ATTN_WIDTH + 2 * N_KV_HEADS * HEAD_DIM + SSD_WIDTH + SSD_CONV_DIM,
)
IN_PROJ_DIM = SPLIT_POINTS[-1] + SSD_HEADS

XATTN_HEADS = 4
XATTN_HEAD_DIM = 128
XATTN_WIDTH = XATTN_HEADS * XATTN_HEAD_DIM

N_GROUPS = 4
EXPERTS_PER_GROUP = 8
N_EXPERTS = N_GROUPS * EXPERTS_PER_GROUP
TOP_K = 2
EXPERT_FF = D_MODEL // 4

kernel_name = 'hybrid_swa_ssd_hier_moe_block'


def rms_norm(x, w):
    xf = x.astype(jnp.float32)
    y = xf * lax.rsqrt(jnp.mean(xf * xf, axis=-1, keepdims=True) + RMS_EPS)
    return (y * w.astype(jnp.float32)).astype(x.dtype)


def partial_rope(t, positions):
    half = ROT_DIM // 2
    inv_freq = jnp.float32(ROPE_THETA) ** (-jnp.arange(0, ROT_DIM, 2, dtype=jnp.float32) / ROT_DIM)
    ang = positions.astype(jnp.float32)[..., None] * inv_freq
    cos = jnp.cos(ang)[:, :, None, :]
    sin = jnp.sin(ang)[:, :, None, :]
    tf = t.astype(jnp.float32)
    x1, x2, rest = tf[..., :half], tf[..., half:ROT_DIM], tf[..., ROT_DIM:]
    out = jnp.concatenate([x1 * cos - x2 * sin, x2 * cos + x1 * sin, rest], axis=-1)
    return out.astype(t.dtype)


def sliding_window_attention(q, k, v, sinks):
    b, s = q.shape[0], q.shape[1]
    nb = s // ATTN_BLOCK
    qb = q.reshape(b, nb, ATTN_BLOCK, N_KV_HEADS, Q_PER_KV, HEAD_DIM)

    def band(t):
        tb = t.reshape(b, nb, ATTN_BLOCK, N_KV_HEADS, HEAD_DIM)
        prev = jnp.pad(tb[:, :-1], ((0, 0), (1, 0), (0, 0), (0, 0), (0, 0)))
        return jnp.concatenate([prev, tb], axis=2)

    kb, vb = band(k), band(v)
    scores = jnp.einsum('bnqhgd,bnkhd->bnhgqk', qb, kb,
                        preferred_element_type=jnp.float32) * (HEAD_DIM ** -0.5)
    q_idx = jnp.arange(ATTN_BLOCK)[:, None] + ATTN_BLOCK
    k_idx = jnp.arange(2 * ATTN_BLOCK)[None, :]
    rel = q_idx - k_idx
    local = (rel >= 0) & (rel < WINDOW)
    valid_prev = (jnp.arange(nb)[:, None, None] > 0) | (k_idx[None] >= ATTN_BLOCK)
    mask = local[None] & valid_prev
    scores = jnp.where(mask[None, :, None, None], scores, -jnp.inf)
    sink = sinks.astype(jnp.float32).reshape(N_KV_HEADS, Q_PER_KV)[None, None, :, :, None, None]
    m = jnp.maximum(jnp.max(scores, axis=-1, keepdims=True), sink)
    p = jnp.exp(scores - m)
    probs = p / (jnp.sum(p, axis=-1, keepdims=True) + jnp.exp(sink - m))
    out = jnp.einsum('bnhgqk,bnkhd->bnqhgd', probs.astype(v.dtype), vb)
    return out.reshape(b, s, N_Q_HEADS * HEAD_DIM)


def causal_depthwise_conv(t, w, bias):
    c = t.shape[-1]
    out = lax.conv_general_dilated(t, w[:, None, :], window_strides=(1,),
                                   padding=[(SSD_CONV - 1, 0)],
                                   dimension_numbers=('NWC', 'WIO', 'NWC'),
                                   feature_group_count=c)
    return out + bias


def ssd_chunked(xs, dt, a, bm, cm):
    b, s = xs.shape[0], xs.shape[1]
    nc = s // SSD_CHUNK
    G, R, P, L = SSD_GROUPS, SSD_HEADS_PER_GROUP, SSD_HEAD_DIM, SSD_CHUNK
    xc = (xs * dt[..., None]).reshape(b, nc, L, G, R, P)
    da = (dt * a).reshape(b, nc, L, G, R)
    da_cum = jnp.cumsum(da, axis=2)
    bc = bm.reshape(b, nc, L, G, SSD_STATE)
    cc = cm.reshape(b, nc, L, G, SSD_STATE)
    seg = da_cum[:, :, :, None] - da_cum[:, :, None, :]
    causal = jnp.tril(jnp.ones((L, L), dtype=bool))
    decay = jnp.exp(jnp.where(causal[:, :, None, None], seg, -jnp.inf))
    cb = jnp.einsum('bclgn,bcsgn->bclsg', cc, bc)
    y_diag = jnp.einsum('bclsgr,bcsgrp->bclgrp', cb[..., None] * decay, xc)
    decay_to_end = jnp.exp(da_cum[:, :, -1:] - da_cum)
    chunk_states = jnp.einsum('bclgn,bclgrp->bcgrpn', bc, xc * decay_to_end[..., None])
    chunk_decay = jnp.exp(da_cum[:, :, -1])

    def step(state, inp):
        st_c, dec_c = inp
        return state * dec_c[..., None, None] + st_c, state

    init = jnp.zeros((b, G, R, P, SSD_STATE), jnp.float32)
    _, start_states = lax.scan(step, init, (jnp.moveaxis(chunk_states, 1, 0),
                                            jnp.moveaxis(chunk_decay, 1, 0)))
    start_states = jnp.moveaxis(start_states, 0, 1)
    y_off = jnp.einsum('bclgn,bcgrpn->bclgrp', cc, start_states) * jnp.exp(da_cum)[..., None]
    return (y_diag + y_off).reshape(b, s, SSD_HEADS, P)


def gated_group_rms_norm(y, z, w):
    b, s, _ = y.shape
    g = (y * jax.nn.silu(z.astype(jnp.float32))).reshape(b, s, SSD_GROUPS, -1)
    g = g * lax.rsqrt(jnp.mean(g * g, axis=-1, keepdims=True) + RMS_EPS)
    return g.reshape(b, s, SSD_WIDTH) * w.astype(jnp.float32)


def hybrid_mixer(h, positions, w_in, sinks, conv_w, conv_b, dt_bias, a_log, d_skip,
                 attn_out_norm_w, ssd_out_norm_w, w_out):
    b, s, _ = h.shape
    proj = h @ w_in
    q, k, v, z, xbc, dt_raw = jnp.split(proj, SPLIT_POINTS, axis=-1)
    q = partial_rope(q.reshape(b, s, N_Q_HEADS, HEAD_DIM), positions)
    k = partial_rope(k.reshape(b, s, N_KV_HEADS, HEAD_DIM), positions)
    v = v.reshape(b, s, N_KV_HEADS, HEAD_DIM)
    attn = rms_norm(sliding_window_attention(q, k, v, sinks), attn_out_norm_w)
    xbc = jax.nn.silu(causal_depthwise_conv(xbc, conv_w, conv_b))
    xs, bm, cm = jnp.split(xbc, (SSD_WIDTH, SSD_WIDTH + SSD_GROUPS * SSD_STATE), axis=-1)
    xs = xs.reshape(b, s, SSD_HEADS, SSD_HEAD_DIM).astype(jnp.float32)
    dt = jax.nn.softplus(dt_raw.astype(jnp.float32) + dt_bias.astype(jnp.float32))
    a = -jnp.exp(a_log.astype(jnp.float32))
    y = ssd_chunked(xs, dt, a,
                    bm.reshape(b, s, SSD_GROUPS, SSD_STATE).astype(jnp.float32),
                    cm.reshape(b, s, SSD_GROUPS, SSD_STATE).astype(jnp.float32))
    y = y + d_skip.astype(jnp.float32)[:, None] * xs
    ssd = gated_group_rms_norm(y.reshape(b, s, SSD_WIDTH), z, ssd_out_norm_w).astype(h.dtype)
    return jnp.concatenate([attn, ssd], axis=-1) @ w_out


def memory_cross_attention(h, mem_h, w_q, w_kv, w_o):
    b, s, _ = h.shape
    m = mem_h.shape[1]
    q = (h @ w_q).reshape(b, s, XATTN_HEADS, XATTN_HEAD_DIM)
    kv = (mem_h @ w_kv).reshape(b, m, 2, XATTN_HEADS, XATTN_HEAD_DIM)
    k, v = kv[:, :, 0], kv[:, :, 1]
    scores = jnp.einsum('bshd,bmhd->bhsm', q, k,
                        preferred_element_type=jnp.float32) * (XATTN_HEAD_DIM ** -0.5)
    probs = jax.nn.softmax(scores, axis=-1).astype(v.dtype)
    out = jnp.einsum('bhsm,bmhd->bshd', probs, v).reshape(b, s, XATTN_WIDTH)
    return out @ w_o


def hierarchical_moe(h, wg, bg, we, be, w_gate, w_up, w_down):
    b, s, d = h.shape
    t = h.reshape(b * s, d)
    g_prob = jax.nn.softmax((t @ wg).astype(jnp.float32) + bg.astype(jnp.float32), axis=-1)
    g_gate, g_idx = lax.top_k(g_prob, 1)
    e_logits = ((t @ we).astype(jnp.float32) + be.astype(jnp.float32)).reshape(-1, N_GROUPS, EXPERTS_PER_GROUP)
    e_sel = jnp.take_along_axis(e_logits, g_idx[:, :, None], axis=1)[:, 0]
    top_vals, top_idx = lax.top_k(e_sel, TOP_K)
    top_w = jax.nn.softmax(top_vals, axis=-1) * g_gate
    expert_id = g_idx * EXPERTS_PER_GROUP + top_idx
    gates = jnp.sum(jax.nn.one_hot(expert_id, N_EXPERTS, dtype=jnp.float32) * top_w[..., None], axis=1)
    gates = gates.astype(t.dtype)
    y = jnp.zeros_like(t)
    for e in range(N_EXPERTS):
        hid = jax.nn.silu(t @ w_gate[e]) * (t @ w_up[e])
        y = y + gates[:, e:e + 1] * (hid @ w_down[e])
    return y.reshape(b, s, d)


def _normal(k, shape, scale):
    return scale * jax.random.normal(k, shape, jnp.float32)


def _gain(k, shape):
    return 1.0 + 0.02 * jax.random.normal(k, shape, jnp.float32)


def setup_inputs(seed: int = 0) -> dict:
    key = jax.random.key(seed)
    ks = jax.random.split(key, 28)
    L = DEPTH
    x = _normal(ks[0], (BATCH, SEQ, D_MODEL), 1.0)
    mem = _normal(ks[1], (BATCH, MEM_TOKENS, D_MODEL), 1.0)
    start = jax.random.randint(ks[2], (BATCH, 1), 0, MAX_POS_OFFSET, dtype=jnp.int32)
    positions = start + jnp.arange(SEQ, dtype=jnp.int32)[None, :]
    mix_norm_w = _gain(ks[3], (L, D_MODEL))
    w_in = _normal(ks[4], (L, D_MODEL, IN_PROJ_DIM), D_MODEL ** -0.5)
    attn_sinks = _normal(ks[5], (L, N_Q_HEADS), 0.5)
    ssd_conv_w = _normal(ks[6], (L, SSD_CONV, SSD_CONV_DIM), SSD_CONV ** -0.5)
    ssd_conv_b = _normal(ks[7], (L, SSD_CONV_DIM), 0.02)
    dt0 = jnp.exp(jax.random.uniform(ks[8], (L, SSD_HEADS), jnp.float32,
                                     minval=math.log(DT_MIN), maxval=math.log(DT_MAX)))
    ssd_dt_bias = dt0 + jnp.log(-jnp.expm1(-dt0))
    ssd_a_log = jnp.log(jax.random.uniform(ks[9], (L, SSD_HEADS), jnp.float32, minval=1.0, maxval=16.0))
    ssd_d = 1.0 + 0.1 * jax.random.normal(ks[10], (L, SSD_HEADS), jnp.float32)
    attn_out_norm_w = _gain(ks[11], (L, ATTN_WIDTH))
    ssd_out_norm_w = _gain(ks[12], (L, SSD_WIDTH))
    w_out = _normal(ks[13], (L, D_MIX, D_MODEL), D_MIX ** -0.5)
    xattn_norm_w = _gain(ks[14], (L, D_MODEL))
    mem_norm_w = _gain(ks[15], (L, D_MODEL))
    xattn_w_q = _normal(ks[16], (L, D_MODEL, XATTN_WIDTH), D_MODEL ** -0.5)
    xattn_w_kv = _normal(ks[17], (L, D_MODEL, 2 * XATTN_WIDTH), D_MODEL ** -0.5)
    xattn_w_o = _normal(ks[18], (L, XATTN_WIDTH, D_MODEL), XATTN_WIDTH ** -0.5)
    ffn_norm_w = _gain(ks[19], (L, D_MODEL))
    router_group_w = _normal(ks[20], (L, D_MODEL, N_GROUPS), D_MODEL ** -0.5)
    router_group_b = _normal(ks[21], (L, N_GROUPS), 0.01)
    router_expert_w = _normal(ks[22], (L, D_MODEL, N_EXPERTS), D_MODEL ** -0.5)
    router_expert_b = _normal(ks[23], (L, N_EXPERTS), 0.01)
    expert_w_gate = _normal(ks[24], (L, N_EXPERTS, D_MODEL, EXPERT_FF), D_MODEL ** -0.5)
    expert_w_up = _normal(ks[25], (L, N_EXPERTS, D_MODEL, EXPERT_FF), D_MODEL ** -0.5)
    expert_w_down = _normal(ks[26], (L, N_EXPERTS, EXPERT_FF, D_MODEL), EXPERT_FF ** -0.5)
    final_norm_w = _gain(ks[27], (D_MODEL,))
    return {
        'x': x, 'mem': mem, 'positions': positions,
        'mix_norm_w': mix_norm_w, 'w_in': w_in, 'attn_sinks': attn_sinks,
        'ssd_conv_w': ssd_conv_w, 'ssd_conv_b': ssd_conv_b, 'ssd_dt_bias': ssd_dt_bias,
        'ssd_a_log': ssd_a_log, 'ssd_d': ssd_d, 'attn_out_norm_w': attn_out_norm_w,
        'ssd_out_norm_w': ssd_out_norm_w, 'w_out': w_out,
        'xattn_norm_w': xattn_norm_w, 'mem_norm_w': mem_norm_w, 'xattn_w_q': xattn_w_q,
        'xattn_w_kv': xattn_w_kv, 'xattn_w_o': xattn_w_o,
        'ffn_norm_w': ffn_norm_w, 'router_group_w': router_group_w, 'router_group_b': router_group_b,
        'router_expert_w': router_expert_w, 'router_expert_b': router_expert_b,
        'expert_w_gate': expert_w_gate, 'expert_w_up': expert_w_up, 'expert_w_down': expert_w_down,
        'final_norm_w': final_norm_w,
    }


def reference(x, mem, positions, mix_norm_w, w_in, attn_sinks, ssd_conv_w, ssd_conv_b,
              ssd_dt_bias, ssd_a_log, ssd_d, attn_out_norm_w, ssd_out_norm_w, w_out,
              xattn_norm_w, mem_norm_w, xattn_w_q, xattn_w_kv, xattn_w_o,
              ffn_norm_w, router_group_w, router_group_b, router_expert_w, router_expert_b,
              expert_w_gate, expert_w_up, expert_w_down, final_norm_w):
    for l in range(DEPTH):
        x = x + hybrid_mixer(rms_norm(x, mix_norm_w[l]), positions, w_in[l], attn_sinks[l],
                             ssd_conv_w[l], ssd_conv_b[l], ssd_dt_bias[l], ssd_a_log[l], ssd_d[l],
                             attn_out_norm_w[l], ssd_out_norm_w[l], w_out[l])
        x = x + memory_cross_attention(rms_norm(x, xattn_norm_w[l]), rms_norm(mem, mem_norm_w[l]),
                                       xattn_w_q[l], xattn_w_kv[l], xattn_w_o[l])
        x = x + hierarchical_moe(rms_norm(x, ffn_norm_w[l]), router_group_w[l], router_group_b[l],
                                 router_expert_w[l], router_expert_b[l],
                                 expert_w_gate[l], expert_w_up[l], expert_w_down[l])
    return rms_norm(x, final_norm_w)
```

```python
import functools

import numpy as np
import jax
import jax.numpy as jnp
from jax import lax
from jax.experimental import pallas as pl
from jax.experimental.pallas import tpu as pltpu

RMS_EPS = 1e-5
HEAD_DIM = 64
N_KV_HEADS = 2
WINDOW = 128
ATTN_BLOCK = 128
ROT_DIM = 16
ROT_HALF = ROT_DIM // 2
ROPE_THETA = 500000.0
SSD_HEAD_DIM = 64
SSD_GROUPS = 2
SSD_STATE = 128
SSD_CONV = 4
SSD_CHUNK = 128
XATTN_HEADS = 4
N_GROUPS = 4
EXPERTS_PER_GROUP = 8

LANES = 128
SUBLANES = 8
MOE_TILE = 256
N_BUCKETS = N_GROUPS * EXPERTS_PER_GROUP * EXPERTS_PER_GROUP
N_USED_BUCKETS = N_GROUPS * (EXPERTS_PER_GROUP * (EXPERTS_PER_GROUP - 1) // 2)
META_LANES = LANES
VMEM_LIMIT = 56 * 1024 * 1024

HIGHEST = lax.Precision.HIGHEST
F32 = jnp.float32
BF16 = jnp.bfloat16


def _cparams(semantics):
    return pltpu.CompilerParams(dimension_semantics=semantics, vmem_limit_bytes=VMEM_LIMIT)


def _rms(x, w):
    return x * lax.rsqrt(jnp.mean(x * x, axis=-1, keepdims=True) + RMS_EPS) * w


def _dot(a, b):
    return jnp.dot(a, b, preferred_element_type=F32)


def _dot_nt(a, b):
    return lax.dot_general(a, b, (((1,), (1,)), ((), ())), preferred_element_type=F32)


def _in_proj_kernel(pos_ref, x_ref, nw_ref, wq_ref, wk_ref, wv_ref, wz_ref, wx_ref, wdt_ref,
                    ec_ref, eup_ref, edn_ref, base_ref,
                    q_ref, k_ref, v_ref, z_ref, xbc_ref, dt_ref):
    x = x_ref[...]
    hb = _rms(x, nw_ref[...]).astype(BF16)

    j = lax.broadcasted_iota(jnp.int32, (ROT_HALF, 1), 0).astype(F32)
    inv_freq = jnp.power(jnp.float32(ROPE_THETA), -(2.0 * j) / ROT_DIM)
    ang = pos_ref[...].astype(F32) * inv_freq
    cos = jnp.cos(ang)
    sin = jnp.sin(ang)

    def place(t, e_ref):
        return lax.dot_general(t, e_ref[...], (((0,), (0,)), ((), ())),
                               precision=HIGHEST, preferred_element_type=F32)

    c_tab = place(cos, ec_ref) + base_ref[...]
    s_up = place(sin, eup_ref)
    s_dn = place(sin, edn_ref)

    def rope(t):
        n = t.shape[1]
        reps = n // LANES
        c = jnp.tile(c_tab, (1, reps)) if reps > 1 else c_tab
        su = jnp.tile(s_up, (1, reps)) if reps > 1 else s_up
        sd = jnp.tile(s_dn, (1, reps)) if reps > 1 else s_dn
        return t * c + pltpu.roll(t, n - ROT_HALF, 1) * su + pltpu.roll(t, ROT_HALF, 1) * sd

    q = _dot(hb, wq_ref[...])
    q_ref[...] = (rope(q) * (HEAD_DIM ** -0.5)).astype(BF16)
    k = _dot(hb, wk_ref[...])
    k_ref[...] = rope(k).astype(BF16)
    v_ref[...] = _dot(hb, wv_ref[...]).astype(BF16)
    z_ref[...] = _dot(hb, wz_ref[...]).astype(BF16)
    xbc_ref[...] = _dot(hb, wx_ref[...])
    dt_ref[...] = _dot(hb, wdt_ref[...])


def _rope_placement():
    d = np.arange(LANES) % HEAD_DIM
    jj = np.arange(ROT_HALF)[:, None]
    ec = ((d[None, :] < ROT_DIM) & ((d[None, :] % ROT_HALF) == jj)).astype(np.float32)
    eup = -((d[None, :] < ROT_HALF) & (d[None, :] == jj)).astype(np.float32)
    edn = ((d[None, :] >= ROT_HALF) & (d[None, :] < ROT_DIM) & ((d[None, :] - ROT_HALF) == jj)).astype(np.float32)
    base = (d >= ROT_DIM).astype(np.float32)[None, :]
    return jnp.asarray(ec), jnp.asarray(eup), jnp.asarray(edn), jnp.asarray(base)


def _in_proj(pos_row, x2d, norm_w, wq, wk, wv, wz, wx, wdt, tm):
    t, d = x2d.shape
    ec, eup, edn, base = _rope_placement()
    full = lambda a: pl.BlockSpec(a.shape, lambda i: (0,) * a.ndim)
    row = lambda n: pl.BlockSpec((tm, n), lambda i: (i, 0))
    outs = [(wq.shape[1], BF16), (wk.shape[1], BF16), (wv.shape[1], BF16), (wz.shape[1], BF16),
            (wx.shape[1], F32), (wdt.shape[1], F32)]
    return pl.pallas_call(
        _in_proj_kernel,
        grid=(t // tm,),
        in_specs=[pl.BlockSpec((1, tm), lambda i: (0, i)), row(d), full(norm_w),
                  full(wq), full(wk), full(wv), full(wz), full(wx), full(wdt),
                  full(ec), full(eup), full(edn), full(base)],
        out_specs=[row(n) for n, _ in outs],
        out_shape=[jax.ShapeDtypeStruct((t, n), dt) for n, dt in outs],
        compiler_params=_cparams(("parallel",)),
        name="in_proj",
    )(pos_row, x2d, norm_w, wq, wk, wv, wz, wx, wdt, ec, eup, edn, base)


def _swa_kernel(sink_ref, q_ref, kp_ref, kc_ref, vp_ref, vc_ref, nw_ref, o_ref, acc_ref):
    i = pl.program_id(1)
    n_q_heads = q_ref.shape[1] // HEAD_DIM
    q_per_kv = n_q_heads // N_KV_HEADS
    qi = lax.broadcasted_iota(jnp.int32, (ATTN_BLOCK, 2 * ATTN_BLOCK), 0) + ATTN_BLOCK
    ki = lax.broadcasted_iota(jnp.int32, (ATTN_BLOCK, 2 * ATTN_BLOCK), 1)
    rel = qi - ki
    mask = (rel >= 0) & (rel < WINDOW) & ((i > 0) | (ki >= ATTN_BLOCK))
    for kv in range(N_KV_HEADS):
        sl = slice(kv * HEAD_DIM, (kv + 1) * HEAD_DIM)
        kcat = jnp.concatenate([kp_ref[:, sl], kc_ref[:, sl]], axis=0)
        vcat = jnp.concatenate([vp_ref[:, sl], vc_ref[:, sl]], axis=0)
        for g in range(q_per_kv):
            h = kv * q_per_kv + g
            hs = slice(h * HEAD_DIM, (h + 1) * HEAD_DIM)
            s = jnp.where(mask, _dot_nt(q_ref[:, hs], kcat), -jnp.inf)
            sink = sink_ref[h]
            m = jnp.maximum(jnp.max(s, axis=-1, keepdims=True), sink)
            p = jnp.exp(s - m)
            denom = jnp.sum(p, axis=-1, keepdims=True) + jnp.exp(sink - m)
            acc_ref[:, hs] = _dot(p.astype(BF16), vcat) / denom
    o_ref[...] = _rms(acc_ref[...], nw_ref[...]).astype(BF16)


def _swa(sinks, q, k, v, norm_w, batch, seq):
    t, qw = q.shape
    kw = k.shape[1]
    nb = seq // ATTN_BLOCK
    cur = lambda b, i: (b * nb + i, 0)
    prev = lambda b, i: (b * nb + jnp.maximum(i - 1, 0), 0)
    return pl.pallas_call(
        _swa_kernel,
        grid=(batch, nb),
        in_specs=[pl.BlockSpec(memory_space=pltpu.SMEM),
                  pl.BlockSpec((ATTN_BLOCK, qw), cur),
                  pl.BlockSpec((ATTN_BLOCK, kw), prev), pl.BlockSpec((ATTN_BLOCK, kw), cur),
                  pl.BlockSpec((ATTN_BLOCK, kw), prev), pl.BlockSpec((ATTN_BLOCK, kw), cur),
                  pl.BlockSpec((1, qw), lambda b, i: (0, 0))],
        out_specs=pl.BlockSpec((ATTN_BLOCK, qw), cur),
        out_shape=jax.ShapeDtypeStruct((t, qw), BF16),
        scratch_shapes=[pltpu.VMEM((ATTN_BLOCK, qw), F32)],
        compiler_params=_cparams(("parallel", "parallel")),
        name="swa",
    )(sinks, q, k, k, v, v, norm_w)


def _ssd_kernel(xbc_ref, dt_ref, z_ref, cw_ref, cb_ref, dtb_ref, alog_ref, dskip_ref, nw_ref,
                tril_ref, o_ref, conv_ref, state_ref, y_ref):
    c = pl.program_id(1)
    L = SSD_CHUNK
    width = z_ref.shape[1]
    n_heads = width // SSD_HEAD_DIM
    heads_per_group = n_heads // SSD_GROUPS
    pad = SUBLANES

    @pl.when(c == 0)
    def _():
        state_ref[...] = jnp.zeros_like(state_ref)
        conv_ref[0:pad, :] = jnp.zeros((pad, conv_ref.shape[1]), F32)

    conv_ref[pad:pad + L, :] = xbc_ref[...]
    acc = cb_ref[...] + cw_ref[0:1, :] * conv_ref[pad - (SSD_CONV - 1):pad - (SSD_CONV - 1) + L, :]
    for jj in range(1, SSD_CONV):
        off = pad - (SSD_CONV - 1) + jj
        acc = acc + cw_ref[jj:jj + 1, :] * conv_ref[off:off + L, :]
    conv_ref[0:pad, :] = conv_ref[L:L + pad, :]
    u = jax.nn.silu(acc)
    xs = u[:, :width]
    bm = u[:, width:width + SSD_GROUPS * SSD_STATE]
    cm = u[:, width + SSD_GROUPS * SSD_STATE:]

    dt = jax.nn.softplus(dt_ref[...] + dtb_ref[...])
    da = dt * (-jnp.exp(alog_ref[...]))
    cum = jnp.dot(tril_ref[...], da, precision=HIGHEST, preferred_element_type=F32)
    cum_t = cum.T
    row = lax.broadcasted_iota(jnp.int32, (L, L), 0)
    col = lax.broadcasted_iota(jnp.int32, (L, L), 1)
    causal = row >= col

    for g in range(SSD_GROUPS):
        bg = bm[:, g * SSD_STATE:(g + 1) * SSD_STATE]
        cg = cm[:, g * SSD_STATE:(g + 1) * SSD_STATE].astype(BF16)
        cb = _dot_nt(cg, bg.astype(BF16))
        bgt = bg.T.astype(BF16)
        for r in range(heads_per_group):
            h = g * heads_per_group + r
            hs = slice(h * SSD_HEAD_DIM, (h + 1) * SSD_HEAD_DIM)
            cum_h = cum[:, h:h + 1]
            seg = cum_h - cum_t[h:h + 1, :]
            decay = jnp.exp(jnp.where(causal, seg, -jnp.inf))
            xs_h = xs[:, hs]
            xc = xs_h * dt[:, h:h + 1]
            st = state_ref[:, hs]
            y = _dot((cb * decay).astype(BF16), xc.astype(BF16))
            y = y + _dot(cg, st.astype(BF16)) * jnp.exp(cum_h)
            y_ref[:, hs] = y + dskip_ref[:, hs] * xs_h
            cum_last = cum[L - 1:L, h:h + 1]
            to_end = jnp.exp(cum_last - cum_h)
            state_ref[:, hs] = st * jnp.exp(cum_last) + _dot(bgt, (xc * to_end).astype(BF16))

    gated = y_ref[...] * jax.nn.silu(z_ref[...].astype(F32))
    gw = width // SSD_GROUPS
    parts = []
    for g in range(SSD_GROUPS):
        gg = gated[:, g * gw:(g + 1) * gw]
        parts.append(gg * lax.rsqrt(jnp.mean(gg * gg, axis=-1, keepdims=True) + RMS_EPS))
    o_ref[...] = (jnp.concatenate(parts, axis=1) * nw_ref[...]).astype(BF16)


def _ssd(xbc, dt_raw, z, conv_w, conv_b, dt_bias, a_log, d_skip, norm_w, batch, seq):
    t, cw = xbc.shape
    width = z.shape[1]
    nc = seq // SSD_CHUNK
    tril = jnp.asarray(np.tril(np.ones((SSD_CHUNK, SSD_CHUNK), np.float32)))
    cur = lambda b, c: (b * nc + c, 0)
    full = lambda a: pl.BlockSpec(a.shape, lambda b, c: (0,) * a.ndim)
    return pl.pallas_call(
        _ssd_kernel,
        grid=(batch, nc),
        in_specs=[pl.BlockSpec((SSD_CHUNK, cw), cur), pl.BlockSpec((SSD_CHUNK, LANES), cur),
                  pl.BlockSpec((SSD_CHUNK, width), cur),
                  full(conv_w), full(conv_b), full(dt_bias), full(a_log), full(d_skip), full(norm_w),
                  full(tril)],
        out_specs=pl.BlockSpec((SSD_CHUNK, width), cur),
        out_shape=jax.ShapeDtypeStruct((t, width), BF16),
        scratch_shapes=[pltpu.VMEM((SSD_CHUNK + 2 * SUBLANES, cw), F32),
                        pltpu.VMEM((SSD_STATE, width), F32),
                        pltpu.VMEM((SSD_CHUNK, width), F32)],
        compiler_params=_cparams(("parallel", "arbitrary")),
        name="ssd",
    )(xbc, dt_raw, z, conv_w, conv_b, dt_bias, a_log, d_skip, norm_w, tril)


def _mem_kv_kernel(m_ref, nw_ref, w_ref, o_ref):
    o_ref[...] = _dot(_rms(m_ref[...], nw_ref[...]).astype(BF16), w_ref[...]).astype(BF16)


def _mem_kv(mem2d, norm_w, w_kv, rows):
    t, d = mem2d.shape
    n = w_kv.shape[1]
    return pl.pallas_call(
        _mem_kv_kernel,
        grid=(t // rows,),
        in_specs=[pl.BlockSpec((rows, d), lambda i: (i, 0)),
                  pl.BlockSpec((1, d), lambda i: (0, 0)),
                  pl.BlockSpec((d, n), lambda i: (0, 0))],
        out_specs=pl.BlockSpec((rows, n), lambda i: (i, 0)),
        out_shape=jax.ShapeDtypeStruct((t, n), BF16),
        compiler_params=_cparams(("parallel",)),
        name="mem_kv",
    )(mem2d, norm_w, w_kv)


def _post_kernel(x_ref, a_ref, s_ref, woa_ref, wos_ref, xnw_ref, wq_ref, kv_ref, wo_ref,
                 fnw_ref, wr_ref, br_ref, tril_ref,
                 x2_ref, pay_ref, meta_ref, cnt_ref, carry_ref):
    step = pl.program_id(0)
    tm = x_ref.shape[0]
    n_exp = N_GROUPS * EXPERTS_PER_GROUP

    @pl.when(step == 0)
    def _():
        carry_ref[...] = jnp.zeros_like(carry_ref)

    x1 = x_ref[...] + _dot(a_ref[...], woa_ref[...]) + _dot(s_ref[...], wos_ref[...])

    xw = wq_ref.shape[1]
    hd = xw // XATTN_HEADS
    q = (_dot(_rms(x1, xnw_ref[...]).astype(BF16), wq_ref[...]) * (hd ** -0.5)).astype(BF16)
    heads = []
    for h in range(XATTN_HEADS):
        s = _dot_nt(q[:, h * hd:(h + 1) * hd], kv_ref[:, h * hd:(h + 1) * hd])
        p = jnp.exp(s - jnp.max(s, axis=-1, keepdims=True))
        o = _dot(p.astype(BF16), kv_ref[:, xw + h * hd:xw + (h + 1) * hd])
        heads.append((o / jnp.sum(p, axis=-1, keepdims=True)).astype(BF16))
    x2 = x1 + _dot(jnp.concatenate(heads, axis=1), wo_ref[...])
    x2_ref[...] = x2

    h3 = _rms(x2, fnw_ref[...])
    logits = jnp.dot(h3, wr_ref[...], precision=HIGHEST, preferred_element_type=F32) + br_ref[...]
    lane = lax.broadcasted_iota(jnp.int32, (tm, LANES), 1)
    big = jnp.int32(LANES)
    neg = -jnp.inf
    g_l = jnp.where((lane >= n_exp) & (lane < n_exp + N_GROUPS), logits, neg)
    g_max = jnp.max(g_l, axis=-1, keepdims=True)
    g_idx = jnp.min(jnp.where(g_l == g_max, lane - n_exp, big), axis=-1, keepdims=True)
    g_gate = 1.0 / jnp.sum(jnp.exp(g_l - g_max), axis=-1, keepdims=True)
    e_l = jnp.where((lane < n_exp) & ((lane // EXPERTS_PER_GROUP) == g_idx), logits, neg)
    m1 = jnp.max(e_l, axis=-1, keepdims=True)
    i1 = jnp.min(jnp.where(e_l == m1, lane, big), axis=-1, keepdims=True)
    e_l2 = jnp.where(lane == i1, neg, e_l)
    m2 = jnp.max(e_l2, axis=-1, keepdims=True)
    i2 = jnp.min(jnp.where(e_l2 == m2, lane, big), axis=-1, keepdims=True)
    e2 = jnp.exp(m2 - m1)
    w1 = (1.0 / (1.0 + e2)) * g_gate
    w2 = (e2 / (1.0 + e2)) * g_gate
    first_low = i1 < i2
    lo = jnp.where(first_low, i1, i2) % EXPERTS_PER_GROUP
    hi = jnp.where(first_low, i2, i1) % EXPERTS_PER_GROUP
    w_lo = jnp.where(first_low, w1, w2)
    w_hi = jnp.where(first_low, w2, w1)
    bucket = g_idx * (EXPERTS_PER_GROUP * EXPERTS_PER_GROUP) + lo * EXPERTS_PER_GROUP + hi

    blane = lax.broadcasted_iota(jnp.int32, (tm, N_BUCKETS), 1)
    onehot = (blane == bucket).astype(F32)
    before = _dot(tril_ref[...], onehot.astype(BF16)) + carry_ref[...]
    rank = jnp.sum(onehot * before, axis=-1, keepdims=True)
    carry_ref[...] = carry_ref[...] + jnp.sum(onehot, axis=0, keepdims=True)
    cnt_ref[...] = carry_ref[...]

    meta = jnp.where(lane == 0, w_lo, jnp.where(lane == 1, w_hi, jnp.where(
        lane == 2, bucket.astype(F32), jnp.where(lane == 3, rank, 0.0))))
    pay_ref[:, :h3.shape[1]] = h3
    pay_ref[:, h3.shape[1]:] = meta
    meta_ref[...] = meta.T[:SUBLANES, :]


def _post(x2d, attn, ssd, woa, wos, xnw, wq, kv, wo, fnw, wr, br, tm, seq, mem_tokens):
    t, d = x2d.shape
    tiles_per_batch = seq // tm
    tril = jnp.asarray(np.tril(np.ones((tm, tm), np.float32), -1), dtype=BF16)
    full = lambda a: pl.BlockSpec(a.shape, lambda i: (0,) * a.ndim)
    row = lambda n: pl.BlockSpec((tm, n), lambda i: (i, 0))
    return pl.pallas_call(
        _post_kernel,
        grid=(t // tm,),
        in_specs=[row(d), row(attn.shape[1]), row(ssd.shape[1]), full(woa), full(wos), full(xnw), full(wq),
                  pl.BlockSpec((mem_tokens, kv.shape[1]), lambda i: (i // tiles_per_batch, 0)),
                  full(wo), full(fnw), full(wr), full(br), full(tril)],
        out_specs=[row(d), row(d + META_LANES), pl.BlockSpec((SUBLANES, tm), lambda i: (0, i)),
                   pl.BlockSpec((1, N_BUCKETS), lambda i: (0, 0))],
        out_shape=[jax.ShapeDtypeStruct((t, d), F32), jax.ShapeDtypeStruct((t, d + META_LANES), F32),
                   jax.ShapeDtypeStruct((SUBLANES, t), F32), jax.ShapeDtypeStruct((1, N_BUCKETS), F32)],
        scratch_shapes=[pltpu.VMEM((1, N_BUCKETS), F32)],
        compiler_params=_cparams(("arbitrary",)),
        name="post",
    )(x2d, attn, ssd, woa, wos, xnw, wq, kv, wo, fnw, wr, br, tril)


def _row_copy(src_hbm, src_row, dst_hbm, dst_row, sem):
    return pltpu.make_async_copy(src_hbm.at[pl.ds(src_row, 1)], dst_hbm.at[pl.ds(dst_row, 1)], sem)


def _dispatch_kernel(dest_ref, pay_hbm, xs_hbm, sem):
    step = pl.program_id(0)
    rows = dest_ref.shape[-1]
    base = step * rows

    def issue(r, carry):
        _row_copy(pay_hbm, base + r, xs_hbm, dest_ref[0, r], sem).start()
        return carry

    lax.fori_loop(0, rows, issue, 0)

    def drain(r, carry):
        _row_copy(pay_hbm, base + r, xs_hbm, dest_ref[0, r], sem).wait()
        return carry

    lax.fori_loop(0, rows, drain, 0)


def _dispatch(dest3d, payload, n_sorted):
    n_steps, _, rows = dest3d.shape
    return pl.pallas_call(
        _dispatch_kernel,
        grid=(n_steps,),
        in_specs=[pl.BlockSpec((None, 1, rows), lambda i: (i, 0, 0), memory_space=pltpu.SMEM),
                  pl.BlockSpec(memory_space=pl.ANY)],
        out_specs=pl.BlockSpec(memory_space=pl.ANY),
        out_shape=jax.ShapeDtypeStruct((n_sorted, payload.shape[1]), payload.dtype),
        scratch_shapes=[pltpu.SemaphoreType.DMA(())],
        compiler_params=_cparams(("arbitrary",)),
        name="dispatch",
    )(dest3d, payload)


def _experts_kernel(grp_ref, ea_ref, eb_ref, blk_ref, valid_ref, xs_ref, wgu_ref, wd_ref, y_ref):
    i = pl.program_id(0)
    d = y_ref.shape[1]
    ff = wd_ref.shape[1]

    @pl.when(valid_ref[i] == 1)
    def _():
        x = xs_ref[:, :d].astype(BF16)
        y = None
        for e_ref, lane in ((ea_ref, 0), (eb_ref, 1)):
            e = e_ref[i]
            gu = _dot(x, wgu_ref[e])
            hid = (jax.nn.silu(gu[:, :ff]) * gu[:, ff:]).astype(BF16)
            part = xs_ref[:, d + lane:d + lane + 1] * _dot(hid, wd_ref[e])
            y = part if y is None else y + part
        y_ref[...] = y


def _experts(tile_group, tile_a, tile_b, tile_blk, tile_valid, xs, wgu, wd):
    n_sorted, pw = xs.shape
    d = wd.shape[2]
    nt = n_sorted // MOE_TILE
    grid_spec = pltpu.PrefetchScalarGridSpec(
        num_scalar_prefetch=5,
        grid=(nt,),
        in_specs=[pl.BlockSpec((MOE_TILE, pw), lambda i, g, a, b, k, v: (k[i], 0)),
                  pl.BlockSpec((EXPERTS_PER_GROUP,) + wgu.shape[1:], lambda i, g, a, b, k, v: (g[i], 0, 0)),
                  pl.BlockSpec((EXPERTS_PER_GROUP,) + wd.shape[1:], lambda i, g, a, b, k, v: (g[i], 0, 0))],
        out_specs=pl.BlockSpec((MOE_TILE, d), lambda i, g, a, b, k, v: (k[i], 0)),
    )
    return pl.pallas_call(
        _experts_kernel,
        grid_spec=grid_spec,
        out_shape=jax.ShapeDtypeStruct((n_sorted, d), F32),
        compiler_params=_cparams(("arbitrary",)),
        name="experts",
    )(tile_group, tile_a, tile_b, tile_blk, tile_valid, xs, wgu, wd)


def _combine_kernel(dest_ref, x2_ref, nw_ref, ys_hbm, o_ref, buf_ref, sem):
    rows = x2_ref.shape[0]

    def copy(r):
        return pltpu.make_async_copy(ys_hbm.at[pl.ds(dest_ref[0, r], 1)], buf_ref.at[pl.ds(r, 1)], sem)

    def issue(r, carry):
        copy(r).start()
        return carry

    lax.fori_loop(0, rows, issue, 0)

    def drain(r, carry):
        copy(r).wait()
        return carry

    lax.fori_loop(0, rows, drain, 0)
    o_ref[...] = _rms(x2_ref[...] + buf_ref[...], nw_ref[...])


def _combine(dest3d, x2, norm_w, ys):
    n_steps, _, rows = dest3d.shape
    t, d = x2.shape
    return pl.pallas_call(
        _combine_kernel,
        grid=(n_steps,),
        in_specs=[pl.BlockSpec((None, 1, rows), lambda i: (i, 0, 0), memory_space=pltpu.SMEM),
                  pl.BlockSpec((rows, d), lambda i: (i, 0)),
                  pl.BlockSpec((1, d), lambda i: (0, 0)),
                  pl.BlockSpec(memory_space=pl.ANY)],
        out_specs=pl.BlockSpec((rows, d), lambda i: (i, 0)),
        out_shape=jax.ShapeDtypeStruct((t, d), F32),
        scratch_shapes=[pltpu.VMEM((rows, d), F32), pltpu.SemaphoreType.DMA(())],
        compiler_params=_cparams(("arbitrary",)),
        name="combine",
    )(dest3d, x2, norm_w, ys)


def _pad_cols(w, n):
    return jnp.pad(w, ((0, 0), (0, n - w.shape[1])))


def _tile_plan(counts, n_tiles):
    per_bucket = (counts + (MOE_TILE - 1)) // MOE_TILE
    tile_end = jnp.cumsum(per_bucket)
    tile_start = tile_end - per_bucket
    total = tile_end[-1]
    ids = jnp.arange(n_tiles, dtype=jnp.int32)
    blk = jnp.minimum(ids, total - 1)
    bucket = jnp.sum((tile_end[None, :] <= blk[:, None]).astype(jnp.int32), axis=1)
    pair = EXPERTS_PER_GROUP * EXPERTS_PER_GROUP
    group = bucket // pair
    slot_a = (bucket % pair) // EXPERTS_PER_GROUP
    slot_b = bucket % EXPERTS_PER_GROUP
    valid = (ids < total).astype(jnp.int32)
    return tile_start * MOE_TILE, group, slot_a, slot_b, blk, valid


def kernel(x, mem, positions, mix_norm_w, w_in, attn_sinks, ssd_conv_w, ssd_conv_b, ssd_dt_bias, ssd_a_log, ssd_d, attn_out_norm_w, ssd_out_norm_w, w_out, xattn_norm_w, mem_norm_w, xattn_w_q, xattn_w_kv, xattn_w_o, ffn_norm_w, router_group_w, router_group_b, router_expert_w, router_expert_b, expert_w_gate, expert_w_up, expert_w_down, final_norm_w):
    batch, seq, d = x.shape
    mem_tokens = mem.shape[1]
    depth = w_in.shape[0]
    t = batch * seq
    attn_w = attn_out_norm_w.shape[1]
    ssd_w = ssd_out_norm_w.shape[1]
    kv_w = N_KV_HEADS * HEAD_DIM
    conv_dim = ssd_conv_w.shape[2]
    n_ssd_heads = ssd_dt_bias.shape[1]
    n_exp = router_expert_w.shape[2]
    assert n_exp == N_GROUPS * EXPERTS_PER_GROUP and router_group_w.shape[2] == N_GROUPS
    assert seq % ATTN_BLOCK == 0 and seq % SSD_CHUNK == 0 and t % MOE_TILE == 0
    tm = min(512, seq)
    rows_per_step = min(2048, t)
    n_tiles = t // MOE_TILE + N_USED_BUCKETS
    n_sorted = n_tiles * MOE_TILE

    pos_row = positions.reshape(1, t)
    x2d = x.reshape(t, d)
    mem2d = mem.reshape(batch * mem_tokens, d)
    row = lambda v: v.reshape(1, -1)

    for l in range(depth):
        w = w_in[l].astype(BF16)
        o = 0
        parts = []
        for n in (attn_w, kv_w, kv_w, ssd_w, conv_dim, n_ssd_heads):
            parts.append(w[:, o:o + n])
            o += n
        wq, wk, wv, wz, wx, wdt = parts
        q, k, v, z, xbc, dt_raw = _in_proj(pos_row, x2d, row(mix_norm_w[l]), wq, wk, wv, wz, wx,
                                           _pad_cols(wdt, LANES), tm)
        attn = _swa(attn_sinks[l], q, k, v, row(attn_out_norm_w[l]), batch, seq)
        ssd = _ssd(xbc, dt_raw, z, ssd_conv_w[l], row(ssd_conv_b[l]),
                   _pad_cols(row(ssd_dt_bias[l]), LANES), _pad_cols(row(ssd_a_log[l]), LANES),
                   row(jnp.repeat(ssd_d[l], SSD_HEAD_DIM)), row(ssd_out_norm_w[l]), batch, seq)
        kv = _mem_kv(mem2d, row(mem_norm_w[l]), xattn_w_kv[l].astype(BF16), mem_tokens)

        wo = w_out[l].astype(BF16)
        wr = _pad_cols(jnp.concatenate([router_expert_w[l], router_group_w[l]], axis=1), LANES)
        br = _pad_cols(row(jnp.concatenate([router_expert_b[l], router_group_b[l]])), LANES)
        x2, payload, meta, counts = _post(
            x2d, attn, ssd, wo[:attn_w], wo[attn_w:], row(xattn_norm_w[l]), xattn_w_q[l].astype(BF16), kv,
            xattn_w_o[l].astype(BF16), row(ffn_norm_w[l]), wr, br, tm, seq, mem_tokens)

        row_start, tile_group, tile_a, tile_b, tile_blk, tile_valid = _tile_plan(
            counts[0].astype(jnp.int32), n_tiles)
        bucket = meta[2].astype(jnp.int32)
        dest = (row_start[bucket] + meta[3].astype(jnp.int32)).reshape(t // rows_per_step, 1, rows_per_step)

        xs = _dispatch(dest, payload, n_sorted)
        wgu = jnp.concatenate([expert_w_gate[l], expert_w_up[l]], axis=2).astype(BF16)
        ys = _experts(tile_group, tile_a, tile_b, tile_blk, tile_valid, xs, wgu, expert_w_down[l].astype(BF16))
        assert depth == 1
        x2d = _combine(dest, x2, row(final_norm_w), ys)

    return x2d.reshape(batch, seq, d)
```

```python
import functools

import numpy as np
import jax
import jax.numpy as jnp
from jax import lax
from jax.experimental import pallas as pl
from jax.experimental.pallas import tpu as pltpu

RMS_EPS = 1e-5
HEAD_DIM = 64
N_KV_HEADS = 2
WINDOW = 128
ATTN_BLOCK = 128
ROT_DIM = 16
ROT_HALF = ROT_DIM // 2
ROPE_THETA = 500000.0
SSD_HEAD_DIM = 64
SSD_GROUPS = 2
SSD_STATE = 128
SSD_CONV = 4
SSD_CHUNK = 128
XATTN_HEADS = 4
N_GROUPS = 4
EXPERTS_PER_GROUP = 8

LANES = 128
SUBLANES = 8
MOE_TILE = 256
SLAB_PITCH = SUBLANES + 1
N_BUCKETS = N_GROUPS * EXPERTS_PER_GROUP * EXPERTS_PER_GROUP
N_USED_BUCKETS = N_GROUPS * (EXPERTS_PER_GROUP * (EXPERTS_PER_GROUP - 1) // 2)
VMEM_LIMIT = 56 * 1024 * 1024

HIGHEST = lax.Precision.HIGHEST
F32 = jnp.float32
BF16 = jnp.bfloat16


def _cparams(semantics):
    return pltpu.CompilerParams(dimension_semantics=semantics, vmem_limit_bytes=VMEM_LIMIT)


def _rms(x, w):
    return x * lax.rsqrt(jnp.mean(x * x, axis=-1, keepdims=True) + RMS_EPS) * w


def _dot(a, b):
    return jnp.dot(a, b, preferred_element_type=F32)


def _dot_nt(a, b):
    return lax.dot_general(a, b, (((1,), (1,)), ((), ())), preferred_element_type=F32)


def _in_proj_kernel(pos_ref, x_ref, nw_ref, wq_ref, wk_ref, wv_ref, wz_ref, wx_ref, wdt_ref,
                    ec_ref, eup_ref, edn_ref, base_ref,
                    q_ref, k_ref, v_ref, z_ref, xbc_ref, dt_ref):
    x = x_ref[...]
    hb = _rms(x, nw_ref[...]).astype(BF16)

    j = lax.broadcasted_iota(jnp.int32, (ROT_HALF, 1), 0).astype(F32)
    inv_freq = jnp.power(jnp.float32(ROPE_THETA), -(2.0 * j) / ROT_DIM)
    ang = pos_ref[...].astype(F32) * inv_freq
    cos = jnp.cos(ang)
    sin = jnp.sin(ang)

    def place(t, e_ref):
        return lax.dot_general(t, e_ref[...], (((0,), (0,)), ((), ())),
                               precision=HIGHEST, preferred_element_type=F32)

    c_tab = place(cos, ec_ref) + base_ref[...]
    s_up = place(sin, eup_ref)
    s_dn = place(sin, edn_ref)

    def rope(t):
        n = t.shape[1]
        reps = n // LANES
        c = jnp.tile(c_tab, (1, reps)) if reps > 1 else c_tab
        su = jnp.tile(s_up, (1, reps)) if reps > 1 else s_up
        sd = jnp.tile(s_dn, (1, reps)) if reps > 1 else s_dn
        return t * c + pltpu.roll(t, n - ROT_HALF, 1) * su + pltpu.roll(t, ROT_HALF, 1) * sd

    q = _dot(hb, wq_ref[...])
    q_ref[...] = (rope(q) * (HEAD_DIM ** -0.5)).astype(BF16)
    k = _dot(hb, wk_ref[...])
    k_ref[...] = rope(k).astype(BF16)
    v_ref[...] = _dot(hb, wv_ref[...]).astype(BF16)
    z_ref[...] = _dot(hb, wz_ref[...]).astype(BF16)
    xbc_ref[...] = _dot(hb, wx_ref[...])
    dt_ref[...] = _dot(hb, wdt_ref[...])


def _rope_placement():
    d = np.arange(LANES) % HEAD_DIM
    jj = np.arange(ROT_HALF)[:, None]
    ec = ((d[None, :] < ROT_DIM) & ((d[None, :] % ROT_HALF) == jj)).astype(np.float32)
    eup = -((d[None, :] < ROT_HALF) & (d[None, :] == jj)).astype(np.float32)
    edn = ((d[None, :] >= ROT_HALF) & (d[None, :] < ROT_DIM) & ((d[None, :] - ROT_HALF) == jj)).astype(np.float32)
    base = (d >= ROT_DIM).astype(np.float32)[None, :]
    return jnp.asarray(ec), jnp.asarray(eup), jnp.asarray(edn), jnp.asarray(base)


def _in_proj(pos_row, x2d, norm_w, wq, wk, wv, wz, wx, wdt, tm):
    t, d = x2d.shape
    ec, eup, edn, base = _rope_placement()
    full = lambda a: pl.BlockSpec(a.shape, lambda i: (0,) * a.ndim)
    row = lambda n: pl.BlockSpec((tm, n), lambda i: (i, 0))
    outs = [(wq.shape[1], BF16), (wk.shape[1], BF16), (wv.shape[1], BF16), (wz.shape[1], BF16),
            (wx.shape[1], F32), (wdt.shape[1], F32)]
    return pl.pallas_call(
        _in_proj_kernel,
        grid=(t // tm,),
        in_specs=[pl.BlockSpec((1, tm), lambda i: (0, i)), row(d), full(norm_w),
                  full(wq), full(wk), full(wv), full(wz), full(wx), full(wdt),
                  full(ec), full(eup), full(edn), full(base)],
        out_specs=[row(n) for n, _ in outs],
        out_shape=[jax.ShapeDtypeStruct((t, n), dt) for n, dt in outs],
        compiler_params=_cparams(("parallel",)),
        name="in_proj",
    )(pos_row, x2d, norm_w, wq, wk, wv, wz, wx, wdt, ec, eup, edn, base)


def _swa_kernel(sink_ref, q_ref, kp_ref, kc_ref, vp_ref, vc_ref, nw_ref, o_ref, acc_ref):
    i = pl.program_id(1)
    n_q_heads = q_ref.shape[1] // HEAD_DIM
    q_per_kv = n_q_heads // N_KV_HEADS
    qi = lax.broadcasted_iota(jnp.int32, (ATTN_BLOCK, 2 * ATTN_BLOCK), 0) + ATTN_BLOCK
    ki = lax.broadcasted_iota(jnp.int32, (ATTN_BLOCK, 2 * ATTN_BLOCK), 1)
    rel = qi - ki
    mask = (rel >= 0) & (rel < WINDOW) & ((i > 0) | (ki >= ATTN_BLOCK))
    for kv in range(N_KV_HEADS):
        sl = slice(kv * HEAD_DIM, (kv + 1) * HEAD_DIM)
        kcat = jnp.concatenate([kp_ref[:, sl], kc_ref[:, sl]], axis=0)
        vcat = jnp.concatenate([vp_ref[:, sl], vc_ref[:, sl]], axis=0)
        for g in range(q_per_kv):
            h = kv * q_per_kv + g
            hs = slice(h * HEAD_DIM, (h + 1) * HEAD_DIM)
            s = jnp.where(mask, _dot_nt(q_ref[:, hs], kcat), -jnp.inf)
            sink = sink_ref[h]
            m = jnp.maximum(jnp.max(s, axis=-1, keepdims=True), sink)
            p = jnp.exp(s - m)
            denom = jnp.sum(p, axis=-1, keepdims=True) + jnp.exp(sink - m)
            acc_ref[:, hs] = _dot(p.astype(BF16), vcat) / denom
    o_ref[...] = _rms(acc_ref[...], nw_ref[...]).astype(BF16)


def _swa(sinks, q, k, v, norm_w, batch, seq):
    t, qw = q.shape
    kw = k.shape[1]
    nb = seq // ATTN_BLOCK
    cur = lambda b, i: (b * nb + i, 0)
    prev = lambda b, i: (b * nb + jnp.maximum(i - 1, 0), 0)
    return pl.pallas_call(
        _swa_kernel,
        grid=(batch, nb),
        in_specs=[pl.BlockSpec(memory_space=pltpu.SMEM),
                  pl.BlockSpec((ATTN_BLOCK, qw), cur),
                  pl.BlockSpec((ATTN_BLOCK, kw), prev), pl.BlockSpec((ATTN_BLOCK, kw), cur),
                  pl.BlockSpec((ATTN_BLOCK, kw), prev), pl.BlockSpec((ATTN_BLOCK, kw), cur),
                  pl.BlockSpec((1, qw), lambda b, i: (0, 0))],
        out_specs=pl.BlockSpec((ATTN_BLOCK, qw), cur),
        out_shape=jax.ShapeDtypeStruct((t, qw), BF16),
        scratch_shapes=[pltpu.VMEM((ATTN_BLOCK, qw), F32)],
        compiler_params=_cparams(("parallel", "parallel")),
        name="swa",
    )(sinks, q, k, k, v, v, norm_w)


def _ssd_kernel(xbc_ref, dt_ref, z_ref, cw_ref, cb_ref, dtb_ref, alog_ref, dskip_ref, nw_ref,
                tril_ref, o_ref, conv_ref, state_ref, y_ref):
    c = pl.program_id(1)
    L = SSD_CHUNK
    width = z_ref.shape[1]
    n_heads = width // SSD_HEAD_DIM
    heads_per_group = n_heads // SSD_GROUPS
    pad = SUBLANES

    @pl.when(c == 0)
    def _():
        state_ref[...] = jnp.zeros_like(state_ref)
        conv_ref[0:pad, :] = jnp.zeros((pad, conv_ref.shape[1]), F32)

    conv_ref[pad:pad + L, :] = xbc_ref[...]
    acc = cb_ref[...] + cw_ref[0:1, :] * conv_ref[pad - (SSD_CONV - 1):pad - (SSD_CONV - 1) + L, :]
    for jj in range(1, SSD_CONV):
        off = pad - (SSD_CONV - 1) + jj
        acc = acc + cw_ref[jj:jj + 1, :] * conv_ref[off:off + L, :]
    conv_ref[0:pad, :] = conv_ref[L:L + pad, :]
    u = jax.nn.silu(acc)
    xs = u[:, :width]
    bm = u[:, width:width + SSD_GROUPS * SSD_STATE]
    cm = u[:, width + SSD_GROUPS * SSD_STATE:]

    dt = jax.nn.softplus(dt_ref[...] + dtb_ref[...])
    da = dt * (-jnp.exp(alog_ref[...]))
    cum = jnp.dot(tril_ref[...], da, precision=HIGHEST, preferred_element_type=F32)
    cum_t = cum.T
    row = lax.broadcasted_iota(jnp.int32, (L, L), 0)
    col = lax.broadcasted_iota(jnp.int32, (L, L), 1)
    causal = row >= col

    for g in range(SSD_GROUPS):
        bg = bm[:, g * SSD_STATE:(g + 1) * SSD_STATE]
        cg = cm[:, g * SSD_STATE:(g + 1) * SSD_STATE].astype(BF16)
        cb = _dot_nt(cg, bg.astype(BF16))
        bgt = bg.T.astype(BF16)
        for r in range(heads_per_group):
            h = g * heads_per_group + r
            hs = slice(h * SSD_HEAD_DIM, (h + 1) * SSD_HEAD_DIM)
            cum_h = cum[:, h:h + 1]
            seg = cum_h - cum_t[h:h + 1, :]
            decay = jnp.exp(jnp.where(causal, seg, -jnp.inf))
            xs_h = xs[:, hs]
            xc = xs_h * dt[:, h:h + 1]
            st = state_ref[:, hs]
            y = _dot((cb * decay).astype(BF16), xc.astype(BF16))
            y = y + _dot(cg, st.astype(BF16)) * jnp.exp(cum_h)
            y_ref[:, hs] = y + dskip_ref[:, hs] * xs_h
            cum_last = cum[L - 1:L, h:h + 1]
            to_end = jnp.exp(cum_last - cum_h)
            state_ref[:, hs] = st * jnp.exp(cum_last) + _dot(bgt, (xc * to_end).astype(BF16))

    gated = y_ref[...] * jax.nn.silu(z_ref[...].astype(F32))
    gw = width // SSD_GROUPS
    parts = []
    for g in range(SSD_GROUPS):
        gg = gated[:, g * gw:(g + 1) * gw]
        parts.append(gg * lax.rsqrt(jnp.mean(gg * gg, axis=-1, keepdims=True) + RMS_EPS))
    o_ref[...] = (jnp.concatenate(parts, axis=1) * nw_ref[...]).astype(BF16)


def _ssd(xbc, dt_raw, z, conv_w, conv_b, dt_bias, a_log, d_skip, norm_w, batch, seq):
    t, cw = xbc.shape
    width = z.shape[1]
    nc = seq // SSD_CHUNK
    tril = jnp.asarray(np.tril(np.ones((SSD_CHUNK, SSD_CHUNK), np.float32)))
    cur = lambda b, c: (b * nc + c, 0)
    full = lambda a: pl.BlockSpec(a.shape, lambda b, c: (0,) * a.ndim)
    return pl.pallas_call(
        _ssd_kernel,
        grid=(batch, nc),
        in_specs=[pl.BlockSpec((SSD_CHUNK, cw), cur), pl.BlockSpec((SSD_CHUNK, LANES), cur),
                  pl.BlockSpec((SSD_CHUNK, width), cur),
                  full(conv_w), full(conv_b), full(dt_bias), full(a_log), full(d_skip), full(norm_w),
                  full(tril)],
        out_specs=pl.BlockSpec((SSD_CHUNK, width), cur),
        out_shape=jax.ShapeDtypeStruct((t, width), BF16),
        scratch_shapes=[pltpu.VMEM((SSD_CHUNK + 2 * SUBLANES, cw), F32),
                        pltpu.VMEM((SSD_STATE, width), F32),
                        pltpu.VMEM((SSD_CHUNK, width), F32)],
        compiler_params=_cparams(("parallel", "arbitrary")),
        name="ssd",
    )(xbc, dt_raw, z, conv_w, conv_b, dt_bias, a_log, d_skip, norm_w, tril)


def _mem_kv_kernel(m_ref, nw_ref, w_ref, o_ref):
    o_ref[...] = _dot(_rms(m_ref[...], nw_ref[...]).astype(BF16), w_ref[...]).astype(BF16)


def _mem_kv(mem2d, norm_w, w_kv, rows):
    t, d = mem2d.shape
    n = w_kv.shape[1]
    return pl.pallas_call(
        _mem_kv_kernel,
        grid=(t // rows,),
        in_specs=[pl.BlockSpec((rows, d), lambda i: (i, 0)),
                  pl.BlockSpec((1, d), lambda i: (0, 0)),
                  pl.BlockSpec((d, n), lambda i: (0, 0))],
        out_specs=pl.BlockSpec((rows, n), lambda i: (i, 0)),
        out_shape=jax.ShapeDtypeStruct((t, n), BF16),
        compiler_params=_cparams(("parallel",)),
        name="mem_kv",
    )(mem2d, norm_w, w_kv)


def _post_kernel(x_ref, a_ref, s_ref, woa_ref, wos_ref, xnw_ref, wq_ref, kv_ref, wo_ref,
                 fnw_ref, wr_ref, br_ref, tril_ref,
                 x2_ref, pay_ref, meta_ref, cnt_ref, carry_ref):
    step = pl.program_id(0)
    tm = x_ref.shape[0]
    n_exp = N_GROUPS * EXPERTS_PER_GROUP

    @pl.when(step == 0)
    def _():
        carry_ref[...] = jnp.zeros_like(carry_ref)

    x1 = x_ref[...] + _dot(a_ref[...], woa_ref[...]) + _dot(s_ref[...], wos_ref[...])

    xw = wq_ref.shape[1]
    hd = xw // XATTN_HEADS
    q = (_dot(_rms(x1, xnw_ref[...]).astype(BF16), wq_ref[...]) * (hd ** -0.5)).astype(BF16)
    heads = []
    for h in range(XATTN_HEADS):
        s = _dot_nt(q[:, h * hd:(h + 1) * hd], kv_ref[:, h * hd:(h + 1) * hd])
        p = jnp.exp(s - jnp.max(s, axis=-1, keepdims=True))
        o = _dot(p.astype(BF16), kv_ref[:, xw + h * hd:xw + (h + 1) * hd])
        heads.append((o / jnp.sum(p, axis=-1, keepdims=True)).astype(BF16))
    x2 = x1 + _dot(jnp.concatenate(heads, axis=1), wo_ref[...])
    x2_ref[...] = x2

    h3 = _rms(x2, fnw_ref[...])
    h_hi = h3.astype(BF16)
    h_lo = (h3 - h_hi.astype(F32)).astype(BF16)
    t_hi = _dot(h_hi, wr_ref[...])
    logits = t_hi[:, :LANES] + (t_hi[:, LANES:] + _dot(h_lo, wr_ref[:, :LANES])) + br_ref[...]
    lane = lax.broadcasted_iota(jnp.int32, (tm, LANES), 1)
    big = jnp.int32(LANES)
    neg = -jnp.inf
    g_l = jnp.where((lane >= n_exp) & (lane < n_exp + N_GROUPS), logits, neg)
    g_max = jnp.max(g_l, axis=-1, keepdims=True)
    g_idx = jnp.min(jnp.where(g_l == g_max, lane - n_exp, big), axis=-1, keepdims=True)
    g_gate = 1.0 / jnp.sum(jnp.exp(g_l - g_max), axis=-1, keepdims=True)
    e_l = jnp.where((lane < n_exp) & ((lane // EXPERTS_PER_GROUP) == g_idx), logits, neg)
    m1 = jnp.max(e_l, axis=-1, keepdims=True)
    i1 = jnp.min(jnp.where(e_l == m1, lane, big), axis=-1, keepdims=True)
    e_l2 = jnp.where(lane == i1, neg, e_l)
    m2 = jnp.max(e_l2, axis=-1, keepdims=True)
    i2 = jnp.min(jnp.where(e_l2 == m2, lane, big), axis=-1, keepdims=True)
    e2 = jnp.exp(m2 - m1)
    w1 = (1.0 / (1.0 + e2)) * g_gate
    w2 = (e2 / (1.0 + e2)) * g_gate
    first_low = i1 < i2
    lo = jnp.where(first_low, i1, i2) % EXPERTS_PER_GROUP
    hi = jnp.where(first_low, i2, i1) % EXPERTS_PER_GROUP
    w_lo = jnp.where(first_low, w1, w2)
    w_hi = jnp.where(first_low, w2, w1)
    bucket = g_idx * (EXPERTS_PER_GROUP * EXPERTS_PER_GROUP) + lo * EXPERTS_PER_GROUP + hi

    blane = lax.broadcasted_iota(jnp.int32, (tm, N_BUCKETS), 1)
    onehot = (blane == bucket).astype(F32)
    before = _dot(tril_ref[...], onehot.astype(BF16)) + carry_ref[...]
    rank = jnp.sum(onehot * before, axis=-1, keepdims=True)
    carry_ref[...] = carry_ref[...] + jnp.sum(onehot, axis=0, keepdims=True)
    cnt_ref[...] = carry_ref[...]

    meta = jnp.where(lane == 0, w_lo, jnp.where(lane == 1, w_hi, jnp.where(
        lane == 2, bucket.astype(F32), jnp.where(lane == 3, rank, 0.0))))
    n_feat_rows = h3.shape[1] // LANES
    for c in range(n_feat_rows):
        pay_ref[pl.ds(c, tm, stride=SLAB_PITCH), :] = h3[:, c * LANES:(c + 1) * LANES]
    pay_ref[pl.ds(n_feat_rows, tm, stride=SLAB_PITCH), :] = meta
    meta_ref[...] = meta.T[:SUBLANES, :]


def _post(x2d, attn, ssd, woa, wos, xnw, wq, kv, wo, fnw, wr, br, tm, seq, mem_tokens):
    t, d = x2d.shape
    tiles_per_batch = seq // tm
    tril = jnp.asarray(np.tril(np.ones((tm, tm), np.float32), -1), dtype=BF16)
    full = lambda a: pl.BlockSpec(a.shape, lambda i: (0,) * a.ndim)
    row = lambda n: pl.BlockSpec((tm, n), lambda i: (i, 0))
    return pl.pallas_call(
        _post_kernel,
        grid=(t // tm,),
        in_specs=[row(d), row(attn.shape[1]), row(ssd.shape[1]), full(woa), full(wos), full(xnw), full(wq),
                  pl.BlockSpec((mem_tokens, kv.shape[1]), lambda i: (i // tiles_per_batch, 0)),
                  full(wo), full(fnw), full(wr), full(br), full(tril)],
        out_specs=[row(d), pl.BlockSpec((tm * SLAB_PITCH, LANES), lambda i: (i, 0)),
                   pl.BlockSpec((SUBLANES, tm), lambda i: (0, i)),
                   pl.BlockSpec((1, N_BUCKETS), lambda i: (0, 0))],
        out_shape=[jax.ShapeDtypeStruct((t, d), F32), jax.ShapeDtypeStruct((t * SLAB_PITCH, LANES), F32),
                   jax.ShapeDtypeStruct((SUBLANES, t), F32), jax.ShapeDtypeStruct((1, N_BUCKETS), F32)],
        scratch_shapes=[pltpu.VMEM((1, N_BUCKETS), F32)],
        compiler_params=_cparams(("arbitrary",)),
        name="post",
    )(x2d, attn, ssd, woa, wos, xnw, wq, kv, wo, fnw, wr, br, tril)


DMA_UNROLL = 8


def _dispatch_kernel(dest_ref, pay_ref, xs_hbm, sem):
    rows = dest_ref.shape[-1]

    def copy(r):
        return pltpu.make_async_copy(pay_ref.at[pl.ds(r * SLAB_PITCH, SLAB_PITCH)],
                                     xs_hbm.at[pl.ds(dest_ref[0, r] * SLAB_PITCH, SLAB_PITCH)], sem)

    def issue(g, carry):
        for u in range(DMA_UNROLL):
            copy(g * DMA_UNROLL + u).start()
        return carry

    lax.fori_loop(0, rows // DMA_UNROLL, issue, 0)

    def drain(g, carry):
        for u in range(DMA_UNROLL):
            copy(g * DMA_UNROLL + u).wait()
        return carry

    lax.fori_loop(0, rows // DMA_UNROLL, drain, 0)


def _dispatch(dest3d, payload, n_sorted):
    n_steps, _, rows = dest3d.shape
    return pl.pallas_call(
        _dispatch_kernel,
        grid=(n_steps,),
        in_specs=[pl.BlockSpec((None, 1, rows), lambda i: (i, 0, 0), memory_space=pltpu.SMEM),
                  pl.BlockSpec((rows * SLAB_PITCH, LANES), lambda i: (i, 0))],
        out_specs=pl.BlockSpec(memory_space=pl.ANY),
        out_shape=jax.ShapeDtypeStruct((n_sorted * SLAB_PITCH, LANES), payload.dtype),
        scratch_shapes=[pltpu.SemaphoreType.DMA(())],
        compiler_params=_cparams(("arbitrary",)),
        name="dispatch",
    )(dest3d, payload)


def _experts_kernel(grp_ref, ea_ref, eb_ref, blk_ref, valid_ref, xs_ref, wgu_ref, wd_ref, y_ref):
    i = pl.program_id(0)
    ff = wd_ref.shape[1]
    d = wd_ref.shape[2]
    n_feat_rows = d // LANES
    token_row = lambda c: pl.ds(c, MOE_TILE, stride=SLAB_PITCH)

    @pl.when(valid_ref[i] == 1)
    def _():
        x = jnp.concatenate([xs_ref[token_row(c), :].astype(BF16) for c in range(n_feat_rows)], axis=1)
        gates = xs_ref[token_row(n_feat_rows), :]
        y = None
        for e_ref, lane in ((ea_ref, 0), (eb_ref, 1)):
            e = e_ref[i]
            gu = _dot(x, wgu_ref[e])
            hid = (jax.nn.silu(gu[:, :ff]) * gu[:, ff:]).astype(BF16)
            part = gates[:, lane:lane + 1] * _dot(hid, wd_ref[e])
            y = part if y is None else y + part
        for c in range(n_feat_rows):
            y_ref[token_row(c), :] = y[:, c * LANES:(c + 1) * LANES]
        y_ref[token_row(n_feat_rows), :] = jnp.zeros((MOE_TILE, LANES), F32)


def _experts(tile_group, tile_a, tile_b, tile_blk, tile_valid, xs, wgu, wd):
    n_sorted = xs.shape[0] // SLAB_PITCH
    d = wd.shape[2]
    nt = n_sorted // MOE_TILE
    slab = lambda: pl.BlockSpec((MOE_TILE * SLAB_PITCH, LANES), lambda i, g, a, b, k, v: (k[i], 0))
    grid_spec = pltpu.PrefetchScalarGridSpec(
        num_scalar_prefetch=5,
        grid=(nt,),
        in_specs=[slab(),
                  pl.BlockSpec((EXPERTS_PER_GROUP,) + wgu.shape[1:], lambda i, g, a, b, k, v: (g[i], 0, 0)),
                  pl.BlockSpec((EXPERTS_PER_GROUP,) + wd.shape[1:], lambda i, g, a, b, k, v: (g[i], 0, 0))],
        out_specs=slab(),
    )
    return pl.pallas_call(
        _experts_kernel,
        grid_spec=grid_spec,
        out_shape=jax.ShapeDtypeStruct(xs.shape, F32),
        compiler_params=_cparams(("arbitrary",)),
        name="experts",
    )(tile_group, tile_a, tile_b, tile_blk, tile_valid, xs, wgu, wd)


COMBINE_ROW_BLOCK = 64


def _combine_kernel(dcur_ref, dnext_ref, x2_ref, nw_ref, ys_hbm, o_ref, buf_ref, sem):
    i = pl.program_id(0)
    n = pl.num_programs(0)
    rows, d = x2_ref.shape
    n_feat_rows = d // LANES
    slot = i % 2

    def gather(dref, s, start):
        def body(g, carry):
            for u in range(DMA_UNROLL):
                r = g * DMA_UNROLL + u
                cp = pltpu.make_async_copy(ys_hbm.at[pl.ds(dref[0, r] * SLAB_PITCH, n_feat_rows)],
                                           buf_ref.at[s, pl.ds(r * SLAB_PITCH, n_feat_rows)], sem.at[s])
                if start:
                    cp.start()
                else:
                    cp.wait()
            return carry

        lax.fori_loop(0, rows // DMA_UNROLL, body, 0)

    @pl.when(i == 0)
    def _():
        gather(dcur_ref, slot, True)

    @pl.when(i + 1 < n)
    def _():
        gather(dnext_ref, 1 - slot, True)

    gather(dcur_ref, slot, False)

    def block(b, carry):
        r0 = pl.multiple_of(b * COMBINE_ROW_BLOCK, COMBINE_ROW_BLOCK)
        rs = pl.ds(r0, COMBINE_ROW_BLOCK)
        chunks = [x2_ref[rs, c * LANES:(c + 1) * LANES]
                  + buf_ref[slot, pl.ds(r0 * SLAB_PITCH + c, COMBINE_ROW_BLOCK, stride=SLAB_PITCH), :]
                  for c in range(n_feat_rows)]
        ssq = chunks[0] * chunks[0]
        for ch in chunks[1:]:
            ssq = ssq + ch * ch
        scale = lax.rsqrt(jnp.sum(ssq, axis=-1, keepdims=True) / d + RMS_EPS)
        for c, ch in enumerate(chunks):
            o_ref[rs, c * LANES:(c + 1) * LANES] = ch * scale * nw_ref[:, c * LANES:(c + 1) * LANES]
        return carry

    lax.fori_loop(0, rows // COMBINE_ROW_BLOCK, block, 0)


def _combine(dest3d, x2, norm_w, ys):
    n_steps, _, rows = dest3d.shape
    t, d = x2.shape
    return pl.pallas_call(
        _combine_kernel,
        grid=(n_steps,),
        in_specs=[pl.BlockSpec((None, 1, rows), lambda i: (i, 0, 0), memory_space=pltpu.SMEM),
                  pl.BlockSpec((None, 1, rows), lambda i: (jnp.minimum(i + 1, n_steps - 1), 0, 0),
                               memory_space=pltpu.SMEM),
                  pl.BlockSpec((rows, d), lambda i: (i, 0)),
                  pl.BlockSpec((1, d), lambda i: (0, 0)),
                  pl.BlockSpec(memory_space=pl.ANY)],
        out_specs=pl.BlockSpec((rows, d), lambda i: (i, 0)),
        out_shape=jax.ShapeDtypeStruct((t, d), F32),
        scratch_shapes=[pltpu.VMEM((2, rows * SLAB_PITCH, LANES), F32), pltpu.SemaphoreType.DMA((2,))],
        compiler_params=_cparams(("arbitrary",)),
        name="combine",
    )(dest3d, dest3d, x2, norm_w, ys)


def _pad_cols(w, n):
    return jnp.pad(w, ((0, 0), (0, n - w.shape[1])))


def _tile_plan(counts, n_tiles):
    per_bucket = (counts + (MOE_TILE - 1)) // MOE_TILE
    tile_end = jnp.cumsum(per_bucket)
    tile_start = tile_end - per_bucket
    total = tile_end[-1]
    ids = jnp.arange(n_tiles, dtype=jnp.int32)
    blk = jnp.minimum(ids, total - 1)
    bucket = jnp.sum((tile_end[None, :] <= blk[:, None]).astype(jnp.int32), axis=1)
    pair = EXPERTS_PER_GROUP * EXPERTS_PER_GROUP
    group = bucket // pair
    slot_a = (bucket % pair) // EXPERTS_PER_GROUP
    slot_b = bucket % EXPERTS_PER_GROUP
    valid = (ids < total).astype(jnp.int32)
    return tile_start * MOE_TILE, group, slot_a, slot_b, blk, valid


def kernel(x, mem, positions, mix_norm_w, w_in, attn_sinks, ssd_conv_w, ssd_conv_b, ssd_dt_bias, ssd_a_log, ssd_d, attn_out_norm_w, ssd_out_norm_w, w_out, xattn_norm_w, mem_norm_w, xattn_w_q, xattn_w_kv, xattn_w_o, ffn_norm_w, router_group_w, router_group_b, router_expert_w, router_expert_b, expert_w_gate, expert_w_up, expert_w_down, final_norm_w):
    batch, seq, d = x.shape
    mem_tokens = mem.shape[1]
    depth = w_in.shape[0]
    t = batch * seq
    attn_w = attn_out_norm_w.shape[1]
    ssd_w = ssd_out_norm_w.shape[1]
    kv_w = N_KV_HEADS * HEAD_DIM
    conv_dim = ssd_conv_w.shape[2]
    n_ssd_heads = ssd_dt_bias.shape[1]
    n_exp = router_expert_w.shape[2]
    assert n_exp == N_GROUPS * EXPERTS_PER_GROUP and router_group_w.shape[2] == N_GROUPS
    assert seq % ATTN_BLOCK == 0 and seq % SSD_CHUNK == 0 and t % MOE_TILE == 0
    tm = min(512, seq)
    dispatch_rows = min(2048, t)
    combine_rows = min(1024, t)
    n_tiles = t // MOE_TILE + N_USED_BUCKETS
    n_sorted = n_tiles * MOE_TILE

    pos_row = positions.reshape(1, t)
    x2d = x.reshape(t, d)
    mem2d = mem.reshape(batch * mem_tokens, d)
    row = lambda v: v.reshape(1, -1)

    for l in range(depth):
        w = w_in[l].astype(BF16)
        o = 0
        parts = []
        for n in (attn_w, kv_w, kv_w, ssd_w, conv_dim, n_ssd_heads):
            parts.append(w[:, o:o + n])
            o += n
        wq, wk, wv, wz, wx, wdt = parts
        q, k, v, z, xbc, dt_raw = _in_proj(pos_row, x2d, row(mix_norm_w[l]), wq, wk, wv, wz, wx,
                                           _pad_cols(wdt, LANES), tm)
        attn = _swa(attn_sinks[l], q, k, v, row(attn_out_norm_w[l]), batch, seq)
        ssd = _ssd(xbc, dt_raw, z, ssd_conv_w[l], row(ssd_conv_b[l]),
                   _pad_cols(row(ssd_dt_bias[l]), LANES), _pad_cols(row(ssd_a_log[l]), LANES),
                   row(jnp.repeat(ssd_d[l], SSD_HEAD_DIM)), row(ssd_out_norm_w[l]), batch, seq)
        kv = _mem_kv(mem2d, row(mem_norm_w[l]), xattn_w_kv[l].astype(BF16), mem_tokens)

        wo = w_out[l].astype(BF16)
        wr32 = _pad_cols(jnp.concatenate([router_expert_w[l], router_group_w[l]], axis=1), LANES)
        wr_hi = wr32.astype(BF16)
        wr = jnp.concatenate([wr_hi, (wr32 - wr_hi.astype(F32)).astype(BF16)], axis=1)
        br = _pad_cols(row(jnp.concatenate([router_expert_b[l], router_group_b[l]])), LANES)
        x2, payload, meta, counts = _post(
            x2d, attn, ssd, wo[:attn_w], wo[attn_w:], row(xattn_norm_w[l]), xattn_w_q[l].astype(BF16), kv,
            xattn_w_o[l].astype(BF16), row(ffn_norm_w[l]), wr, br, tm, seq, mem_tokens)

        row_start, tile_group, tile_a, tile_b, tile_blk, tile_valid = _tile_plan(
            counts[0].astype(jnp.int32), n_tiles)
        bucket = meta[2].astype(jnp.int32)
        hit = bucket[:, None] == jnp.arange(N_BUCKETS, dtype=jnp.int32)[None, :]
        dest = jnp.sum(jnp.where(hit, row_start[None, :], 0), axis=1) + meta[3].astype(jnp.int32)
        xs = _dispatch(dest.reshape(t // dispatch_rows, 1, dispatch_rows), payload, n_sorted)
        wgu = jnp.concatenate([expert_w_gate[l], expert_w_up[l]], axis=2).astype(BF16)
        ys = _experts(tile_group, tile_a, tile_b, tile_blk, tile_valid, xs, wgu, expert_w_down[l].astype(BF16))
        assert depth == 1
        x2d = _combine(dest.reshape(t // combine_rows, 1, combine_rows), x2, row(final_norm_w), ys)

    return x2d.reshape(batch, seq, d)
```

```python
import functools

import numpy as np
import jax
import jax.numpy as jnp
from jax import lax
from jax.experimental import pallas as pl
from jax.experimental.pallas import tpu as pltpu

RMS_EPS = 1e-5
HEAD_DIM = 64
N_KV_HEADS = 2
WINDOW = 128
ATTN_BLOCK = 128
ROT_DIM = 16
ROT_HALF = ROT_DIM // 2
ROPE_THETA = 500000.0
SSD_HEAD_DIM = 64
SSD_GROUPS = 2
SSD_STATE = 128
SSD_CONV = 4
SSD_CHUNK = 128
XATTN_HEADS = 4
N_GROUPS = 4
EXPERTS_PER_GROUP = 8

LANES = 128
SUBLANES = 8
MOE_TILE = 256
SLAB_PITCH = SUBLANES + 1
N_BUCKETS = N_GROUPS * EXPERTS_PER_GROUP * EXPERTS_PER_GROUP
N_USED_BUCKETS = N_GROUPS * (EXPERTS_PER_GROUP * (EXPERTS_PER_GROUP - 1) // 2)
VMEM_LIMIT = 56 * 1024 * 1024

HIGHEST = lax.Precision.HIGHEST
F32 = jnp.float32
BF16 = jnp.bfloat16


def _cparams(semantics):
    return pltpu.CompilerParams(dimension_semantics=semantics, vmem_limit_bytes=VMEM_LIMIT)


def _rms(x, w):
    return x * lax.rsqrt(jnp.mean(x * x, axis=-1, keepdims=True) + RMS_EPS) * w


def _dot(a, b):
    return jnp.dot(a, b, preferred_element_type=F32)


def _dot_nt(a, b):
    return lax.dot_general(a, b, (((1,), (1,)), ((), ())), preferred_element_type=F32)


def _in_proj_kernel(pos_ref, x_ref, nw_ref, w_ref, place_ref, base_ref,
                    q_ref, k_ref, v_ref, z_ref, xbc_ref, dt_ref):
    x = x_ref[...]
    hb = _rms(x, nw_ref[...]).astype(BF16)

    j = lax.broadcasted_iota(jnp.int32, (ROT_HALF, 1), 0).astype(F32)
    inv_freq = jnp.power(jnp.float32(ROPE_THETA), -(2.0 * j) / ROT_DIM)
    ang = pos_ref[...].astype(F32) * inv_freq
    cs = jnp.concatenate([jnp.cos(ang), jnp.sin(ang)], axis=0)

    cs_hi = cs.astype(BF16)
    cs_lo = (cs - cs_hi.astype(F32)).astype(BF16)
    tn = (((0,), (0,)), ((), ()))
    tabs = (lax.dot_general(cs_hi, place_ref[...], tn, preferred_element_type=F32)
            + lax.dot_general(cs_lo, place_ref[...], tn, preferred_element_type=F32))
    c_tab = tabs[:, :LANES] + base_ref[...]
    s_up = tabs[:, LANES:2 * LANES]
    s_dn = tabs[:, 2 * LANES:]

    def rope(t):
        n = t.shape[1]
        reps = n // LANES
        c = jnp.tile(c_tab, (1, reps)) if reps > 1 else c_tab
        su = jnp.tile(s_up, (1, reps)) if reps > 1 else s_up
        sd = jnp.tile(s_dn, (1, reps)) if reps > 1 else s_dn
        return t * c + pltpu.roll(t, n - ROT_HALF, 1) * su + pltpu.roll(t, ROT_HALF, 1) * sd

    proj = _dot(hb, w_ref[...])
    o = 0
    pieces = []
    for ref in (q_ref, k_ref, v_ref, z_ref, xbc_ref, dt_ref):
        pieces.append(proj[:, o:o + ref.shape[1]])
        o += ref.shape[1]
    q, k, v, z, xbc, dt = pieces
    q_ref[...] = (rope(q) * (HEAD_DIM ** -0.5)).astype(BF16)
    k_ref[...] = rope(k).astype(BF16)
    v_ref[...] = v.astype(BF16)
    z_ref[...] = z.astype(BF16)
    xbc_ref[...] = xbc
    dt_ref[...] = dt


def _rope_placement():
    d = np.arange(LANES) % HEAD_DIM
    jj = np.arange(ROT_HALF)[:, None]
    ec = ((d[None, :] < ROT_DIM) & ((d[None, :] % ROT_HALF) == jj)).astype(np.float32)
    eup = -((d[None, :] < ROT_HALF) & (d[None, :] == jj)).astype(np.float32)
    edn = ((d[None, :] >= ROT_HALF) & (d[None, :] < ROT_DIM) & ((d[None, :] - ROT_HALF) == jj)).astype(np.float32)
    zero = np.zeros_like(ec)
    place = np.concatenate([np.concatenate([ec, zero, zero], axis=1),
                            np.concatenate([zero, eup, edn], axis=1)], axis=0)
    base = (d >= ROT_DIM).astype(np.float32)[None, :]
    return jnp.asarray(place, dtype=BF16), jnp.asarray(base)


def _in_proj(pos_row, x2d, norm_w, w_all, widths, tm):
    t, d = x2d.shape
    place, base = _rope_placement()
    full = lambda a: pl.BlockSpec(a.shape, lambda i: (0,) * a.ndim)
    row = lambda n: pl.BlockSpec((tm, n), lambda i: (i, 0))
    dtypes = (BF16, BF16, BF16, BF16, F32, F32)
    return pl.pallas_call(
        _in_proj_kernel,
        grid=(t // tm,),
        in_specs=[pl.BlockSpec((1, tm), lambda i: (0, i)), row(d), full(norm_w), full(w_all),
                  full(place), full(base)],
        out_specs=[row(n) for n in widths],
        out_shape=[jax.ShapeDtypeStruct((t, n), dt) for n, dt in zip(widths, dtypes)],
        compiler_params=_cparams(("parallel",)),
        name="in_proj",
    )(pos_row, x2d, norm_w, w_all, place, base)


SWA_ROWS = 32


def _swa_kernel(sink_ref, q_ref, kp_ref, kc_ref, vp_ref, vc_ref, nw_ref, o_ref,
                s_ref, p_ref, bias_ref, acc_ref):
    i = pl.program_id(1)
    blk = ATTN_BLOCK
    n_q_heads = q_ref.shape[1] // HEAD_DIM
    q_per_kv = n_q_heads // N_KV_HEADS
    qi = lax.broadcasted_iota(jnp.int32, (blk, 2 * blk), 0) + blk
    ki = lax.broadcasted_iota(jnp.int32, (blk, 2 * blk), 1)
    rel = qi - ki
    mask = (rel >= 0) & (rel < WINDOW) & ((i > 0) | (ki >= blk))
    bias_ref[...] = jnp.where(mask, 0.0, -jnp.inf)

    for kv in range(N_KV_HEADS):
        sl = slice(kv * HEAD_DIM, (kv + 1) * HEAD_DIM)
        kcat = jnp.concatenate([kp_ref[:, sl], kc_ref[:, sl]], axis=0)
        for g in range(q_per_kv):
            h = kv * q_per_kv + g
            s_ref[h * blk:(h + 1) * blk, :] = _dot_nt(q_ref[:, h * HEAD_DIM:(h + 1) * HEAD_DIM], kcat)

    for h in range(n_q_heads):
        sink = sink_ref[h]
        for c in range(blk // SWA_ROWS):
            rows = slice(h * blk + c * SWA_ROWS, h * blk + (c + 1) * SWA_ROWS)
            s = s_ref[rows, :] + bias_ref[c * SWA_ROWS:(c + 1) * SWA_ROWS, :]
            m = jnp.maximum(jnp.max(s, axis=-1, keepdims=True), sink)
            p = jnp.exp(s - m)
            denom = jnp.sum(p, axis=-1, keepdims=True) + jnp.exp(sink - m)
            p_ref[rows, :] = (p * (1.0 / denom)).astype(BF16)

    for kv in range(N_KV_HEADS):
        sl = slice(kv * HEAD_DIM, (kv + 1) * HEAD_DIM)
        vcat = jnp.concatenate([vp_ref[:, sl], vc_ref[:, sl]], axis=0)
        for g in range(q_per_kv):
            h = kv * q_per_kv + g
            acc_ref[:, h * HEAD_DIM:(h + 1) * HEAD_DIM] = _dot(p_ref[h * blk:(h + 1) * blk, :], vcat)
    o_ref[...] = _rms(acc_ref[...], nw_ref[...]).astype(BF16)


def _swa(sinks, q, k, v, norm_w, batch, seq):
    t, qw = q.shape
    kw = k.shape[1]
    nb = seq // ATTN_BLOCK
    cur = lambda b, i: (b * nb + i, 0)
    prev = lambda b, i: (b * nb + jnp.maximum(i - 1, 0), 0)
    return pl.pallas_call(
        _swa_kernel,
        grid=(batch, nb),
        in_specs=[pl.BlockSpec(memory_space=pltpu.SMEM),
                  pl.BlockSpec((ATTN_BLOCK, qw), cur),
                  pl.BlockSpec((ATTN_BLOCK, kw), prev), pl.BlockSpec((ATTN_BLOCK, kw), cur),
                  pl.BlockSpec((ATTN_BLOCK, kw), prev), pl.BlockSpec((ATTN_BLOCK, kw), cur),
                  pl.BlockSpec((1, qw), lambda b, i: (0, 0))],
        out_specs=pl.BlockSpec((ATTN_BLOCK, qw), cur),
        out_shape=jax.ShapeDtypeStruct((t, qw), BF16),
        scratch_shapes=[pltpu.VMEM((qw // HEAD_DIM * ATTN_BLOCK, 2 * ATTN_BLOCK), F32),
                        pltpu.VMEM((qw // HEAD_DIM * ATTN_BLOCK, 2 * ATTN_BLOCK), BF16),
                        pltpu.VMEM((ATTN_BLOCK, 2 * ATTN_BLOCK), F32),
                        pltpu.VMEM((ATTN_BLOCK, qw), F32)],
        compiler_params=_cparams(("parallel", "parallel")),
        name="swa",
    )(sinks, q, k, k, v, v, norm_w)


def _ssd_kernel(xbc_ref, dt_ref, z_ref, cw_ref, cb_ref, dtb_ref, alog_ref, dskip_ref, nw_ref,
                tril_ref, o_ref, conv_ref, state_ref, y_ref):
    c = pl.program_id(1)
    L = SSD_CHUNK
    width = z_ref.shape[1]
    n_heads = width // SSD_HEAD_DIM
    heads_per_group = n_heads // SSD_GROUPS
    pad = SUBLANES

    @pl.when(c == 0)
    def _():
        state_ref[...] = jnp.zeros_like(state_ref)
        conv_ref[0:pad, :] = jnp.zeros((pad, conv_ref.shape[1]), F32)

    conv_ref[pad:pad + L, :] = xbc_ref[...]
    acc = cb_ref[...] + cw_ref[0:1, :] * conv_ref[pad - (SSD_CONV - 1):pad - (SSD_CONV - 1) + L, :]
    for jj in range(1, SSD_CONV):
        off = pad - (SSD_CONV - 1) + jj
        acc = acc + cw_ref[jj:jj + 1, :] * conv_ref[off:off + L, :]
    conv_ref[0:pad, :] = conv_ref[L:L + pad, :]
    u = jax.nn.silu(acc)
    xs = u[:, :width]
    bm = u[:, width:width + SSD_GROUPS * SSD_STATE]
    cm = u[:, width + SSD_GROUPS * SSD_STATE:]

    dt = jax.nn.softplus(dt_ref[...] + dtb_ref[...])
    da = dt * (-jnp.exp(alog_ref[...]))
    cum = jnp.dot(tril_ref[...], da, precision=HIGHEST, preferred_element_type=F32)
    cum_t = cum.T
    row = lax.broadcasted_iota(jnp.int32, (L, L), 0)
    col = lax.broadcasted_iota(jnp.int32, (L, L), 1)
    causal = row >= col

    for g in range(SSD_GROUPS):
        bg = bm[:, g * SSD_STATE:(g + 1) * SSD_STATE]
        cg = cm[:, g * SSD_STATE:(g + 1) * SSD_STATE].astype(BF16)
        cb = _dot_nt(cg, bg.astype(BF16))
        bgt = bg.T.astype(BF16)
        for r in range(heads_per_group):
            h = g * heads_per_group + r
            hs = slice(h * SSD_HEAD_DIM, (h + 1) * SSD_HEAD_DIM)
            cum_h = cum[:, h:h + 1]
            seg = cum_h - cum_t[h:h + 1, :]
            decay = jnp.exp(jnp.where(causal, seg, -jnp.inf))
            xs_h = xs[:, hs]
            xc = xs_h * dt[:, h:h + 1]
            st = state_ref[:, hs]
            y = _dot((cb * decay).astype(BF16), xc.astype(BF16))
            y = y + _dot(cg, st.astype(BF16)) * jnp.exp(cum_h)
            y_ref[:, hs] = y + dskip_ref[:, hs] * xs_h
            cum_last = cum[L - 1:L, h:h + 1]
            to_end = jnp.exp(cum_last - cum_h)
            state_ref[:, hs] = st * jnp.exp(cum_last) + _dot(bgt, (xc * to_end).astype(BF16))

    gated = y_ref[...] * jax.nn.silu(z_ref[...].astype(F32))
    gw = width // SSD_GROUPS
    parts = []
    for g in range(SSD_GROUPS):
        gg = gated[:, g * gw:(g + 1) * gw]
        parts.append(gg * lax.rsqrt(jnp.mean(gg * gg, axis=-1, keepdims=True) + RMS_EPS))
    o_ref[...] = (jnp.concatenate(parts, axis=1) * nw_ref[...]).astype(BF16)


def _ssd(xbc, dt_raw, z, conv_w, conv_b, dt_bias, a_log, d_skip, norm_w, batch, seq):
    t, cw = xbc.shape
    width = z.shape[1]
    nc = seq // SSD_CHUNK
    tril = jnp.asarray(np.tril(np.ones((SSD_CHUNK, SSD_CHUNK), np.float32)))
    cur = lambda b, c: (b * nc + c, 0)
    full = lambda a: pl.BlockSpec(a.shape, lambda b, c: (0,) * a.ndim)
    return pl.pallas_call(
        _ssd_kernel,
        grid=(batch, nc),
        in_specs=[pl.BlockSpec((SSD_CHUNK, cw), cur), pl.BlockSpec((SSD_CHUNK, LANES), cur),
                  pl.BlockSpec((SSD_CHUNK, width), cur),
                  full(conv_w), full(conv_b), full(dt_bias), full(a_log), full(d_skip), full(norm_w),
                  full(tril)],
        out_specs=pl.BlockSpec((SSD_CHUNK, width), cur),
        out_shape=jax.ShapeDtypeStruct((t, width), BF16),
        scratch_shapes=[pltpu.VMEM((SSD_CHUNK + 2 * SUBLANES, cw), F32),
                        pltpu.VMEM((SSD_STATE, width), F32),
                        pltpu.VMEM((SSD_CHUNK, width), F32)],
        compiler_params=_cparams(("parallel", "arbitrary")),
        name="ssd",
    )(xbc, dt_raw, z, conv_w, conv_b, dt_bias, a_log, d_skip, norm_w, tril)


def _mem_kv_kernel(m_ref, nw_ref, w_ref, o_ref):
    o_ref[...] = _dot(_rms(m_ref[...], nw_ref[...]).astype(BF16), w_ref[...]).astype(BF16)


def _mem_kv(mem2d, norm_w, w_kv, rows):
    t, d = mem2d.shape
    n = w_kv.shape[1]
    return pl.pallas_call(
        _mem_kv_kernel,
        grid=(t // rows,),
        in_specs=[pl.BlockSpec((rows, d), lambda i: (i, 0)),
                  pl.BlockSpec((1, d), lambda i: (0, 0)),
                  pl.BlockSpec((d, n), lambda i: (0, 0))],
        out_specs=pl.BlockSpec((rows, n), lambda i: (i, 0)),
        out_shape=jax.ShapeDtypeStruct((t, n), BF16),
        compiler_params=_cparams(("parallel",)),
        name="mem_kv",
    )(mem2d, norm_w, w_kv)


def _post_kernel(x_ref, a_ref, s_ref, woa_ref, wos_ref, xnw_ref, wq_ref, kv_ref, wo_ref,
                 fnw_ref, wr_ref, br_ref, tril_ref,
                 x2_ref, pay_ref, meta_ref, cnt_ref, carry_ref):
    step = pl.program_id(0)
    tm = x_ref.shape[0]
    n_exp = N_GROUPS * EXPERTS_PER_GROUP

    @pl.when(step == 0)
    def _():
        carry_ref[...] = jnp.zeros_like(carry_ref)

    x1 = x_ref[...] + _dot(a_ref[...], woa_ref[...]) + _dot(s_ref[...], wos_ref[...])

    xw = wq_ref.shape[1]
    hd = xw // XATTN_HEADS
    q = (_dot(_rms(x1, xnw_ref[...]).astype(BF16), wq_ref[...]) * (hd ** -0.5)).astype(BF16)
    heads = []
    for h in range(XATTN_HEADS):
        s = _dot_nt(q[:, h * hd:(h + 1) * hd], kv_ref[:, h * hd:(h + 1) * hd])
        p = jnp.exp(s - jnp.max(s, axis=-1, keepdims=True))
        o = _dot(p.astype(BF16), kv_ref[:, xw + h * hd:xw + (h + 1) * hd])
        heads.append((o / jnp.sum(p, axis=-1, keepdims=True)).astype(BF16))
    x2 = x1 + _dot(jnp.concatenate(heads, axis=1), wo_ref[...])
    x2_ref[...] = x2

    h3 = _rms(x2, fnw_ref[...])
    h_hi = h3.astype(BF16)
    h_lo = (h3 - h_hi.astype(F32)).astype(BF16)
    t_hi = _dot(h_hi, wr_ref[...])
    logits = t_hi[:, :LANES] + (t_hi[:, LANES:] + _dot(h_lo, wr_ref[:, :LANES])) + br_ref[...]
    lane = lax.broadcasted_iota(jnp.int32, (tm, LANES), 1)
    big = jnp.int32(LANES)
    neg = -jnp.inf
    g_l = jnp.where((lane >= n_exp) & (lane < n_exp + N_GROUPS), logits, neg)
    g_max = jnp.max(g_l, axis=-1, keepdims=True)
    g_idx = jnp.min(jnp.where(g_l == g_max, lane - n_exp, big), axis=-1, keepdims=True)
    g_gate = 1.0 / jnp.sum(jnp.exp(g_l - g_max), axis=-1, keepdims=True)
    e_l = jnp.where((lane < n_exp) & ((lane // EXPERTS_PER_GROUP) == g_idx), logits, neg)
    m1 = jnp.max(e_l, axis=-1, keepdims=True)
    i1 = jnp.min(jnp.where(e_l == m1, lane, big), axis=-1, keepdims=True)
    e_l2 = jnp.where(lane == i1, neg, e_l)
    m2 = jnp.max(e_l2, axis=-1, keepdims=True)
    i2 = jnp.min(jnp.where(e_l2 == m2, lane, big), axis=-1, keepdims=True)
    e2 = jnp.exp(m2 - m1)
    w1 = (1.0 / (1.0 + e2)) * g_gate
    w2 = (e2 / (1.0 + e2)) * g_gate
    first_low = i1 < i2
    lo = jnp.where(first_low, i1, i2) % EXPERTS_PER_GROUP
    hi = jnp.where(first_low, i2, i1) % EXPERTS_PER_GROUP
    w_lo = jnp.where(first_low, w1, w2)
    w_hi = jnp.where(first_low, w2, w1)
    bucket = g_idx * (EXPERTS_PER_GROUP * EXPERTS_PER_GROUP) + lo * EXPERTS_PER_GROUP + hi

    blane = lax.broadcasted_iota(jnp.int32, (tm, N_BUCKETS), 1)
    onehot = (blane == bucket).astype(F32)
    before = _dot(tril_ref[...], onehot.astype(BF16)) + carry_ref[...]
    rank = jnp.sum(onehot * before, axis=-1, keepdims=True)
    carry_ref[...] = carry_ref[...] + jnp.sum(onehot, axis=0, keepdims=True)
    cnt_ref[...] = carry_ref[...]

    meta = jnp.where(lane == 0, w_lo, jnp.where(lane == 1, w_hi, jnp.where(
        lane == 2, bucket.astype(F32), jnp.where(lane == 3, rank, 0.0))))
    n_feat_rows = h3.shape[1] // LANES
    for c in range(n_feat_rows):
        pay_ref[pl.ds(c, tm, stride=SLAB_PITCH), :] = h3[:, c * LANES:(c + 1) * LANES]
    pay_ref[pl.ds(n_feat_rows, tm, stride=SLAB_PITCH), :] = meta
    meta_ref[...] = meta.T[:SUBLANES, :]


def _post(x2d, attn, ssd, woa, wos, xnw, wq, kv, wo, fnw, wr, br, tm, seq, mem_tokens):
    t, d = x2d.shape
    tiles_per_batch = seq // tm
    tril = jnp.asarray(np.tril(np.ones((tm, tm), np.float32), -1), dtype=BF16)
    full = lambda a: pl.BlockSpec(a.shape, lambda i: (0,) * a.ndim)
    row = lambda n: pl.BlockSpec((tm, n), lambda i: (i, 0))
    return pl.pallas_call(
        _post_kernel,
        grid=(t // tm,),
        in_specs=[row(d), row(attn.shape[1]), row(ssd.shape[1]), full(woa), full(wos), full(xnw), full(wq),
                  pl.BlockSpec((mem_tokens, kv.shape[1]), lambda i: (i // tiles_per_batch, 0)),
                  full(wo), full(fnw), full(wr), full(br), full(tril)],
        out_specs=[row(d), pl.BlockSpec((tm * SLAB_PITCH, LANES), lambda i: (i, 0)),
                   pl.BlockSpec((SUBLANES, tm), lambda i: (0, i)),
                   pl.BlockSpec((1, N_BUCKETS), lambda i: (0, 0))],
        out_shape=[jax.ShapeDtypeStruct((t, d), F32), jax.ShapeDtypeStruct((t * SLAB_PITCH, LANES), F32),
                   jax.ShapeDtypeStruct((SUBLANES, t), F32), jax.ShapeDtypeStruct((1, N_BUCKETS), F32)],
        scratch_shapes=[pltpu.VMEM((1, N_BUCKETS), F32)],
        compiler_params=_cparams(("arbitrary",)),
        name="post",
    )(x2d, attn, ssd, woa, wos, xnw, wq, kv, wo, fnw, wr, br, tril)


DMA_UNROLL = 8


def _dispatch_kernel(dest_ref, pay_ref, xs_hbm, sem):
    rows = dest_ref.shape[-1]

    def copy(r):
        return pltpu.make_async_copy(pay_ref.at[pl.ds(r * SLAB_PITCH, SLAB_PITCH)],
                                     xs_hbm.at[pl.ds(dest_ref[0, r] * SLAB_PITCH, SLAB_PITCH)], sem)

    def issue(g, carry):
        for u in range(DMA_UNROLL):
            copy(g * DMA_UNROLL + u).start()
        return carry

    lax.fori_loop(0, rows // DMA_UNROLL, issue, 0)

    def drain(g, carry):
        for u in range(DMA_UNROLL):
            copy(g * DMA_UNROLL + u).wait()
        return carry

    lax.fori_loop(0, rows // DMA_UNROLL, drain, 0)


def _dispatch(dest3d, payload, n_sorted):
    n_steps, _, rows = dest3d.shape
    return pl.pallas_call(
        _dispatch_kernel,
        grid=(n_steps,),
        in_specs=[pl.BlockSpec((None, 1, rows), lambda i: (i, 0, 0), memory_space=pltpu.SMEM),
                  pl.BlockSpec((rows * SLAB_PITCH, LANES), lambda i: (i, 0))],
        out_specs=pl.BlockSpec(memory_space=pl.ANY),
        out_shape=jax.ShapeDtypeStruct((n_sorted * SLAB_PITCH, LANES), payload.dtype),
        scratch_shapes=[pltpu.SemaphoreType.DMA(())],
        compiler_params=_cparams(("arbitrary",)),
        name="dispatch",
    )(dest3d, payload)


def _experts_kernel(grp_ref, ea_ref, eb_ref, blk_ref, valid_ref, xs_ref, wgu_ref, wd_ref, y_ref):
    i = pl.program_id(0)
    ff = wd_ref.shape[1]
    d = wd_ref.shape[2]
    n_feat_rows = d // LANES
    token_row = lambda c: pl.ds(c, MOE_TILE, stride=SLAB_PITCH)

    @pl.when(valid_ref[i] == 1)
    def _():
        x = jnp.concatenate([xs_ref[token_row(c), :].astype(BF16) for c in range(n_feat_rows)], axis=1)
        gates = xs_ref[token_row(n_feat_rows), :]
        y = None
        for e_ref, lane in ((ea_ref, 0), (eb_ref, 1)):
            e = e_ref[i]
            gu = _dot(x, wgu_ref[e])
            hid = (jax.nn.silu(gu[:, :ff]) * gu[:, ff:]).astype(BF16)
            part = gates[:, lane:lane + 1] * _dot(hid, wd_ref[e])
            y = part if y is None else y + part
        for c in range(n_feat_rows):
            y_ref[token_row(c), :] = y[:, c * LANES:(c + 1) * LANES]
        y_ref[token_row(n_feat_rows), :] = jnp.zeros((MOE_TILE, LANES), F32)


def _experts(tile_group, tile_a, tile_b, tile_blk, tile_valid, xs, wgu, wd):
    n_sorted = xs.shape[0] // SLAB_PITCH
    d = wd.shape[2]
    nt = n_sorted // MOE_TILE
    slab = lambda: pl.BlockSpec((MOE_TILE * SLAB_PITCH, LANES), lambda i, g, a, b, k, v: (k[i], 0))
    grid_spec = pltpu.PrefetchScalarGridSpec(
        num_scalar_prefetch=5,
        grid=(nt,),
        in_specs=[slab(),
                  pl.BlockSpec((EXPERTS_PER_GROUP,) + wgu.shape[1:], lambda i, g, a, b, k, v: (g[i], 0, 0)),
                  pl.BlockSpec((EXPERTS_PER_GROUP,) + wd.shape[1:], lambda i, g, a, b, k, v: (g[i], 0, 0))],
        out_specs=slab(),
    )
    return pl.pallas_call(
        _experts_kernel,
        grid_spec=grid_spec,
        out_shape=jax.ShapeDtypeStruct(xs.shape, F32),
        compiler_params=_cparams(("arbitrary",)),
        name="experts",
    )(tile_group, tile_a, tile_b, tile_blk, tile_valid, xs, wgu, wd)


COMBINE_ROW_BLOCK = 64


def _combine_kernel(dcur_ref, dnext_ref, x2_ref, nw_ref, ys_hbm, o_ref, buf_ref, sem):
    i = pl.program_id(0)
    n = pl.num_programs(0)
    rows, d = x2_ref.shape
    n_feat_rows = d // LANES
    slot = i % 2

    def gather(dref, s, start):
        def body(g, carry):
            for u in range(DMA_UNROLL):
                r = g * DMA_UNROLL + u
                cp = pltpu.make_async_copy(ys_hbm.at[pl.ds(dref[0, r] * SLAB_PITCH, n_feat_rows)],
                                           buf_ref.at[s, pl.ds(r * SLAB_PITCH, n_feat_rows)], sem.at[s])
                if start:
                    cp.start()
                else:
                    cp.wait()
            return carry

        lax.fori_loop(0, rows // DMA_UNROLL, body, 0)

    @pl.when(i == 0)
    def _():
        gather(dcur_ref, slot, True)

    @pl.when(i + 1 < n)
    def _():
        gather(dnext_ref, 1 - slot, True)

    gather(dcur_ref, slot, False)

    def block(b, carry):
        r0 = pl.multiple_of(b * COMBINE_ROW_BLOCK, COMBINE_ROW_BLOCK)
        rs = pl.ds(r0, COMBINE_ROW_BLOCK)
        chunks = [x2_ref[rs, c * LANES:(c + 1) * LANES]
                  + buf_ref[slot, pl.ds(r0 * SLAB_PITCH + c, COMBINE_ROW_BLOCK, stride=SLAB_PITCH), :]
                  for c in range(n_feat_rows)]
        ssq = chunks[0] * chunks[0]
        for ch in chunks[1:]:
            ssq = ssq + ch * ch
        scale = lax.rsqrt(jnp.sum(ssq, axis=-1, keepdims=True) / d + RMS_EPS)
        for c, ch in enumerate(chunks):
            o_ref[rs, c * LANES:(c + 1) * LANES] = ch * scale * nw_ref[:, c * LANES:(c + 1) * LANES]
        return carry

    lax.fori_loop(0, rows // COMBINE_ROW_BLOCK, block, 0)


def _combine(dest3d, x2, norm_w, ys):
    n_steps, _, rows = dest3d.shape
    t, d = x2.shape
    return pl.pallas_call(
        _combine_kernel,
        grid=(n_steps,),
        in_specs=[pl.BlockSpec((None, 1, rows), lambda i: (i, 0, 0), memory_space=pltpu.SMEM),
                  pl.BlockSpec((None, 1, rows), lambda i: (jnp.minimum(i + 1, n_steps - 1), 0, 0),
                               memory_space=pltpu.SMEM),
                  pl.BlockSpec((rows, d), lambda i: (i, 0)),
                  pl.BlockSpec((1, d), lambda i: (0, 0)),
                  pl.BlockSpec(memory_space=pl.ANY)],
        out_specs=pl.BlockSpec((rows, d), lambda i: (i, 0)),
        out_shape=jax.ShapeDtypeStruct((t, d), F32),
        scratch_shapes=[pltpu.VMEM((2, rows * SLAB_PITCH, LANES), F32), pltpu.SemaphoreType.DMA((2,))],
        compiler_params=_cparams(("arbitrary",)),
        name="combine",
    )(dest3d, dest3d, x2, norm_w, ys)


def _pad_cols(w, n):
    return jnp.pad(w, ((0, 0), (0, n - w.shape[1])))


def _tile_plan(counts, n_tiles):
    per_bucket = (counts + (MOE_TILE - 1)) // MOE_TILE
    tile_end = jnp.cumsum(per_bucket)
    tile_start = tile_end - per_bucket
    total = tile_end[-1]
    ids = jnp.arange(n_tiles, dtype=jnp.int32)
    blk = jnp.minimum(ids, total - 1)
    bucket = jnp.sum((tile_end[None, :] <= blk[:, None]).astype(jnp.int32), axis=1)
    pair = EXPERTS_PER_GROUP * EXPERTS_PER_GROUP
    group = bucket // pair
    slot_a = (bucket % pair) // EXPERTS_PER_GROUP
    slot_b = bucket % EXPERTS_PER_GROUP
    valid = (ids < total).astype(jnp.int32)
    return tile_start * MOE_TILE, group, slot_a, slot_b, blk, valid


def kernel(x, mem, positions, mix_norm_w, w_in, attn_sinks, ssd_conv_w, ssd_conv_b, ssd_dt_bias, ssd_a_log, ssd_d, attn_out_norm_w, ssd_out_norm_w, w_out, xattn_norm_w, mem_norm_w, xattn_w_q, xattn_w_kv, xattn_w_o, ffn_norm_w, router_group_w, router_group_b, router_expert_w, router_expert_b, expert_w_gate, expert_w_up, expert_w_down, final_norm_w):
    batch, seq, d = x.shape
    mem_tokens = mem.shape[1]
    depth = w_in.shape[0]
    t = batch * seq
    attn_w = attn_out_norm_w.shape[1]
    ssd_w = ssd_out_norm_w.shape[1]
    kv_w = N_KV_HEADS * HEAD_DIM
    conv_dim = ssd_conv_w.shape[2]
    n_exp = router_expert_w.shape[2]
    assert n_exp == N_GROUPS * EXPERTS_PER_GROUP and router_group_w.shape[2] == N_GROUPS
    assert seq % ATTN_BLOCK == 0 and seq % SSD_CHUNK == 0 and t % MOE_TILE == 0
    tm = min(512, seq)
    dispatch_rows = min(2048, t)
    combine_rows = min(1024, t)
    n_tiles = t // MOE_TILE + N_USED_BUCKETS
    n_sorted = n_tiles * MOE_TILE

    pos_row = positions.reshape(1, t)
    x2d = x.reshape(t, d)
    mem2d = mem.reshape(batch * mem_tokens, d)
    row = lambda v: v.reshape(1, -1)

    for l in range(depth):
        widths = (attn_w, kv_w, kv_w, ssd_w, conv_dim, LANES)
        q, k, v, z, xbc, dt_raw = _in_proj(pos_row, x2d, row(mix_norm_w[l]),
                                           _pad_cols(w_in[l], sum(widths)).astype(BF16), widths, tm)
        attn = _swa(attn_sinks[l], q, k, v, row(attn_out_norm_w[l]), batch, seq)
        ssd = _ssd(xbc, dt_raw, z, ssd_conv_w[l], row(ssd_conv_b[l]),
                   _pad_cols(row(ssd_dt_bias[l]), LANES), _pad_cols(row(ssd_a_log[l]), LANES),
                   row(jnp.repeat(ssd_d[l], SSD_HEAD_DIM)), row(ssd_out_norm_w[l]), batch, seq)
        kv = _mem_kv(mem2d, row(mem_norm_w[l]), xattn_w_kv[l].astype(BF16), mem_tokens)

        wo = w_out[l].astype(BF16)
        wr32 = _pad_cols(jnp.concatenate([router_expert_w[l], router_group_w[l]], axis=1), LANES)
        wr_hi = wr32.astype(BF16)
        wr = jnp.concatenate([wr_hi, (wr32 - wr_hi.astype(F32)).astype(BF16)], axis=1)
        br = _pad_cols(row(jnp.concatenate([router_expert_b[l], router_group_b[l]])), LANES)
        x2, payload, meta, counts = _post(
            x2d, attn, ssd, wo[:attn_w], wo[attn_w:], row(xattn_norm_w[l]), xattn_w_q[l].astype(BF16), kv,
            xattn_w_o[l].astype(BF16), row(ffn_norm_w[l]), wr, br, tm, seq, mem_tokens)

        row_start, tile_group, tile_a, tile_b, tile_blk, tile_valid = _tile_plan(
            counts[0].astype(jnp.int32), n_tiles)
        bucket = meta[2].astype(jnp.int32)
        hit = bucket[:, None] == jnp.arange(N_BUCKETS, dtype=jnp.int32)[None, :]
        dest = jnp.sum(jnp.where(hit, row_start[None, :], 0), axis=1) + meta[3].astype(jnp.int32)
        xs = _dispatch(dest.reshape(t // dispatch_rows, 1, dispatch_rows), payload, n_sorted)
        wgu = jnp.concatenate([expert_w_gate[l], expert_w_up[l]], axis=2).astype(BF16)
        ys = _experts(tile_group, tile_a, tile_b, tile_blk, tile_valid, xs, wgu, expert_w_down[l].astype(BF16))
        assert depth == 1
        x2d = _combine(dest.reshape(t // combine_rows, 1, combine_rows), x2, row(final_norm_w), ys)

    return x2d.reshape(batch, seq, d)
```

```python
import functools

import numpy as np
import jax
import jax.numpy as jnp
from jax import lax
from jax.experimental import pallas as pl
from jax.experimental.pallas import tpu as pltpu

RMS_EPS = 1e-5
HEAD_DIM = 64
N_KV_HEADS = 2
WINDOW = 128
ATTN_BLOCK = 128
ROT_DIM = 16
ROT_HALF = ROT_DIM // 2
ROPE_THETA = 500000.0
SSD_HEAD_DIM = 64
SSD_GROUPS = 2
SSD_STATE = 128
SSD_CONV = 4
SSD_CHUNK = 128
XATTN_HEADS = 4
N_GROUPS = 4
EXPERTS_PER_GROUP = 8

LANES = 128
SUBLANES = 8
MOE_TILE = 256
SLAB_PITCH = SUBLANES + 1
N_BUCKETS = N_GROUPS * EXPERTS_PER_GROUP * EXPERTS_PER_GROUP
N_USED_BUCKETS = N_GROUPS * (EXPERTS_PER_GROUP * (EXPERTS_PER_GROUP - 1) // 2)
VMEM_LIMIT = 56 * 1024 * 1024

F32 = jnp.float32
BF16 = jnp.bfloat16


def _cparams(semantics):
    return pltpu.CompilerParams(dimension_semantics=semantics, vmem_limit_bytes=VMEM_LIMIT)


def _rms(x, w):
    return x * lax.rsqrt(jnp.mean(x * x, axis=-1, keepdims=True) + RMS_EPS) * w


def _dot(a, b):
    return jnp.dot(a, b, preferred_element_type=F32)


def _dot_nt(a, b):
    return lax.dot_general(a, b, (((1,), (1,)), ((), ())), preferred_element_type=F32)


def _in_proj_kernel(pos_ref, x_ref, nw_ref, w_ref, place_ref, base_ref,
                    q_ref, k_ref, v_ref, z_ref, xbc_ref, dt_ref):
    x = x_ref[...]
    hb = _rms(x, nw_ref[...]).astype(BF16)

    j = lax.broadcasted_iota(jnp.int32, (ROT_HALF, 1), 0).astype(F32)
    inv_freq = jnp.power(jnp.float32(ROPE_THETA), -(2.0 * j) / ROT_DIM)
    ang = pos_ref[...].astype(F32) * inv_freq
    cs = jnp.concatenate([jnp.cos(ang), jnp.sin(ang)], axis=0)

    cs_hi = cs.astype(BF16)
    cs_lo = (cs - cs_hi.astype(F32)).astype(BF16)
    tn = (((0,), (0,)), ((), ()))
    tabs = (lax.dot_general(cs_hi, place_ref[...], tn, preferred_element_type=F32)
            + lax.dot_general(cs_lo, place_ref[...], tn, preferred_element_type=F32))
    c_tab = tabs[:, :LANES] + base_ref[...]
    s_up = tabs[:, LANES:2 * LANES]
    s_dn = tabs[:, 2 * LANES:]

    def rope(t):
        n = t.shape[1]
        reps = n // LANES
        c = jnp.tile(c_tab, (1, reps)) if reps > 1 else c_tab
        su = jnp.tile(s_up, (1, reps)) if reps > 1 else s_up
        sd = jnp.tile(s_dn, (1, reps)) if reps > 1 else s_dn
        return t * c + pltpu.roll(t, n - ROT_HALF, 1) * su + pltpu.roll(t, ROT_HALF, 1) * sd

    proj = _dot(hb, w_ref[...])
    o = 0
    pieces = []
    for ref in (q_ref, k_ref, v_ref, z_ref, xbc_ref, dt_ref):
        pieces.append(proj[:, o:o + ref.shape[1]])
        o += ref.shape[1]
    q, k, v, z, xbc, dt = pieces
    q_ref[...] = (rope(q) * (HEAD_DIM ** -0.5)).astype(BF16)
    k_ref[...] = rope(k).astype(BF16)
    v_ref[...] = v.astype(BF16)
    z_ref[...] = z.astype(BF16)
    xbc_ref[...] = xbc
    dt_ref[...] = dt


def _rope_placement():
    d = np.arange(LANES) % HEAD_DIM
    jj = np.arange(ROT_HALF)[:, None]
    ec = ((d[None, :] < ROT_DIM) & ((d[None, :] % ROT_HALF) == jj)).astype(np.float32)
    eup = -((d[None, :] < ROT_HALF) & (d[None, :] == jj)).astype(np.float32)
    edn = ((d[None, :] >= ROT_HALF) & (d[None, :] < ROT_DIM) & ((d[None, :] - ROT_HALF) == jj)).astype(np.float32)
    zero = np.zeros_like(ec)
    place = np.concatenate([np.concatenate([ec, zero, zero], axis=1),
                            np.concatenate([zero, eup, edn], axis=1)], axis=0)
    base = (d >= ROT_DIM).astype(np.float32)[None, :]
    return jnp.asarray(place, dtype=BF16), jnp.asarray(base)


def _in_proj(pos_row, x2d, norm_w, w_all, widths, tm):
    t, d = x2d.shape
    place, base = _rope_placement()
    full = lambda a: pl.BlockSpec(a.shape, lambda i: (0,) * a.ndim)
    row = lambda n: pl.BlockSpec((tm, n), lambda i: (i, 0))
    dtypes = (BF16, BF16, BF16, BF16, F32, F32)
    return pl.pallas_call(
        _in_proj_kernel,
        grid=(t // tm,),
        in_specs=[pl.BlockSpec((1, tm), lambda i: (0, i)), row(d), full(norm_w), full(w_all),
                  full(place), full(base)],
        out_specs=[row(n) for n in widths],
        out_shape=[jax.ShapeDtypeStruct((t, n), dt) for n, dt in zip(widths, dtypes)],
        compiler_params=_cparams(("parallel",)),
        name="in_proj",
    )(pos_row, x2d, norm_w, w_all, place, base)


SWA_ROWS = 32


def _swa_kernel(sink_ref, q_ref, kp_ref, kc_ref, vp_ref, vc_ref, nw_ref, o_ref,
                s_ref, p_ref, bias_ref, acc_ref):
    i = pl.program_id(1)
    blk = ATTN_BLOCK
    n_q_heads = q_ref.shape[1] // HEAD_DIM
    q_per_kv = n_q_heads // N_KV_HEADS
    qi = lax.broadcasted_iota(jnp.int32, (blk, 2 * blk), 0) + blk
    ki = lax.broadcasted_iota(jnp.int32, (blk, 2 * blk), 1)
    rel = qi - ki
    mask = (rel >= 0) & (rel < WINDOW) & ((i > 0) | (ki >= blk))
    bias_ref[...] = jnp.where(mask, 0.0, -jnp.inf)

    for kv in range(N_KV_HEADS):
        sl = slice(kv * HEAD_DIM, (kv + 1) * HEAD_DIM)
        kcat = jnp.concatenate([kp_ref[:, sl], kc_ref[:, sl]], axis=0)
        for g in range(q_per_kv):
            h = kv * q_per_kv + g
            s_ref[h * blk:(h + 1) * blk, :] = _dot_nt(q_ref[:, h * HEAD_DIM:(h + 1) * HEAD_DIM], kcat)

    for h in range(n_q_heads):
        sink = sink_ref[h]
        for c in range(blk // SWA_ROWS):
            rows = slice(h * blk + c * SWA_ROWS, h * blk + (c + 1) * SWA_ROWS)
            s = s_ref[rows, :] + bias_ref[c * SWA_ROWS:(c + 1) * SWA_ROWS, :]
            m = jnp.maximum(jnp.max(s, axis=-1, keepdims=True), sink)
            p = jnp.exp(s - m)
            denom = jnp.sum(p, axis=-1, keepdims=True) + jnp.exp(sink - m)
            p_ref[rows, :] = (p * (1.0 / denom)).astype(BF16)

    for kv in range(N_KV_HEADS):
        sl = slice(kv * HEAD_DIM, (kv + 1) * HEAD_DIM)
        vcat = jnp.concatenate([vp_ref[:, sl], vc_ref[:, sl]], axis=0)
        for g in range(q_per_kv):
            h = kv * q_per_kv + g
            acc_ref[:, h * HEAD_DIM:(h + 1) * HEAD_DIM] = _dot(p_ref[h * blk:(h + 1) * blk, :], vcat)
    o_ref[...] = _rms(acc_ref[...], nw_ref[...]).astype(BF16)


def _swa(sinks, q, k, v, norm_w, batch, seq):
    t, qw = q.shape
    kw = k.shape[1]
    nb = seq // ATTN_BLOCK
    cur = lambda b, i: (b * nb + i, 0)
    prev = lambda b, i: (b * nb + jnp.maximum(i - 1, 0), 0)
    return pl.pallas_call(
        _swa_kernel,
        grid=(batch, nb),
        in_specs=[pl.BlockSpec(memory_space=pltpu.SMEM),
                  pl.BlockSpec((ATTN_BLOCK, qw), cur),
                  pl.BlockSpec((ATTN_BLOCK, kw), prev), pl.BlockSpec((ATTN_BLOCK, kw), cur),
                  pl.BlockSpec((ATTN_BLOCK, kw), prev), pl.BlockSpec((ATTN_BLOCK, kw), cur),
                  pl.BlockSpec((1, qw), lambda b, i: (0, 0))],
        out_specs=pl.BlockSpec((ATTN_BLOCK, qw), cur),
        out_shape=jax.ShapeDtypeStruct((t, qw), BF16),
        scratch_shapes=[pltpu.VMEM((qw // HEAD_DIM * ATTN_BLOCK, 2 * ATTN_BLOCK), F32),
                        pltpu.VMEM((qw // HEAD_DIM * ATTN_BLOCK, 2 * ATTN_BLOCK), BF16),
                        pltpu.VMEM((ATTN_BLOCK, 2 * ATTN_BLOCK), F32),
                        pltpu.VMEM((ATTN_BLOCK, qw), F32)],
        compiler_params=_cparams(("parallel", "parallel")),
        name="swa",
    )(sinks, q, k, k, v, v, norm_w)


SSD_TAIL = 16


def _split2(x):
    hi = x.astype(BF16)
    return hi, (x - hi.astype(F32)).astype(BF16)


def _ssd_kernel(xbc_ref, dt_ref, z_ref, cw_ref, cb_ref, dtb_ref, alog_ref, dskip_ref, nw_ref,
                tril_ref, shift_ref, expand_ref, o_ref, ext_ref, state_ref, y_ref):
    c = pl.program_id(1)
    L = SSD_CHUNK
    width = z_ref.shape[1]
    n_heads = width // SSD_HEAD_DIM
    heads_per_group = n_heads // SSD_GROUPS
    gw = width // SSD_GROUPS

    @pl.when(c == 0)
    def _():
        state_ref[...] = jnp.zeros_like(state_ref)
        ext_ref[0:SSD_TAIL, :] = jnp.zeros((SSD_TAIL, ext_ref.shape[1]), F32)

    cur = xbc_ref[...]
    ext_ref[SSD_TAIL:SSD_TAIL + L, :] = cur
    e_hi, e_lo = _split2(ext_ref[...])
    shifted = _dot(shift_ref[...], e_hi) + _dot(shift_ref[...], e_lo)
    ext_ref[0:SSD_TAIL, :] = cur[L - SSD_TAIL:, :]
    acc = cb_ref[...] + cw_ref[SSD_CONV - 1:SSD_CONV, :] * cur
    for jj in range(SSD_CONV - 1):
        acc = acc + cw_ref[jj:jj + 1, :] * shifted[jj * L:(jj + 1) * L, :]
    u = jax.nn.silu(acc)
    xs = u[:, :width]
    bm = u[:, width:width + SSD_GROUPS * SSD_STATE]
    cm = u[:, width + SSD_GROUPS * SSD_STATE:]

    dt = jax.nn.softplus(dt_ref[...] + dtb_ref[...])
    da = dt * (-jnp.exp(alog_ref[...]))
    d1 = da.astype(BF16)
    r1 = da - d1.astype(F32)
    d2 = r1.astype(BF16)
    d3 = (r1 - d2.astype(F32)).astype(BF16)
    cum12 = _dot(tril_ref[...], jnp.concatenate([d1, d2], axis=1))
    cum = cum12[:, :LANES] + (cum12[:, LANES:] + _dot(tril_ref[...], d3))
    cum_t = cum.T
    cum_last = cum[L - 1:L, :]
    e_cum = jnp.exp(cum)
    w_end = dt * jnp.exp(cum_last - cum)

    s_hi, s_lo = _split2(jnp.concatenate([dt, e_cum, w_end], axis=0))
    spread = _dot(s_hi, expand_ref[...]) + _dot(s_lo, expand_ref[...])
    dt_x, ecum_x, wend_x = spread[:L], spread[L:2 * L], spread[2 * L:]
    xc = (xs * dt_x).astype(BF16)
    xw = (xs * wend_x).astype(BF16)
    skip = dskip_ref[...] * xs

    row = lax.broadcasted_iota(jnp.int32, (L, L), 0)
    col = lax.broadcasted_iota(jnp.int32, (L, L), 1)
    causal = row >= col
    first_half = lax.broadcasted_iota(jnp.int32, (L, LANES), 1) < SSD_HEAD_DIM

    for g in range(SSD_GROUPS):
        gs = slice(g * gw, (g + 1) * gw)
        bg = bm[:, g * SSD_STATE:(g + 1) * SSD_STATE]
        cg = cm[:, g * SSD_STATE:(g + 1) * SSD_STATE].astype(BF16)
        cb = _dot_nt(cg, bg.astype(BF16))
        st = state_ref[:, gs]
        y_off = _dot(cg, st.astype(BF16)) * ecum_x[:, gs]
        state_ref[:, gs] = st * ecum_x[L - 1:L, gs] + _dot(bg.T.astype(BF16), xw[:, gs])
        for k in range(heads_per_group // 2):
            ps = slice(g * gw + k * LANES, g * gw + (k + 1) * LANES)
            halves = []
            for sub in range(2):
                h = g * heads_per_group + 2 * k + sub
                seg = cum[:, h:h + 1] - cum_t[h:h + 1, :]
                decay = jnp.exp(jnp.where(causal, seg, -jnp.inf))
                halves.append(_dot((cb * decay).astype(BF16), xc[:, ps]))
            y_ref[:, ps] = (jnp.where(first_half, halves[0], halves[1])
                            + y_off[:, k * LANES:(k + 1) * LANES] + skip[:, ps])

    gated = y_ref[...] * jax.nn.silu(z_ref[...].astype(F32))
    parts = []
    for g in range(SSD_GROUPS):
        gg = gated[:, g * gw:(g + 1) * gw]
        parts.append(gg * lax.rsqrt(jnp.mean(gg * gg, axis=-1, keepdims=True) + RMS_EPS))
    o_ref[...] = (jnp.concatenate(parts, axis=1) * nw_ref[...]).astype(BF16)


def _ssd(xbc, dt_raw, z, conv_w, conv_b, dt_bias, a_log, d_skip, norm_w, batch, seq):
    t, cw = xbc.shape
    width = z.shape[1]
    nc = seq // SSD_CHUNK
    L = SSD_CHUNK
    tril = jnp.asarray(np.tril(np.ones((L, L), np.float32)), dtype=BF16)
    shift = np.zeros(((SSD_CONV - 1) * L, SSD_TAIL + L), np.float32)
    for jj in range(SSD_CONV - 1):
        shift[jj * L + np.arange(L), SSD_TAIL - (SSD_CONV - 1) + jj + np.arange(L)] = 1.0
    shift = jnp.asarray(shift, dtype=BF16)
    expand = np.zeros((LANES, width), np.float32)
    for h in range(width // SSD_HEAD_DIM):
        expand[h, h * SSD_HEAD_DIM:(h + 1) * SSD_HEAD_DIM] = 1.0
    expand = jnp.asarray(expand, dtype=BF16)
    cur = lambda b, c: (b * nc + c, 0)
    full = lambda a: pl.BlockSpec(a.shape, lambda b, c: (0,) * a.ndim)
    return pl.pallas_call(
        _ssd_kernel,
        grid=(batch, nc),
        in_specs=[pl.BlockSpec((L, cw), cur), pl.BlockSpec((L, LANES), cur),
                  pl.BlockSpec((L, width), cur),
                  full(conv_w), full(conv_b), full(dt_bias), full(a_log), full(d_skip), full(norm_w),
                  full(tril), full(shift), full(expand)],
        out_specs=pl.BlockSpec((L, width), cur),
        out_shape=jax.ShapeDtypeStruct((t, width), BF16),
        scratch_shapes=[pltpu.VMEM((SSD_TAIL + L, cw), F32),
                        pltpu.VMEM((SSD_STATE, width), F32),
                        pltpu.VMEM((L, width), F32)],
        compiler_params=_cparams(("parallel", "arbitrary")),
        name="ssd",
    )(xbc, dt_raw, z, conv_w, conv_b, dt_bias, a_log, d_skip, norm_w, tril, shift, expand)


def _mem_kv_kernel(m_ref, nw_ref, w_ref, o_ref):
    o_ref[...] = _dot(_rms(m_ref[...], nw_ref[...]).astype(BF16), w_ref[...]).astype(BF16)


def _mem_kv(mem2d, norm_w, w_kv, rows):
    t, d = mem2d.shape
    n = w_kv.shape[1]
    return pl.pallas_call(
        _mem_kv_kernel,
        grid=(t // rows,),
        in_specs=[pl.BlockSpec((rows, d), lambda i: (i, 0)),
                  pl.BlockSpec((1, d), lambda i: (0, 0)),
                  pl.BlockSpec((d, n), lambda i: (0, 0))],
        out_specs=pl.BlockSpec((rows, n), lambda i: (i, 0)),
        out_shape=jax.ShapeDtypeStruct((t, n), BF16),
        compiler_params=_cparams(("parallel",)),
        name="mem_kv",
    )(mem2d, norm_w, w_kv)


def _post_kernel(x_ref, a_ref, s_ref, woa_ref, wos_ref, xnw_ref, wq_ref, kv_ref, wo_ref,
                 fnw_ref, wr_ref, br_ref, tril_ref,
                 x2_ref, pay_ref, meta_ref, cnt_ref, carry_ref):
    step = pl.program_id(0)
    tm = x_ref.shape[0]
    n_exp = N_GROUPS * EXPERTS_PER_GROUP

    @pl.when(step == 0)
    def _():
        carry_ref[...] = jnp.zeros_like(carry_ref)

    x1 = x_ref[...] + _dot(a_ref[...], woa_ref[...]) + _dot(s_ref[...], wos_ref[...])

    xw = wq_ref.shape[1]
    hd = xw // XATTN_HEADS
    q = (_dot(_rms(x1, xnw_ref[...]).astype(BF16), wq_ref[...]) * (hd ** -0.5)).astype(BF16)
    heads = []
    for h in range(XATTN_HEADS):
        s = _dot_nt(q[:, h * hd:(h + 1) * hd], kv_ref[:, h * hd:(h + 1) * hd])
        p = jnp.exp(s - jnp.max(s, axis=-1, keepdims=True))
        o = _dot(p.astype(BF16), kv_ref[:, xw + h * hd:xw + (h + 1) * hd])
        heads.append((o / jnp.sum(p, axis=-1, keepdims=True)).astype(BF16))
    x2 = x1 + _dot(jnp.concatenate(heads, axis=1), wo_ref[...])
    x2_ref[...] = x2

    h3 = _rms(x2, fnw_ref[...])
    h_hi = h3.astype(BF16)
    h_lo = (h3 - h_hi.astype(F32)).astype(BF16)
    t_hi = _dot(h_hi, wr_ref[...])
    logits = t_hi[:, :LANES] + (t_hi[:, LANES:] + _dot(h_lo, wr_ref[:, :LANES])) + br_ref[...]
    lane = lax.broadcasted_iota(jnp.int32, (tm, LANES), 1)
    big = jnp.int32(LANES)
    neg = -jnp.inf
    g_l = jnp.where((lane >= n_exp) & (lane < n_exp + N_GROUPS), logits, neg)
    g_max = jnp.max(g_l, axis=-1, keepdims=True)
    g_idx = jnp.min(jnp.where(g_l == g_max, lane - n_exp, big), axis=-1, keepdims=True)
    g_gate = 1.0 / jnp.sum(jnp.exp(g_l - g_max), axis=-1, keepdims=True)
    e_l = jnp.where((lane < n_exp) & ((lane // EXPERTS_PER_GROUP) == g_idx), logits, neg)
    m1 = jnp.max(e_l, axis=-1, keepdims=True)
    i1 = jnp.min(jnp.where(e_l == m1, lane, big), axis=-1, keepdims=True)
    e_l2 = jnp.where(lane == i1, neg, e_l)
    m2 = jnp.max(e_l2, axis=-1, keepdims=True)
    i2 = jnp.min(jnp.where(e_l2 == m2, lane, big), axis=-1, keepdims=True)
    e2 = jnp.exp(m2 - m1)
    w1 = (1.0 / (1.0 + e2)) * g_gate
    w2 = (e2 / (1.0 + e2)) * g_gate
    first_low = i1 < i2
    lo = jnp.where(first_low, i1, i2) % EXPERTS_PER_GROUP
    hi = jnp.where(first_low, i2, i1) % EXPERTS_PER_GROUP
    w_lo = jnp.where(first_low, w1, w2)
    w_hi = jnp.where(first_low, w2, w1)
    bucket = g_idx * (EXPERTS_PER_GROUP * EXPERTS_PER_GROUP) + lo * EXPERTS_PER_GROUP + hi

    blane = lax.broadcasted_iota(jnp.int32, (tm, N_BUCKETS), 1)
    onehot = (blane == bucket).astype(F32)
    before = _dot(tril_ref[...], onehot.astype(BF16)) + carry_ref[...]
    rank = jnp.sum(onehot * before, axis=-1, keepdims=True)
    carry_ref[...] = carry_ref[...] + jnp.sum(onehot, axis=0, keepdims=True)
    cnt_ref[...] = carry_ref[...]

    meta = jnp.where(lane == 0, w_lo, jnp.where(lane == 1, w_hi, jnp.where(
        lane == 2, bucket.astype(F32), jnp.where(lane == 3, rank, 0.0))))
    n_feat_rows = h3.shape[1] // LANES
    for c in range(n_feat_rows):
        pay_ref[pl.ds(c, tm, stride=SLAB_PITCH), :] = h3[:, c * LANES:(c + 1) * LANES]
    pay_ref[pl.ds(n_feat_rows, tm, stride=SLAB_PITCH), :] = meta
    meta_ref[...] = meta.T[:SUBLANES, :]


def _post(x2d, attn, ssd, woa, wos, xnw, wq, kv, wo, fnw, wr, br, tm, seq, mem_tokens):
    t, d = x2d.shape
    tiles_per_batch = seq // tm
    tril = jnp.asarray(np.tril(np.ones((tm, tm), np.float32), -1), dtype=BF16)
    full = lambda a: pl.BlockSpec(a.shape, lambda i: (0,) * a.ndim)
    row = lambda n: pl.BlockSpec((tm, n), lambda i: (i, 0))
    return pl.pallas_call(
        _post_kernel,
        grid=(t // tm,),
        in_specs=[row(d), row(attn.shape[1]), row(ssd.shape[1]), full(woa), full(wos), full(xnw), full(wq),
                  pl.BlockSpec((mem_tokens, kv.shape[1]), lambda i: (i // tiles_per_batch, 0)),
                  full(wo), full(fnw), full(wr), full(br), full(tril)],
        out_specs=[row(d), pl.BlockSpec((tm * SLAB_PITCH, LANES), lambda i: (i, 0)),
                   pl.BlockSpec((SUBLANES, tm), lambda i: (0, i)),
                   pl.BlockSpec((1, N_BUCKETS), lambda i: (0, 0))],
        out_shape=[jax.ShapeDtypeStruct((t, d), F32), jax.ShapeDtypeStruct((t * SLAB_PITCH, LANES), F32),
                   jax.ShapeDtypeStruct((SUBLANES, t), F32), jax.ShapeDtypeStruct((1, N_BUCKETS), F32)],
        scratch_shapes=[pltpu.VMEM((1, N_BUCKETS), F32)],
        compiler_params=_cparams(("arbitrary",)),
        name="post",
    )(x2d, attn, ssd, woa, wos, xnw, wq, kv, wo, fnw, wr, br, tril)


DMA_UNROLL = 8
DMA_PRIORITIES = 2


def _dispatch_kernel(dest_ref, pay_ref, xs_hbm, sem):
    rows = dest_ref.shape[-1]

    def copy(r):
        return pltpu.make_async_copy(pay_ref.at[pl.ds(r * SLAB_PITCH, SLAB_PITCH)],
                                     xs_hbm.at[pl.ds(dest_ref[0, r] * SLAB_PITCH, SLAB_PITCH)], sem)

    def issue(g, carry):
        for u in range(DMA_UNROLL):
            copy(g * DMA_UNROLL + u).start(priority=u % DMA_PRIORITIES)
        return carry

    lax.fori_loop(0, rows // DMA_UNROLL, issue, 0)

    def drain(g, carry):
        for u in range(DMA_UNROLL):
            copy(g * DMA_UNROLL + u).wait()
        return carry

    lax.fori_loop(0, rows // DMA_UNROLL, drain, 0)


def _dispatch(dest3d, payload, n_sorted):
    n_steps, _, rows = dest3d.shape
    return pl.pallas_call(
        _dispatch_kernel,
        grid=(n_steps,),
        in_specs=[pl.BlockSpec((None, 1, rows), lambda i: (i, 0, 0), memory_space=pltpu.SMEM),
                  pl.BlockSpec((rows * SLAB_PITCH, LANES), lambda i: (i, 0))],
        out_specs=pl.BlockSpec(memory_space=pl.ANY),
        out_shape=jax.ShapeDtypeStruct((n_sorted * SLAB_PITCH, LANES), payload.dtype),
        scratch_shapes=[pltpu.SemaphoreType.DMA(())],
        compiler_params=_cparams(("arbitrary",)),
        name="dispatch",
    )(dest3d, payload)


def _experts_kernel(grp_ref, ea_ref, eb_ref, blk_ref, valid_ref, xs_ref, wgu_ref, wd_ref, y_ref):
    i = pl.program_id(0)
    ff = wd_ref.shape[1]
    d = wd_ref.shape[2]
    n_feat_rows = d // LANES
    token_row = lambda c: pl.ds(c, MOE_TILE, stride=SLAB_PITCH)

    @pl.when(valid_ref[i] == 1)
    def _():
        x = jnp.concatenate([xs_ref[token_row(c), :].astype(BF16) for c in range(n_feat_rows)], axis=1)
        gates = xs_ref[token_row(n_feat_rows), :]
        y = None
        for e_ref, lane in ((ea_ref, 0), (eb_ref, 1)):
            e = e_ref[i]
            gu = _dot(x, wgu_ref[e])
            hid = (jax.nn.silu(gu[:, :ff]) * gu[:, ff:]).astype(BF16)
            part = gates[:, lane:lane + 1] * _dot(hid, wd_ref[e])
            y = part if y is None else y + part
        for c in range(n_feat_rows):
            y_ref[token_row(c), :] = y[:, c * LANES:(c + 1) * LANES]
        y_ref[token_row(n_feat_rows), :] = jnp.zeros((MOE_TILE, LANES), F32)


def _experts(tile_group, tile_a, tile_b, tile_blk, tile_valid, xs, wgu, wd):
    n_sorted = xs.shape[0] // SLAB_PITCH
    d = wd.shape[2]
    nt = n_sorted // MOE_TILE
    slab = lambda: pl.BlockSpec((MOE_TILE * SLAB_PITCH, LANES), lambda i, g, a, b, k, v: (k[i], 0))
    grid_spec = pltpu.PrefetchScalarGridSpec(
        num_scalar_prefetch=5,
        grid=(nt,),
        in_specs=[slab(),
                  pl.BlockSpec((EXPERTS_PER_GROUP,) + wgu.shape[1:], lambda i, g, a, b, k, v: (g[i], 0, 0)),
                  pl.BlockSpec((EXPERTS_PER_GROUP,) + wd.shape[1:], lambda i, g, a, b, k, v: (g[i], 0, 0))],
        out_specs=slab(),
    )
    return pl.pallas_call(
        _experts_kernel,
        grid_spec=grid_spec,
        out_shape=jax.ShapeDtypeStruct(xs.shape, F32),
        compiler_params=_cparams(("arbitrary",)),
        name="experts",
    )(tile_group, tile_a, tile_b, tile_blk, tile_valid, xs, wgu, wd)


COMBINE_ROW_BLOCK = 64


def _combine_kernel(dcur_ref, dnext_ref, x2_ref, nw_ref, ys_hbm, o_ref, buf_ref, sem):
    i = pl.program_id(0)
    n = pl.num_programs(0)
    rows, d = x2_ref.shape
    n_feat_rows = d // LANES
    slot = i % 2

    def gather(dref, s, start):
        def body(g, carry):
            for u in range(DMA_UNROLL):
                r = g * DMA_UNROLL + u
                cp = pltpu.make_async_copy(ys_hbm.at[pl.ds(dref[0, r] * SLAB_PITCH, n_feat_rows)],
                                           buf_ref.at[s, pl.ds(r * SLAB_PITCH, n_feat_rows)], sem.at[s])
                if start:
                    cp.start(priority=u % DMA_PRIORITIES)
                else:
                    cp.wait()
            return carry

        lax.fori_loop(0, rows // DMA_UNROLL, body, 0)

    @pl.when(i == 0)
    def _():
        gather(dcur_ref, slot, True)

    @pl.when(i + 1 < n)
    def _():
        gather(dnext_ref, 1 - slot, True)

    gather(dcur_ref, slot, False)

    def block(b, carry):
        r0 = pl.multiple_of(b * COMBINE_ROW_BLOCK, COMBINE_ROW_BLOCK)
        rs = pl.ds(r0, COMBINE_ROW_BLOCK)
        chunks = [x2_ref[rs, c * LANES:(c + 1) * LANES]
                  + buf_ref[slot, pl.ds(r0 * SLAB_PITCH + c, COMBINE_ROW_BLOCK, stride=SLAB_PITCH), :]
                  for c in range(n_feat_rows)]
        ssq = chunks[0] * chunks[0]
        for ch in chunks[1:]:
            ssq = ssq + ch * ch
        scale = lax.rsqrt(jnp.sum(ssq, axis=-1, keepdims=True) / d + RMS_EPS)
        for c, ch in enumerate(chunks):
            o_ref[rs, c * LANES:(c + 1) * LANES] = ch * scale * nw_ref[:, c * LANES:(c + 1) * LANES]
        return carry

    lax.fori_loop(0, rows // COMBINE_ROW_BLOCK, block, 0)


def _combine(dest3d, x2, norm_w, ys):
    n_steps, _, rows = dest3d.shape
    t, d = x2.shape
    return pl.pallas_call(
        _combine_kernel,
        grid=(n_steps,),
        in_specs=[pl.BlockSpec((None, 1, rows), lambda i: (i, 0, 0), memory_space=pltpu.SMEM),
                  pl.BlockSpec((None, 1, rows), lambda i: (jnp.minimum(i + 1, n_steps - 1), 0, 0),
                               memory_space=pltpu.SMEM),
                  pl.BlockSpec((rows, d), lambda i: (i, 0)),
                  pl.BlockSpec((1, d), lambda i: (0, 0)),
                  pl.BlockSpec(memory_space=pl.ANY)],
        out_specs=pl.BlockSpec((rows, d), lambda i: (i, 0)),
        out_shape=jax.ShapeDtypeStruct((t, d), F32),
        scratch_shapes=[pltpu.VMEM((2, rows * SLAB_PITCH, LANES), F32), pltpu.SemaphoreType.DMA((2,))],
        compiler_params=_cparams(("arbitrary",)),
        name="combine",
    )(dest3d, dest3d, x2, norm_w, ys)


def _pad_cols(w, n):
    return jnp.pad(w, ((0, 0), (0, n - w.shape[1])))


def _tile_plan(counts, n_tiles):
    per_bucket = (counts + (MOE_TILE - 1)) // MOE_TILE
    tile_end = jnp.cumsum(per_bucket)
    tile_start = tile_end - per_bucket
    total = tile_end[-1]
    ids = jnp.arange(n_tiles, dtype=jnp.int32)
    blk = jnp.minimum(ids, total - 1)
    bucket = jnp.sum((tile_end[None, :] <= blk[:, None]).astype(jnp.int32), axis=1)
    pair = EXPERTS_PER_GROUP * EXPERTS_PER_GROUP
    group = bucket // pair
    slot_a = (bucket % pair) // EXPERTS_PER_GROUP
    slot_b = bucket % EXPERTS_PER_GROUP
    valid = (ids < total).astype(jnp.int32)
    return tile_start * MOE_TILE, group, slot_a, slot_b, blk, valid


def kernel(x, mem, positions, mix_norm_w, w_in, attn_sinks, ssd_conv_w, ssd_conv_b, ssd_dt_bias, ssd_a_log, ssd_d, attn_out_norm_w, ssd_out_norm_w, w_out, xattn_norm_w, mem_norm_w, xattn_w_q, xattn_w_kv, xattn_w_o, ffn_norm_w, router_group_w, router_group_b, router_expert_w, router_expert_b, expert_w_gate, expert_w_up, expert_w_down, final_norm_w):
    batch, seq, d = x.shape
    mem_tokens = mem.shape[1]
    depth = w_in.shape[0]
    t = batch * seq
    attn_w = attn_out_norm_w.shape[1]
    ssd_w = ssd_out_norm_w.shape[1]
    kv_w = N_KV_HEADS * HEAD_DIM
    conv_dim = ssd_conv_w.shape[2]
    n_exp = router_expert_w.shape[2]
    assert n_exp == N_GROUPS * EXPERTS_PER_GROUP and router_group_w.shape[2] == N_GROUPS
    assert seq % ATTN_BLOCK == 0 and seq % SSD_CHUNK == 0 and t % MOE_TILE == 0
    tm = min(512, seq)
    dispatch_rows = min(2048, t)
    combine_rows = min(1024, t)
    n_tiles = t // MOE_TILE + N_USED_BUCKETS
    n_sorted = n_tiles * MOE_TILE

    pos_row = positions.reshape(1, t)
    x2d = x.reshape(t, d)
    mem2d = mem.reshape(batch * mem_tokens, d)
    row = lambda v: v.reshape(1, -1)

    for l in range(depth):
        widths = (attn_w, kv_w, kv_w, ssd_w, conv_dim, LANES)
        q, k, v, z, xbc, dt_raw = _in_proj(pos_row, x2d, row(mix_norm_w[l]),
                                           _pad_cols(w_in[l], sum(widths)).astype(BF16), widths, tm)
        attn = _swa(attn_sinks[l], q, k, v, row(attn_out_norm_w[l]), batch, seq)
        ssd = _ssd(xbc, dt_raw, z, ssd_conv_w[l], row(ssd_conv_b[l]),
                   _pad_cols(row(ssd_dt_bias[l]), LANES), _pad_cols(row(ssd_a_log[l]), LANES),
                   row(jnp.repeat(ssd_d[l], SSD_HEAD_DIM)), row(ssd_out_norm_w[l]), batch, seq)
        kv = _mem_kv(mem2d, row(mem_norm_w[l]), xattn_w_kv[l].astype(BF16), mem_tokens)

        wo = w_out[l].astype(BF16)
        wr32 = _pad_cols(jnp.concatenate([router_expert_w[l], router_group_w[l]], axis=1), LANES)
        wr_hi = wr32.astype(BF16)
        wr = jnp.concatenate([wr_hi, (wr32 - wr_hi.astype(F32)).astype(BF16)], axis=1)
        br = _pad_cols(row(jnp.concatenate([router_expert_b[l], router_group_b[l]])), LANES)
        x2, payload, meta, counts = _post(
            x2d, attn, ssd, wo[:attn_w], wo[attn_w:], row(xattn_norm_w[l]), xattn_w_q[l].astype(BF16), kv,
            xattn_w_o[l].astype(BF16), row(ffn_norm_w[l]), wr, br, tm, seq, mem_tokens)

        row_start, tile_group, tile_a, tile_b, tile_blk, tile_valid = _tile_plan(
            counts[0].astype(jnp.int32), n_tiles)
        bucket = meta[2].astype(jnp.int32)
        hit = bucket[:, None] == jnp.arange(N_BUCKETS, dtype=jnp.int32)[None, :]
        dest = jnp.sum(jnp.where(hit, row_start[None, :], 0), axis=1) + meta[3].astype(jnp.int32)
        xs = _dispatch(dest.reshape(t // dispatch_rows, 1, dispatch_rows), payload, n_sorted)
        wgu = jnp.concatenate([expert_w_gate[l], expert_w_up[l]], axis=2).astype(BF16)
        ys = _experts(tile_group, tile_a, tile_b, tile_blk, tile_valid, xs, wgu, expert_w_down[l].astype(BF16))
        assert depth == 1
        x2d = _combine(dest.reshape(t // combine_rows, 1, combine_rows), x2, row(final_norm_w), ys)

    return x2d.reshape(batch, seq, d)
```

```python
import functools

import numpy as np
import jax
import jax.numpy as jnp
from jax import lax
from jax.experimental import pallas as pl
from jax.experimental.pallas import tpu as pltpu

RMS_EPS = 1e-5
HEAD_DIM = 64
N_KV_HEADS = 2
WINDOW = 128
ATTN_BLOCK = 128
ROT_DIM = 16
ROT_HALF = ROT_DIM // 2
ROPE_THETA = 500000.0
SSD_HEAD_DIM = 64
SSD_GROUPS = 2
SSD_STATE = 128
SSD_CONV = 4
SSD_CHUNK = 128
XATTN_HEADS = 4
N_GROUPS = 4
EXPERTS_PER_GROUP = 8

LANES = 128
SUBLANES = 8
MOE_TILE = 256
SLAB_PITCH = SUBLANES + 1
N_BUCKETS = N_GROUPS * EXPERTS_PER_GROUP * EXPERTS_PER_GROUP
N_USED_BUCKETS = N_GROUPS * (EXPERTS_PER_GROUP * (EXPERTS_PER_GROUP - 1) // 2)
VMEM_LIMIT = 56 * 1024 * 1024

F32 = jnp.float32
BF16 = jnp.bfloat16


def _cparams(semantics):
    return pltpu.CompilerParams(dimension_semantics=semantics, vmem_limit_bytes=VMEM_LIMIT)


def _rms(x, w):
    return x * lax.rsqrt(jnp.mean(x * x, axis=-1, keepdims=True) + RMS_EPS) * w


def _dot(a, b):
    return jnp.dot(a, b, preferred_element_type=F32)


def _dot_nt(a, b):
    return lax.dot_general(a, b, (((1,), (1,)), ((), ())), preferred_element_type=F32)


def _in_proj_kernel(pos_ref, x_ref, nw_ref, w_ref, place_ref, base_ref,
                    q_ref, k_ref, v_ref, z_ref, xbc_ref, dt_ref):
    x = x_ref[...]
    hb = _rms(x, nw_ref[...]).astype(BF16)

    j = lax.broadcasted_iota(jnp.int32, (ROT_HALF, 1), 0).astype(F32)
    inv_freq = jnp.power(jnp.float32(ROPE_THETA), -(2.0 * j) / ROT_DIM)
    ang = pos_ref[...].astype(F32) * inv_freq
    cs = jnp.concatenate([jnp.cos(ang), jnp.sin(ang)], axis=0)

    cs_hi = cs.astype(BF16)
    cs_lo = (cs - cs_hi.astype(F32)).astype(BF16)
    tn = (((0,), (0,)), ((), ()))
    tabs = (lax.dot_general(cs_hi, place_ref[...], tn, preferred_element_type=F32)
            + lax.dot_general(cs_lo, place_ref[...], tn, preferred_element_type=F32))
    c_tab = tabs[:, :LANES] + base_ref[...]
    s_up = tabs[:, LANES:2 * LANES]
    s_dn = tabs[:, 2 * LANES:]

    def rope(t):
        n = t.shape[1]
        reps = n // LANES
        c = jnp.tile(c_tab, (1, reps)) if reps > 1 else c_tab
        su = jnp.tile(s_up, (1, reps)) if reps > 1 else s_up
        sd = jnp.tile(s_dn, (1, reps)) if reps > 1 else s_dn
        return t * c + pltpu.roll(t, n - ROT_HALF, 1) * su + pltpu.roll(t, ROT_HALF, 1) * sd

    proj = _dot(hb, w_ref[...])
    o = 0
    pieces = []
    for ref in (q_ref, k_ref, v_ref, z_ref, xbc_ref, dt_ref):
        pieces.append(proj[:, o:o + ref.shape[1]])
        o += ref.shape[1]
    q, k, v, z, xbc, dt = pieces
    q_ref[...] = (rope(q) * (HEAD_DIM ** -0.5)).astype(BF16)
    k_ref[...] = rope(k).astype(BF16)
    v_ref[...] = v.astype(BF16)
    z_ref[...] = z.astype(BF16)
    xbc_ref[...] = xbc
    dt_ref[...] = dt


def _rope_placement():
    d = np.arange(LANES) % HEAD_DIM
    jj = np.arange(ROT_HALF)[:, None]
    ec = ((d[None, :] < ROT_DIM) & ((d[None, :] % ROT_HALF) == jj)).astype(np.float32)
    eup = -((d[None, :] < ROT_HALF) & (d[None, :] == jj)).astype(np.float32)
    edn = ((d[None, :] >= ROT_HALF) & (d[None, :] < ROT_DIM) & ((d[None, :] - ROT_HALF) == jj)).astype(np.float32)
    zero = np.zeros_like(ec)
    place = np.concatenate([np.concatenate([ec, zero, zero], axis=1),
                            np.concatenate([zero, eup, edn], axis=1)], axis=0)
    base = (d >= ROT_DIM).astype(np.float32)[None, :]
    return jnp.asarray(place, dtype=BF16), jnp.asarray(base)


def _in_proj(pos_row, x2d, norm_w, w_all, widths, tm):
    t, d = x2d.shape
    place, base = _rope_placement()
    full = lambda a: pl.BlockSpec(a.shape, lambda i: (0,) * a.ndim)
    row = lambda n: pl.BlockSpec((tm, n), lambda i: (i, 0))
    dtypes = (BF16, BF16, BF16, BF16, F32, F32)
    return pl.pallas_call(
        _in_proj_kernel,
        grid=(t // tm,),
        in_specs=[pl.BlockSpec((1, tm), lambda i: (0, i)), row(d), full(norm_w), full(w_all),
                  full(place), full(base)],
        out_specs=[row(n) for n in widths],
        out_shape=[jax.ShapeDtypeStruct((t, n), dt) for n, dt in zip(widths, dtypes)],
        compiler_params=_cparams(("parallel",)),
        name="in_proj",
    )(pos_row, x2d, norm_w, w_all, place, base)


SWA_SUB = 4


def _swa_kernel(sink_ref, q_ref, kp_ref, kc_ref, vp_ref, vc_ref, nw_ref, o_ref,
                s_ref, p_ref, bias_ref, acc_ref):
    i = pl.program_id(1)
    blk = ATTN_BLOCK
    n_q_heads = q_ref.shape[1] // HEAD_DIM
    q_per_kv = n_q_heads // N_KV_HEADS
    qi = lax.broadcasted_iota(jnp.int32, (blk, 2 * blk), 0) + blk
    ki = lax.broadcasted_iota(jnp.int32, (blk, 2 * blk), 1)
    rel = qi - ki
    local = (rel >= 0) & (rel < WINDOW)
    bias_ref[0] = jnp.where(local & ((i > 0) | (ki >= blk)), 0.0, -jnp.inf)
    bias_ref[1] = jnp.where(local, 0.0, -jnp.inf)
    sink_col = ki == 0

    def band(prev_ref, cur_ref, j):
        if j == 0:
            return jnp.concatenate([prev_ref[...], cur_ref[0:blk, :]], axis=0)
        return cur_ref[(j - 1) * blk:(j + 1) * blk, :]

    for j in range(SWA_SUB):
        kband = band(kp_ref, kc_ref, j)
        for kv in range(N_KV_HEADS):
            kcat = kband[:, kv * HEAD_DIM:(kv + 1) * HEAD_DIM]
            for g in range(q_per_kv):
                h = kv * q_per_kv + g
                r0 = (j * n_q_heads + h) * blk
                s_ref[r0:r0 + blk, :] = _dot_nt(q_ref[j * blk:(j + 1) * blk, h * HEAD_DIM:(h + 1) * HEAD_DIM], kcat)

    for j in range(SWA_SUB):
        for h in range(n_q_heads):
            rows = slice((j * n_q_heads + h) * blk, (j * n_q_heads + h + 1) * blk)
            s = jnp.where(sink_col, sink_ref[h], s_ref[rows, :] + bias_ref[min(j, 1)])
            p_ref[rows, :] = jnp.exp(s - jnp.max(s, axis=-1, keepdims=True)).astype(BF16)

    ones = jnp.ones((2 * blk, LANES), BF16)
    low = lax.broadcasted_iota(jnp.int32, (blk, LANES), 1) < HEAD_DIM
    for j in range(SWA_SUB):
        vband = band(vp_ref, vc_ref, j)
        vband = jnp.where(lax.broadcasted_iota(jnp.int32, vband.shape, 0) == 0, jnp.zeros_like(vband), vband)
        for kv in range(N_KV_HEADS):
            v_kv = vband[:, kv * HEAD_DIM:(kv + 1) * HEAD_DIM]
            vaug = jnp.concatenate([v_kv, v_kv, ones], axis=1)
            for k in range(q_per_kv // 2):
                h = kv * q_per_kv + 2 * k
                r0 = (j * n_q_heads + h) * blk
                o_even = _dot(p_ref[r0:r0 + blk, :], vaug)
                o_odd = _dot(p_ref[r0 + blk:r0 + 2 * blk, :], vaug)
                num = jnp.where(low, o_even[:, :LANES], o_odd[:, :LANES])
                den = jnp.where(low, o_even[:, LANES:], o_odd[:, LANES:])
                acc_ref[j * blk:(j + 1) * blk, h * HEAD_DIM:(h + 2) * HEAD_DIM] = num * (1.0 / den)
    o_ref[...] = _rms(acc_ref[...], nw_ref[...]).astype(BF16)


def _swa(sinks, q, k, v, norm_w, batch, seq):
    t, qw = q.shape
    kw = k.shape[1]
    rows = SWA_SUB * ATTN_BLOCK
    nb = seq // rows
    n_heads = qw // HEAD_DIM
    cur = lambda b, i: (b * nb + i, 0)
    prev = lambda b, i: ((b * nb + i) * SWA_SUB - jnp.minimum(i, 1), 0)
    return pl.pallas_call(
        _swa_kernel,
        grid=(batch, nb),
        in_specs=[pl.BlockSpec(memory_space=pltpu.SMEM),
                  pl.BlockSpec((rows, qw), cur),
                  pl.BlockSpec((ATTN_BLOCK, kw), prev), pl.BlockSpec((rows, kw), cur),
                  pl.BlockSpec((ATTN_BLOCK, kw), prev), pl.BlockSpec((rows, kw), cur),
                  pl.BlockSpec((1, qw), lambda b, i: (0, 0))],
        out_specs=pl.BlockSpec((rows, qw), cur),
        out_shape=jax.ShapeDtypeStruct((t, qw), BF16),
        scratch_shapes=[pltpu.VMEM((SWA_SUB * n_heads * ATTN_BLOCK, 2 * ATTN_BLOCK), F32),
                        pltpu.VMEM((SWA_SUB * n_heads * ATTN_BLOCK, 2 * ATTN_BLOCK), BF16),
                        pltpu.VMEM((2, ATTN_BLOCK, 2 * ATTN_BLOCK), F32),
                        pltpu.VMEM((rows, qw), F32)],
        compiler_params=_cparams(("parallel", "parallel")),
        name="swa",
    )(sinks, q, k, k, v, v, norm_w)


SSD_TAIL = 16


def _split2(x):
    hi = x.astype(BF16)
    return hi, (x - hi.astype(F32)).astype(BF16)


def _ssd_kernel(xbc_ref, dt_ref, z_ref, cw_ref, cb_ref, dtb_ref, alog_ref, dskip_ref, nw_ref,
                tril_ref, shift_ref, expand_ref, o_ref, ext_ref, state_ref, y_ref):
    c = pl.program_id(1)
    L = SSD_CHUNK
    width = z_ref.shape[1]
    n_heads = width // SSD_HEAD_DIM
    heads_per_group = n_heads // SSD_GROUPS
    gw = width // SSD_GROUPS

    @pl.when(c == 0)
    def _():
        state_ref[...] = jnp.zeros_like(state_ref)
        ext_ref[0:SSD_TAIL, :] = jnp.zeros((SSD_TAIL, ext_ref.shape[1]), F32)

    cur = xbc_ref[...]
    ext_ref[SSD_TAIL:SSD_TAIL + L, :] = cur
    e_hi, e_lo = _split2(ext_ref[...])
    shifted = _dot(shift_ref[...], e_hi) + _dot(shift_ref[...], e_lo)
    ext_ref[0:SSD_TAIL, :] = cur[L - SSD_TAIL:, :]
    acc = cb_ref[...] + cw_ref[SSD_CONV - 1:SSD_CONV, :] * cur
    for jj in range(SSD_CONV - 1):
        acc = acc + cw_ref[jj:jj + 1, :] * shifted[jj * L:(jj + 1) * L, :]
    u = jax.nn.silu(acc)
    xs = u[:, :width]
    bm = u[:, width:width + SSD_GROUPS * SSD_STATE]
    cm = u[:, width + SSD_GROUPS * SSD_STATE:]

    dt = jax.nn.softplus(dt_ref[...] + dtb_ref[...])
    da = dt * (-jnp.exp(alog_ref[...]))
    d1 = da.astype(BF16)
    r1 = da - d1.astype(F32)
    d2 = r1.astype(BF16)
    d3 = (r1 - d2.astype(F32)).astype(BF16)
    cum12 = _dot(tril_ref[...], jnp.concatenate([d1, d2], axis=1))
    cum = cum12[:, :LANES] + (cum12[:, LANES:] + _dot(tril_ref[...], d3))
    cum_t = cum.T
    cum_last = cum[L - 1:L, :]
    e_cum = jnp.exp(cum)
    w_end = dt * jnp.exp(cum_last - cum)

    s_hi, s_lo = _split2(jnp.concatenate([dt, e_cum, w_end], axis=0))
    spread = _dot(s_hi, expand_ref[...]) + _dot(s_lo, expand_ref[...])
    dt_x, ecum_x, wend_x = spread[:L], spread[L:2 * L], spread[2 * L:]
    xc = (xs * dt_x).astype(BF16)
    xw = (xs * wend_x).astype(BF16)
    skip = dskip_ref[...] * xs

    row = lax.broadcasted_iota(jnp.int32, (L, L), 0)
    col = lax.broadcasted_iota(jnp.int32, (L, L), 1)
    causal = row >= col
    first_half = lax.broadcasted_iota(jnp.int32, (L, LANES), 1) < SSD_HEAD_DIM

    for g in range(SSD_GROUPS):
        gs = slice(g * gw, (g + 1) * gw)
        bg = bm[:, g * SSD_STATE:(g + 1) * SSD_STATE]
        cg = cm[:, g * SSD_STATE:(g + 1) * SSD_STATE].astype(BF16)
        cb = _dot_nt(cg, bg.astype(BF16))
        st = state_ref[:, gs]
        y_off = _dot(cg, st.astype(BF16)) * ecum_x[:, gs]
        state_ref[:, gs] = st * ecum_x[L - 1:L, gs] + _dot(bg.T.astype(BF16), xw[:, gs])
        for k in range(heads_per_group // 2):
            ps = slice(g * gw + k * LANES, g * gw + (k + 1) * LANES)
            halves = []
            for sub in range(2):
                h = g * heads_per_group + 2 * k + sub
                seg = cum[:, h:h + 1] - cum_t[h:h + 1, :]
                decay = jnp.exp(jnp.where(causal, seg, -jnp.inf))
                halves.append(_dot((cb * decay).astype(BF16), xc[:, ps]))
            y_ref[:, ps] = (jnp.where(first_half, halves[0], halves[1])
                            + y_off[:, k * LANES:(k + 1) * LANES] + skip[:, ps])

    gated = y_ref[...] * jax.nn.silu(z_ref[...].astype(F32))
    parts = []
    for g in range(SSD_GROUPS):
        gg = gated[:, g * gw:(g + 1) * gw]
        parts.append(gg * lax.rsqrt(jnp.mean(gg * gg, axis=-1, keepdims=True) + RMS_EPS))
    o_ref[...] = (jnp.concatenate(parts, axis=1) * nw_ref[...]).astype(BF16)


def _ssd(xbc, dt_raw, z, conv_w, conv_b, dt_bias, a_log, d_skip, norm_w, batch, seq):
    t, cw = xbc.shape
    width = z.shape[1]
    nc = seq // SSD_CHUNK
    L = SSD_CHUNK
    tril = jnp.asarray(np.tril(np.ones((L, L), np.float32)), dtype=BF16)
    shift = np.zeros(((SSD_CONV - 1) * L, SSD_TAIL + L), np.float32)
    for jj in range(SSD_CONV - 1):
        shift[jj * L + np.arange(L), SSD_TAIL - (SSD_CONV - 1) + jj + np.arange(L)] = 1.0
    shift = jnp.asarray(shift, dtype=BF16)
    expand = np.zeros((LANES, width), np.float32)
    for h in range(width // SSD_HEAD_DIM):
        expand[h, h * SSD_HEAD_DIM:(h + 1) * SSD_HEAD_DIM] = 1.0
    expand = jnp.asarray(expand, dtype=BF16)
    cur = lambda b, c: (b * nc + c, 0)
    full = lambda a: pl.BlockSpec(a.shape, lambda b, c: (0,) * a.ndim)
    return pl.pallas_call(
        _ssd_kernel,
        grid=(batch, nc),
        in_specs=[pl.BlockSpec((L, cw), cur), pl.BlockSpec((L, LANES), cur),
                  pl.BlockSpec((L, width), cur),
                  full(conv_w), full(conv_b), full(dt_bias), full(a_log), full(d_skip), full(norm_w),
                  full(tril), full(shift), full(expand)],
        out_specs=pl.BlockSpec((L, width), cur),
        out_shape=jax.ShapeDtypeStruct((t, width), BF16),
        scratch_shapes=[pltpu.VMEM((SSD_TAIL + L, cw), F32),
                        pltpu.VMEM((SSD_STATE, width), F32),
                        pltpu.VMEM((L, width), F32)],
        compiler_params=_cparams(("parallel", "arbitrary")),
        name="ssd",
    )(xbc, dt_raw, z, conv_w, conv_b, dt_bias, a_log, d_skip, norm_w, tril, shift, expand)


def _mem_kv_kernel(m_ref, nw_ref, w_ref, o_ref):
    o_ref[...] = _dot(_rms(m_ref[...], nw_ref[...]).astype(BF16), w_ref[...]).astype(BF16)


def _mem_kv(mem2d, norm_w, w_kv, rows):
    t, d = mem2d.shape
    n = w_kv.shape[1]
    return pl.pallas_call(
        _mem_kv_kernel,
        grid=(t // rows,),
        in_specs=[pl.BlockSpec((rows, d), lambda i: (i, 0)),
                  pl.BlockSpec((1, d), lambda i: (0, 0)),
                  pl.BlockSpec((d, n), lambda i: (0, 0))],
        out_specs=pl.BlockSpec((rows, n), lambda i: (i, 0)),
        out_shape=jax.ShapeDtypeStruct((t, n), BF16),
        compiler_params=_cparams(("parallel",)),
        name="mem_kv",
    )(mem2d, norm_w, w_kv)


def _post_kernel(x_ref, a_ref, s_ref, woa_ref, wos_ref, xnw_ref, wq_ref, kv_ref, wo_ref,
                 fnw_ref, wr_ref, br_ref, tril_ref,
                 x2_ref, pay_ref, meta_ref, cnt_ref, carry_ref):
    step = pl.program_id(0)
    tm = x_ref.shape[0]
    n_exp = N_GROUPS * EXPERTS_PER_GROUP

    @pl.when(step == 0)
    def _():
        carry_ref[...] = jnp.zeros_like(carry_ref)

    x1 = x_ref[...] + _dot(a_ref[...], woa_ref[...]) + _dot(s_ref[...], wos_ref[...])

    xw = wq_ref.shape[1]
    hd = xw // XATTN_HEADS
    q = (_dot(_rms(x1, xnw_ref[...]).astype(BF16), wq_ref[...]) * (hd ** -0.5)).astype(BF16)
    heads = []
    for h in range(XATTN_HEADS):
        s = _dot_nt(q[:, h * hd:(h + 1) * hd], kv_ref[:, h * hd:(h + 1) * hd])
        p = jnp.exp(s - jnp.max(s, axis=-1, keepdims=True))
        o = _dot(p.astype(BF16), kv_ref[:, xw + h * hd:xw + (h + 1) * hd])
        heads.append((o / jnp.sum(p, axis=-1, keepdims=True)).astype(BF16))
    x2 = x1 + _dot(jnp.concatenate(heads, axis=1), wo_ref[...])
    x2_ref[...] = x2

    h3 = _rms(x2, fnw_ref[...])
    h_hi = h3.astype(BF16)
    h_lo = (h3 - h_hi.astype(F32)).astype(BF16)
    t_hi = _dot(h_hi, wr_ref[...])
    logits = t_hi[:, :LANES] + (t_hi[:, LANES:] + _dot(h_lo, wr_ref[:, :LANES])) + br_ref[...]
    lane = lax.broadcasted_iota(jnp.int32, (tm, LANES), 1)
    big = jnp.int32(LANES)
    neg = -jnp.inf
    g_l = jnp.where((lane >= n_exp) & (lane < n_exp + N_GROUPS), logits, neg)
    g_max = jnp.max(g_l, axis=-1, keepdims=True)
    g_idx = jnp.min(jnp.where(g_l == g_max, lane - n_exp, big), axis=-1, keepdims=True)
    g_gate = 1.0 / jnp.sum(jnp.exp(g_l - g_max), axis=-1, keepdims=True)
    e_l = jnp.where((lane < n_exp) & ((lane // EXPERTS_PER_GROUP) == g_idx), logits, neg)
    m1 = jnp.max(e_l, axis=-1, keepdims=True)
    i1 = jnp.min(jnp.where(e_l == m1, lane, big), axis=-1, keepdims=True)
    e_l2 = jnp.where(lane == i1, neg, e_l)
    m2 = jnp.max(e_l2, axis=-1, keepdims=True)
    i2 = jnp.min(jnp.where(e_l2 == m2, lane, big), axis=-1, keepdims=True)
    e2 = jnp.exp(m2 - m1)
    w1 = (1.0 / (1.0 + e2)) * g_gate
    w2 = (e2 / (1.0 + e2)) * g_gate
    first_low = i1 < i2
    lo = jnp.where(first_low, i1, i2) % EXPERTS_PER_GROUP
    hi = jnp.where(first_low, i2, i1) % EXPERTS_PER_GROUP
    w_lo = jnp.where(first_low, w1, w2)
    w_hi = jnp.where(first_low, w2, w1)
    bucket = g_idx * (EXPERTS_PER_GROUP * EXPERTS_PER_GROUP) + lo * EXPERTS_PER_GROUP + hi

    blane = lax.broadcasted_iota(jnp.int32, (tm, N_BUCKETS), 1)
    onehot = (blane == bucket).astype(F32)
    before = _dot(tril_ref[...], onehot.astype(BF16)) + carry_ref[...]
    rank = jnp.sum(onehot * before, axis=-1, keepdims=True)
    carry_ref[...] = carry_ref[...] + jnp.sum(onehot, axis=0, keepdims=True)
    cnt_ref[...] = carry_ref[...]

    meta = jnp.where(lane == 0, w_lo, jnp.where(lane == 1, w_hi, jnp.where(
        lane == 2, bucket.astype(F32), jnp.where(lane == 3, rank, 0.0))))
    n_feat_rows = h3.shape[1] // LANES
    for c in range(n_feat_rows):
        pay_ref[pl.ds(c, tm, stride=SLAB_PITCH), :] = h3[:, c * LANES:(c + 1) * LANES]
    pay_ref[pl.ds(n_feat_rows, tm, stride=SLAB_PITCH), :] = meta
    meta_ref[...] = meta.T[:SUBLANES, :]


def _post(x2d, attn, ssd, woa, wos, xnw, wq, kv, wo, fnw, wr, br, tm, seq, mem_tokens):
    t, d = x2d.shape
    tiles_per_batch = seq // tm
    tril = jnp.asarray(np.tril(np.ones((tm, tm), np.float32), -1), dtype=BF16)
    full = lambda a: pl.BlockSpec(a.shape, lambda i: (0,) * a.ndim)
    row = lambda n: pl.BlockSpec((tm, n), lambda i: (i, 0))
    return pl.pallas_call(
        _post_kernel,
        grid=(t // tm,),
        in_specs=[row(d), row(attn.shape[1]), row(ssd.shape[1]), full(woa), full(wos), full(xnw), full(wq),
                  pl.BlockSpec((mem_tokens, kv.shape[1]), lambda i: (i // tiles_per_batch, 0)),
                  full(wo), full(fnw), full(wr), full(br), full(tril)],
        out_specs=[row(d), pl.BlockSpec((tm * SLAB_PITCH, LANES), lambda i: (i, 0)),
                   pl.BlockSpec((SUBLANES, tm), lambda i: (0, i)),
                   pl.BlockSpec((1, N_BUCKETS), lambda i: (0, 0))],
        out_shape=[jax.ShapeDtypeStruct((t, d), F32), jax.ShapeDtypeStruct((t * SLAB_PITCH, LANES), F32),
                   jax.ShapeDtypeStruct((SUBLANES, t), F32), jax.ShapeDtypeStruct((1, N_BUCKETS), F32)],
        scratch_shapes=[pltpu.VMEM((1, N_BUCKETS), F32)],
        compiler_params=_cparams(("arbitrary",)),
        name="post",
    )(x2d, attn, ssd, woa, wos, xnw, wq, kv, wo, fnw, wr, br, tril)


DMA_UNROLL = 8
DMA_PRIORITIES = 2


def _dispatch_kernel(dest_ref, pay_ref, xs_hbm, sem):
    rows = dest_ref.shape[-1]

    def copy(r):
        return pltpu.make_async_copy(pay_ref.at[pl.ds(r * SLAB_PITCH, SLAB_PITCH)],
                                     xs_hbm.at[pl.ds(dest_ref[0, r], SLAB_PITCH)], sem)

    def issue(g, carry):
        for u in range(DMA_UNROLL):
            copy(g * DMA_UNROLL + u).start(priority=u % DMA_PRIORITIES)
        return carry

    lax.fori_loop(0, rows // DMA_UNROLL, issue, 0)

    def drain(g, carry):
        for u in range(DMA_UNROLL):
            copy(g * DMA_UNROLL + u).wait()
        return carry

    lax.fori_loop(0, rows // DMA_UNROLL, drain, 0)


def _dispatch(dest3d, payload, n_sorted):
    n_steps, _, rows = dest3d.shape
    return pl.pallas_call(
        _dispatch_kernel,
        grid=(n_steps,),
        in_specs=[pl.BlockSpec((None, 1, rows), lambda i: (i, 0, 0), memory_space=pltpu.SMEM),
                  pl.BlockSpec((rows * SLAB_PITCH, LANES), lambda i: (i, 0))],
        out_specs=pl.BlockSpec(memory_space=pl.ANY),
        out_shape=jax.ShapeDtypeStruct((n_sorted * SLAB_PITCH, LANES), payload.dtype),
        scratch_shapes=[pltpu.SemaphoreType.DMA(())],
        compiler_params=_cparams(("arbitrary",)),
        name="dispatch",
    )(dest3d, payload)


def _experts_kernel(grp_ref, ea_ref, eb_ref, blk_ref, valid_ref, xs_ref, wgu_ref, wd_ref, y_ref):
    i = pl.program_id(0)
    ff = wd_ref.shape[1]
    d = wd_ref.shape[2]
    n_feat_rows = d // LANES
    token_row = lambda c: pl.ds(c, MOE_TILE, stride=SLAB_PITCH)

    @pl.when(valid_ref[i] == 1)
    def _():
        x = jnp.concatenate([xs_ref[token_row(c), :].astype(BF16) for c in range(n_feat_rows)], axis=1)
        gates = xs_ref[token_row(n_feat_rows), :]
        y = None
        for e_ref, lane in ((ea_ref, 0), (eb_ref, 1)):
            e = e_ref[i]
            gu = _dot(x, wgu_ref[e])
            hid = (jax.nn.silu(gu[:, :ff]) * gu[:, ff:]).astype(BF16)
            part = gates[:, lane:lane + 1] * _dot(hid, wd_ref[e])
            y = part if y is None else y + part
        for c in range(n_feat_rows):
            y_ref[token_row(c), :] = y[:, c * LANES:(c + 1) * LANES]
        y_ref[token_row(n_feat_rows), :] = jnp.zeros((MOE_TILE, LANES), F32)


def _experts(tile_group, tile_a, tile_b, tile_blk, tile_valid, xs, wgu, wd):
    n_sorted = xs.shape[0] // SLAB_PITCH
    d = wd.shape[2]
    nt = n_sorted // MOE_TILE
    slab = lambda: pl.BlockSpec((MOE_TILE * SLAB_PITCH, LANES), lambda i, g, a, b, k, v: (k[i], 0))
    grid_spec = pltpu.PrefetchScalarGridSpec(
        num_scalar_prefetch=5,
        grid=(nt,),
        in_specs=[slab(),
                  pl.BlockSpec((EXPERTS_PER_GROUP,) + wgu.shape[1:], lambda i, g, a, b, k, v: (g[i], 0, 0)),
                  pl.BlockSpec((EXPERTS_PER_GROUP,) + wd.shape[1:], lambda i, g, a, b, k, v: (g[i], 0, 0))],
        out_specs=slab(),
    )
    return pl.pallas_call(
        _experts_kernel,
        grid_spec=grid_spec,
        out_shape=jax.ShapeDtypeStruct(xs.shape, F32),
        compiler_params=_cparams(("arbitrary",)),
        name="experts",
    )(tile_group, tile_a, tile_b, tile_blk, tile_valid, xs, wgu, wd)


COMBINE_ROW_BLOCK = 64
COMBINE_UNROLL = 4


def _combine_kernel(dcur_ref, dnext_ref, x2_ref, nw_ref, ys_hbm, o_ref, buf_ref, sem):
    i = pl.program_id(0)
    n = pl.num_programs(0)
    rows, d = x2_ref.shape
    n_feat_rows = d // LANES
    slot = i % 2

    def gather(dref, s, start):
        def body(g, carry):
            for u in range(DMA_UNROLL):
                r = g * DMA_UNROLL + u
                cp = pltpu.make_async_copy(ys_hbm.at[pl.ds(dref[0, r], n_feat_rows)],
                                           buf_ref.at[s, pl.ds(r * SLAB_PITCH, n_feat_rows)], sem.at[s])
                if start:
                    cp.start(priority=u % DMA_PRIORITIES)
                else:
                    cp.wait()
            return carry

        lax.fori_loop(0, rows // DMA_UNROLL, body, 0)

    @pl.when(i == 0)
    def _():
        gather(dcur_ref, slot, True)

    @pl.when(i + 1 < n)
    def _():
        gather(dnext_ref, 1 - slot, True)

    gather(dcur_ref, slot, False)

    def block(b, carry):
        r0 = pl.multiple_of(b * COMBINE_ROW_BLOCK, COMBINE_ROW_BLOCK)
        rs = pl.ds(r0, COMBINE_ROW_BLOCK)
        chunks = [x2_ref[rs, c * LANES:(c + 1) * LANES]
                  + buf_ref[slot, pl.ds(r0 * SLAB_PITCH + c, COMBINE_ROW_BLOCK, stride=SLAB_PITCH), :]
                  for c in range(n_feat_rows)]
        ssq = chunks[0] * chunks[0]
        for ch in chunks[1:]:
            ssq = ssq + ch * ch
        scale = lax.rsqrt(jnp.sum(ssq, axis=-1, keepdims=True) / d + RMS_EPS)
        for c, ch in enumerate(chunks):
            o_ref[rs, c * LANES:(c + 1) * LANES] = ch * scale * nw_ref[:, c * LANES:(c + 1) * LANES]
        return carry

    lax.fori_loop(0, rows // COMBINE_ROW_BLOCK, block, 0, unroll=COMBINE_UNROLL)


def _combine(dest3d, x2, norm_w, ys):
    n_steps, _, rows = dest3d.shape
    t, d = x2.shape
    return pl.pallas_call(
        _combine_kernel,
        grid=(n_steps,),
        in_specs=[pl.BlockSpec((None, 1, rows), lambda i: (i, 0, 0), memory_space=pltpu.SMEM),
                  pl.BlockSpec((None, 1, rows), lambda i: (jnp.minimum(i + 1, n_steps - 1), 0, 0),
                               memory_space=pltpu.SMEM),
                  pl.BlockSpec((rows, d), lambda i: (i, 0)),
                  pl.BlockSpec((1, d), lambda i: (0, 0)),
                  pl.BlockSpec(memory_space=pl.ANY)],
        out_specs=pl.BlockSpec((rows, d), lambda i: (i, 0)),
        out_shape=jax.ShapeDtypeStruct((t, d), F32),
        scratch_shapes=[pltpu.VMEM((2, rows * SLAB_PITCH, LANES), F32), pltpu.SemaphoreType.DMA((2,))],
        compiler_params=_cparams(("arbitrary",)),
        name="combine",
    )(dest3d, dest3d, x2, norm_w, ys)


def _pad_cols(w, n):
    return jnp.pad(w, ((0, 0), (0, n - w.shape[1])))


def _tile_plan(counts, n_tiles):
    per_bucket = (counts + (MOE_TILE - 1)) // MOE_TILE
    tile_end = jnp.cumsum(per_bucket)
    tile_start = tile_end - per_bucket
    total = tile_end[-1]
    ids = jnp.arange(n_tiles, dtype=jnp.int32)
    blk = jnp.minimum(ids, total - 1)
    bucket = jnp.sum((tile_end[None, :] <= blk[:, None]).astype(jnp.int32), axis=1)
    pair = EXPERTS_PER_GROUP * EXPERTS_PER_GROUP
    group = bucket // pair
    slot_a = (bucket % pair) // EXPERTS_PER_GROUP
    slot_b = bucket % EXPERTS_PER_GROUP
    valid = (ids < total).astype(jnp.int32)
    return tile_start * MOE_TILE, group, slot_a, slot_b, blk, valid


def kernel(x, mem, positions, mix_norm_w, w_in, attn_sinks, ssd_conv_w, ssd_conv_b, ssd_dt_bias, ssd_a_log, ssd_d, attn_out_norm_w, ssd_out_norm_w, w_out, xattn_norm_w, mem_norm_w, xattn_w_q, xattn_w_kv, xattn_w_o, ffn_norm_w, router_group_w, router_group_b, router_expert_w, router_expert_b, expert_w_gate, expert_w_up, expert_w_down, final_norm_w):
    batch, seq, d = x.shape
    mem_tokens = mem.shape[1]
    depth = w_in.shape[0]
    t = batch * seq
    attn_w = attn_out_norm_w.shape[1]
    ssd_w = ssd_out_norm_w.shape[1]
    kv_w = N_KV_HEADS * HEAD_DIM
    conv_dim = ssd_conv_w.shape[2]
    n_exp = router_expert_w.shape[2]
    assert n_exp == N_GROUPS * EXPERTS_PER_GROUP and router_group_w.shape[2] == N_GROUPS
    assert seq % ATTN_BLOCK == 0 and seq % SSD_CHUNK == 0 and t % MOE_TILE == 0
    tm = min(512, seq)
    dispatch_rows = min(2048, t)
    combine_rows = min(1024, t)
    n_tiles = t // MOE_TILE + N_USED_BUCKETS
    n_sorted = n_tiles * MOE_TILE

    pos_row = positions.reshape(1, t)
    x2d = x.reshape(t, d)
    mem2d = mem.reshape(batch * mem_tokens, d)
    row = lambda v: v.reshape(1, -1)

    for l in range(depth):
        widths = (attn_w, kv_w, kv_w, ssd_w, conv_dim, LANES)
        q, k, v, z, xbc, dt_raw = _in_proj(pos_row, x2d, row(mix_norm_w[l]),
                                           _pad_cols(w_in[l], sum(widths)).astype(BF16), widths, tm)
        attn = _swa(attn_sinks[l], q, k, v, row(attn_out_norm_w[l]), batch, seq)
        ssd = _ssd(xbc, dt_raw, z, ssd_conv_w[l], row(ssd_conv_b[l]),
                   _pad_cols(row(ssd_dt_bias[l]), LANES), _pad_cols(row(ssd_a_log[l]), LANES),
                   row(jnp.repeat(ssd_d[l], SSD_HEAD_DIM)), row(ssd_out_norm_w[l]), batch, seq)
        kv = _mem_kv(mem2d, row(mem_norm_w[l]), xattn_w_kv[l].astype(BF16), mem_tokens)

        wo = w_out[l].astype(BF16)
        wr32 = _pad_cols(jnp.concatenate([router_expert_w[l], router_group_w[l]], axis=1), LANES)
        wr_hi = wr32.astype(BF16)
        wr = jnp.concatenate([wr_hi, (wr32 - wr_hi.astype(F32)).astype(BF16)], axis=1)
        br = _pad_cols(row(jnp.concatenate([router_expert_b[l], router_group_b[l]])), LANES)
        x2, payload, meta, counts = _post(
            x2d, attn, ssd, wo[:attn_w], wo[attn_w:], row(xattn_norm_w[l]), xattn_w_q[l].astype(BF16), kv,
            xattn_w_o[l].astype(BF16), row(ffn_norm_w[l]), wr, br, tm, seq, mem_tokens)

        row_start, tile_group, tile_a, tile_b, tile_blk, tile_valid = _tile_plan(
            counts[0].astype(jnp.int32), n_tiles)
        bucket = meta[2].astype(jnp.int32)
        hit = bucket[:, None] == jnp.arange(N_BUCKETS, dtype=jnp.int32)[None, :]
        dest = jnp.sum(jnp.where(hit, row_start[None, :], 0), axis=1) + meta[3].astype(jnp.int32)
        dest = dest * SLAB_PITCH
        xs = _dispatch(dest.reshape(t // dispatch_rows, 1, dispatch_rows), payload, n_sorted)
        wgu = jnp.concatenate([expert_w_gate[l], expert_w_up[l]], axis=2).astype(BF16)
        ys = _experts(tile_group, tile_a, tile_b, tile_blk, tile_valid, xs, wgu, expert_w_down[l].astype(BF16))
        assert depth == 1
        x2d = _combine(dest.reshape(t // combine_rows, 1, combine_rows), x2, row(final_norm_w), ys)

    return x2d.reshape(batch, seq, d)
```

```python
import functools

import numpy as np
import jax
import jax.numpy as jnp
from jax import lax
from jax.experimental import pallas as pl
from jax.experimental.pallas import tpu as pltpu

RMS_EPS = 1e-5
HEAD_DIM = 64
N_KV_HEADS = 2
WINDOW = 128
ATTN_BLOCK = 128
ROT_DIM = 16
ROT_HALF = ROT_DIM // 2
ROPE_THETA = 500000.0
SSD_HEAD_DIM = 64
SSD_GROUPS = 2
SSD_STATE = 128
SSD_CONV = 4
SSD_CHUNK = 128
XATTN_HEADS = 4
N_GROUPS = 4
EXPERTS_PER_GROUP = 8

LANES = 128
SUBLANES = 8
MOE_TILE = 256
SLAB_PITCH = SUBLANES + 1
N_BUCKETS = N_GROUPS * EXPERTS_PER_GROUP * EXPERTS_PER_GROUP
N_USED_BUCKETS = N_GROUPS * (EXPERTS_PER_GROUP * (EXPERTS_PER_GROUP - 1) // 2)
VMEM_LIMIT = 56 * 1024 * 1024

F32 = jnp.float32
BF16 = jnp.bfloat16


def _cparams(semantics):
    return pltpu.CompilerParams(dimension_semantics=semantics, vmem_limit_bytes=VMEM_LIMIT)


def _rms(x, w):
    return x * lax.rsqrt(jnp.mean(x * x, axis=-1, keepdims=True) + RMS_EPS) * w


def _dot(a, b):
    return jnp.dot(a, b, preferred_element_type=F32)


def _dot_nt(a, b):
    return lax.dot_general(a, b, (((1,), (1,)), ((), ())), preferred_element_type=F32)


def _in_proj_kernel(pos_ref, x_ref, nw_ref, w_ref, place_ref, base_ref,
                    q_ref, k_ref, v_ref, z_ref, xbc_ref, dt_ref):
    x = x_ref[...]
    hb = _rms(x, nw_ref[...]).astype(BF16)

    j = lax.broadcasted_iota(jnp.int32, (ROT_HALF, 1), 0).astype(F32)
    inv_freq = jnp.power(jnp.float32(ROPE_THETA), -(2.0 * j) / ROT_DIM)
    ang = pos_ref[...].astype(F32) * inv_freq
    cs = jnp.concatenate([jnp.cos(ang), jnp.sin(ang)], axis=0)

    cs_hi = cs.astype(BF16)
    cs_lo = (cs - cs_hi.astype(F32)).astype(BF16)
    tn = (((0,), (0,)), ((), ()))
    tabs = (lax.dot_general(cs_hi, place_ref[...], tn, preferred_element_type=F32)
            + lax.dot_general(cs_lo, place_ref[...], tn, preferred_element_type=F32))
    c_tab = tabs[:, :LANES] + base_ref[...]
    s_tab = tabs[:, LANES:]
    first = (lax.broadcasted_iota(jnp.int32, s_tab.shape, 1) % HEAD_DIM) < ROT_HALF
    s_up = jnp.where(first, s_tab, 0.0)
    s_dn = jnp.where(first, 0.0, s_tab)

    def rope(t):
        n = t.shape[1]
        reps = n // LANES
        c = jnp.tile(c_tab, (1, reps)) if reps > 1 else c_tab
        su = jnp.tile(s_up, (1, reps)) if reps > 1 else s_up
        sd = jnp.tile(s_dn, (1, reps)) if reps > 1 else s_dn
        return t * c + pltpu.roll(t, n - ROT_HALF, 1) * su + pltpu.roll(t, ROT_HALF, 1) * sd

    proj = _dot(hb, w_ref[...])
    o = 0
    pieces = []
    for ref in (q_ref, k_ref, v_ref, z_ref, xbc_ref, dt_ref):
        pieces.append(proj[:, o:o + ref.shape[1]])
        o += ref.shape[1]
    q, k, v, z, xbc, dt = pieces
    q_ref[...] = (rope(q) * (HEAD_DIM ** -0.5)).astype(BF16)
    k_ref[...] = rope(k).astype(BF16)
    v_ref[...] = v.astype(BF16)
    z_ref[...] = z.astype(BF16)
    xbc_ref[...] = xbc
    dt_ref[...] = dt


def _rope_placement():
    d = np.arange(LANES) % HEAD_DIM
    jj = np.arange(ROT_HALF)[:, None]
    ec = ((d[None, :] < ROT_DIM) & ((d[None, :] % ROT_HALF) == jj)).astype(np.float32)
    eup = -((d[None, :] < ROT_HALF) & (d[None, :] == jj)).astype(np.float32)
    edn = ((d[None, :] >= ROT_HALF) & (d[None, :] < ROT_DIM) & ((d[None, :] - ROT_HALF) == jj)).astype(np.float32)
    zero = np.zeros_like(ec)
    place = np.concatenate([np.concatenate([ec, zero], axis=1),
                            np.concatenate([zero, eup + edn], axis=1)], axis=0)
    base = (d >= ROT_DIM).astype(np.float32)[None, :]
    return jnp.asarray(place, dtype=BF16), jnp.asarray(base)


def _in_proj(pos_row, x2d, norm_w, w_all, widths, tm):
    t, d = x2d.shape
    place, base = _rope_placement()
    full = lambda a: pl.BlockSpec(a.shape, lambda i: (0,) * a.ndim)
    row = lambda n: pl.BlockSpec((tm, n), lambda i: (i, 0))
    dtypes = (BF16, BF16, BF16, BF16, F32, F32)
    return pl.pallas_call(
        _in_proj_kernel,
        grid=(t // tm,),
        in_specs=[pl.BlockSpec((1, tm), lambda i: (0, i)), row(d), full(norm_w), full(w_all),
                  full(place), full(base)],
        out_specs=[row(n) for n in widths],
        out_shape=[jax.ShapeDtypeStruct((t, n), dt) for n, dt in zip(widths, dtypes)],
        compiler_params=_cparams(("parallel",)),
        name="in_proj",
    )(pos_row, x2d, norm_w, w_all, place, base)


SWA_SUB = 4


def _swa_kernel(sink_ref, q_ref, kp_ref, kc_ref, vp_ref, vc_ref, nw_ref, o_ref,
                s_ref, p_ref, bias_ref, acc_ref):
    i = pl.program_id(1)
    blk = ATTN_BLOCK
    n_q_heads = q_ref.shape[1] // HEAD_DIM
    q_per_kv = n_q_heads // N_KV_HEADS
    qi = lax.broadcasted_iota(jnp.int32, (blk, 2 * blk), 0) + blk
    ki = lax.broadcasted_iota(jnp.int32, (blk, 2 * blk), 1)
    rel = qi - ki
    local = (rel >= 0) & (rel < WINDOW)
    bias_ref[0] = jnp.where(local & ((i > 0) | (ki >= blk)), 0.0, -jnp.inf)
    bias_ref[1] = jnp.where(local, 0.0, -jnp.inf)
    sink_col = ki == 0

    def band(prev_ref, cur_ref, j):
        if j == 0:
            return jnp.concatenate([prev_ref[...], cur_ref[0:blk, :]], axis=0)
        return cur_ref[(j - 1) * blk:(j + 1) * blk, :]

    for j in range(SWA_SUB):
        kband = band(kp_ref, kc_ref, j)
        for kv in range(N_KV_HEADS):
            kcat = kband[:, kv * HEAD_DIM:(kv + 1) * HEAD_DIM]
            for g in range(q_per_kv):
                h = kv * q_per_kv + g
                r0 = (j * n_q_heads + h) * blk
                s_ref[r0:r0 + blk, :] = _dot_nt(q_ref[j * blk:(j + 1) * blk, h * HEAD_DIM:(h + 1) * HEAD_DIM], kcat)

    for j in range(SWA_SUB):
        for h in range(n_q_heads):
            rows = slice((j * n_q_heads + h) * blk, (j * n_q_heads + h + 1) * blk)
            s = jnp.where(sink_col, sink_ref[h], s_ref[rows, :] + bias_ref[min(j, 1)])
            p_ref[rows, :] = jnp.exp(s - jnp.max(s, axis=-1, keepdims=True)).astype(BF16)

    ones = jnp.ones((2 * blk, LANES), BF16)
    low = lax.broadcasted_iota(jnp.int32, (blk, LANES), 1) < HEAD_DIM
    for j in range(SWA_SUB):
        vband = band(vp_ref, vc_ref, j)
        vband = jnp.where(lax.broadcasted_iota(jnp.int32, vband.shape, 0) == 0, jnp.zeros_like(vband), vband)
        for kv in range(N_KV_HEADS):
            v_kv = vband[:, kv * HEAD_DIM:(kv + 1) * HEAD_DIM]
            vaug = jnp.concatenate([v_kv, v_kv, ones], axis=1)
            for k in range(q_per_kv // 2):
                h = kv * q_per_kv + 2 * k
                r0 = (j * n_q_heads + h) * blk
                o_even = _dot(p_ref[r0:r0 + blk, :], vaug)
                o_odd = _dot(p_ref[r0 + blk:r0 + 2 * blk, :], vaug)
                num = jnp.where(low, o_even[:, :LANES], o_odd[:, :LANES])
                den = jnp.where(low, o_even[:, LANES:], o_odd[:, LANES:])
                acc_ref[j * blk:(j + 1) * blk, h * HEAD_DIM:(h + 2) * HEAD_DIM] = num * (1.0 / den)
    o_ref[...] = _rms(acc_ref[...], nw_ref[...]).astype(BF16)


def _swa(sinks, q, k, v, norm_w, batch, seq):
    t, qw = q.shape
    kw = k.shape[1]
    rows = SWA_SUB * ATTN_BLOCK
    nb = seq // rows
    n_heads = qw // HEAD_DIM
    cur = lambda b, i: (b * nb + i, 0)
    prev = lambda b, i: ((b * nb + i) * SWA_SUB - jnp.minimum(i, 1), 0)
    return pl.pallas_call(
        _swa_kernel,
        grid=(batch, nb),
        in_specs=[pl.BlockSpec(memory_space=pltpu.SMEM),
                  pl.BlockSpec((rows, qw), cur),
                  pl.BlockSpec((ATTN_BLOCK, kw), prev), pl.BlockSpec((rows, kw), cur),
                  pl.BlockSpec((ATTN_BLOCK, kw), prev), pl.BlockSpec((rows, kw), cur),
                  pl.BlockSpec((1, qw), lambda b, i: (0, 0))],
        out_specs=pl.BlockSpec((rows, qw), cur),
        out_shape=jax.ShapeDtypeStruct((t, qw), BF16),
        scratch_shapes=[pltpu.VMEM((SWA_SUB * n_heads * ATTN_BLOCK, 2 * ATTN_BLOCK), F32),
                        pltpu.VMEM((SWA_SUB * n_heads * ATTN_BLOCK, 2 * ATTN_BLOCK), BF16),
                        pltpu.VMEM((2, ATTN_BLOCK, 2 * ATTN_BLOCK), F32),
                        pltpu.VMEM((rows, qw), F32)],
        compiler_params=_cparams(("parallel", "parallel")),
        name="swa",
    )(sinks, q, k, k, v, v, norm_w)


SSD_TAIL = 16
SSD_BATCH = 4


def _split2(x):
    hi = x.astype(BF16)
    return hi, (x - hi.astype(F32)).astype(BF16)


def _ssd_kernel(xbc_ref, dt_ref, z_ref, cw_ref, cb_ref, dtb_ref, alog_ref, dskip_ref, nw_ref,
                tril_ref, shift_ref, expand_ref, o_ref, ext_ref, state_ref, y_ref):
    @pl.when(pl.program_id(1) == 0)
    def _():
        state_ref[...] = jnp.zeros_like(state_ref)
        ext_ref[:, 0:SSD_TAIL, :] = jnp.zeros((SSD_BATCH, SSD_TAIL, ext_ref.shape[2]), F32)

    for slot in range(SSD_BATCH):
        _ssd_chunk(xbc_ref.at[slot], dt_ref.at[slot], z_ref.at[slot], cw_ref, cb_ref, dtb_ref, alog_ref,
                   dskip_ref, nw_ref, tril_ref, shift_ref, expand_ref, o_ref.at[slot],
                   ext_ref.at[slot], state_ref.at[slot], y_ref.at[slot])


def _ssd_chunk(xbc_ref, dt_ref, z_ref, cw_ref, cb_ref, dtb_ref, alog_ref, dskip_ref, nw_ref,
               tril_ref, shift_ref, expand_ref, o_ref, ext_ref, state_ref, y_ref):
    L = SSD_CHUNK
    width = z_ref.shape[1]
    n_heads = width // SSD_HEAD_DIM
    heads_per_group = n_heads // SSD_GROUPS
    gw = width // SSD_GROUPS

    cur = xbc_ref[...]
    ext_ref[SSD_TAIL:SSD_TAIL + L, :] = cur
    shifted = _dot(shift_ref[...], ext_ref[...].astype(BF16))
    ext_ref[0:SSD_TAIL, :] = cur[L - SSD_TAIL:, :]
    acc = cb_ref[...] + cw_ref[SSD_CONV - 1:SSD_CONV, :] * cur
    for jj in range(SSD_CONV - 1):
        acc = acc + cw_ref[jj:jj + 1, :] * shifted[jj * L:(jj + 1) * L, :]
    u = jax.nn.silu(acc)
    xs = u[:, :width]
    bm = u[:, width:width + SSD_GROUPS * SSD_STATE]
    cm = u[:, width + SSD_GROUPS * SSD_STATE:]

    dt = jax.nn.softplus(dt_ref[...] + dtb_ref[...])
    da = dt * (-jnp.exp(alog_ref[...]))
    d1 = da.astype(BF16)
    r1 = da - d1.astype(F32)
    d2 = r1.astype(BF16)
    d3 = (r1 - d2.astype(F32)).astype(BF16)
    cum12 = _dot(tril_ref[...], jnp.concatenate([d1, d2], axis=1))
    cum = cum12[:, :LANES] + (cum12[:, LANES:] + _dot(tril_ref[...], d3))
    cum_t = cum.T
    cum_last = cum[L - 1:L, :]
    e_cum = jnp.exp(cum)
    w_end = dt * jnp.exp(cum_last - cum)

    s_hi, s_lo = _split2(jnp.concatenate([dt, e_cum, w_end], axis=0))
    spread = _dot(s_hi, expand_ref[...]) + _dot(s_lo, expand_ref[...])
    dt_x, ecum_x, wend_x = spread[:L], spread[L:2 * L], spread[2 * L:]
    xc = (xs * dt_x).astype(BF16)
    xw = (xs * wend_x).astype(BF16)
    skip = dskip_ref[...] * xs

    row = lax.broadcasted_iota(jnp.int32, (L, L), 0)
    col = lax.broadcasted_iota(jnp.int32, (L, L), 1)
    causal = row >= col
    first_half = lax.broadcasted_iota(jnp.int32, (L, LANES), 1) < SSD_HEAD_DIM

    for g in range(SSD_GROUPS):
        gs = slice(g * gw, (g + 1) * gw)
        bg = bm[:, g * SSD_STATE:(g + 1) * SSD_STATE]
        cg = cm[:, g * SSD_STATE:(g + 1) * SSD_STATE].astype(BF16)
        cb = _dot_nt(cg, bg.astype(BF16))
        st = state_ref[:, gs]
        y_off = _dot(cg, st.astype(BF16)) * ecum_x[:, gs]
        state_ref[:, gs] = st * ecum_x[L - 1:L, gs] + _dot(bg.T.astype(BF16), xw[:, gs])
        for k in range(heads_per_group // 2):
            ps = slice(g * gw + k * LANES, g * gw + (k + 1) * LANES)
            halves = []
            for sub in range(2):
                h = g * heads_per_group + 2 * k + sub
                seg = cum[:, h:h + 1] - cum_t[h:h + 1, :]
                decay = jnp.exp(jnp.where(causal, seg, -jnp.inf))
                halves.append(_dot((cb * decay).astype(BF16), xc[:, ps]))
            y_ref[:, ps] = (jnp.where(first_half, halves[0], halves[1])
                            + y_off[:, k * LANES:(k + 1) * LANES] + skip[:, ps])

    gated = y_ref[...] * jax.nn.silu(z_ref[...].astype(F32))
    parts = []
    for g in range(SSD_GROUPS):
        gg = gated[:, g * gw:(g + 1) * gw]
        parts.append(gg * lax.rsqrt(jnp.mean(gg * gg, axis=-1, keepdims=True) + RMS_EPS))
    o_ref[...] = (jnp.concatenate(parts, axis=1) * nw_ref[...]).astype(BF16)


def _ssd(xbc, dt_raw, z, conv_w, conv_b, dt_bias, a_log, d_skip, norm_w, batch, seq):
    t, cw = xbc.shape
    width = z.shape[1]
    nc = seq // SSD_CHUNK
    L = SSD_CHUNK
    tril = jnp.asarray(np.tril(np.ones((L, L), np.float32)), dtype=BF16)
    shift = np.zeros(((SSD_CONV - 1) * L, SSD_TAIL + L), np.float32)
    for jj in range(SSD_CONV - 1):
        shift[jj * L + np.arange(L), SSD_TAIL - (SSD_CONV - 1) + jj + np.arange(L)] = 1.0
    shift = jnp.asarray(shift, dtype=BF16)
    expand = np.zeros((LANES, width), np.float32)
    for h in range(width // SSD_HEAD_DIM):
        expand[h, h * SSD_HEAD_DIM:(h + 1) * SSD_HEAD_DIM] = 1.0
    expand = jnp.asarray(expand, dtype=BF16)
    cur = lambda b, c: (b, c, 0)
    full = lambda a: pl.BlockSpec(a.shape, lambda b, c: (0,) * a.ndim)
    per_seq = lambda a: a.reshape(batch, seq, a.shape[1])
    out = pl.pallas_call(
        _ssd_kernel,
        grid=(batch // SSD_BATCH, nc),
        in_specs=[pl.BlockSpec((SSD_BATCH, L, cw), cur), pl.BlockSpec((SSD_BATCH, L, LANES), cur),
                  pl.BlockSpec((SSD_BATCH, L, width), cur),
                  full(conv_w), full(conv_b), full(dt_bias), full(a_log), full(d_skip), full(norm_w),
                  full(tril), full(shift), full(expand)],
        out_specs=pl.BlockSpec((SSD_BATCH, L, width), cur),
        out_shape=jax.ShapeDtypeStruct((batch, seq, width), BF16),
        scratch_shapes=[pltpu.VMEM((SSD_BATCH, SSD_TAIL + L, cw), F32),
                        pltpu.VMEM((SSD_BATCH, SSD_STATE, width), F32),
                        pltpu.VMEM((SSD_BATCH, L, width), F32)],
        compiler_params=_cparams(("parallel", "arbitrary")),
        name="ssd",
    )(per_seq(xbc), per_seq(dt_raw), per_seq(z), conv_w, conv_b, dt_bias, a_log, d_skip, norm_w,
      tril, shift, expand)
    return out.reshape(t, width)


def _mem_kv_kernel(m_ref, nw_ref, w_ref, o_ref):
    o_ref[...] = _dot(_rms(m_ref[...], nw_ref[...]).astype(BF16), w_ref[...]).astype(BF16)


def _mem_kv(mem2d, norm_w, w_kv, rows):
    t, d = mem2d.shape
    n = w_kv.shape[1]
    return pl.pallas_call(
        _mem_kv_kernel,
        grid=(t // rows,),
        in_specs=[pl.BlockSpec((rows, d), lambda i: (i, 0)),
                  pl.BlockSpec((1, d), lambda i: (0, 0)),
                  pl.BlockSpec((d, n), lambda i: (0, 0))],
        out_specs=pl.BlockSpec((rows, n), lambda i: (i, 0)),
        out_shape=jax.ShapeDtypeStruct((t, n), BF16),
        compiler_params=_cparams(("parallel",)),
        name="mem_kv",
    )(mem2d, norm_w, w_kv)


def _post_kernel(x_ref, a_ref, s_ref, woa_ref, wos_ref, xnw_ref, wq_ref, kv_ref, wo_ref,
                 fnw_ref, wr_ref, br_ref, tril_ref,
                 x2_ref, pay_ref, meta_ref, cnt_ref, carry_ref):
    step = pl.program_id(0)
    tm = x_ref.shape[0]
    n_exp = N_GROUPS * EXPERTS_PER_GROUP

    @pl.when(step == 0)
    def _():
        carry_ref[...] = jnp.zeros_like(carry_ref)

    x1 = x_ref[...] + _dot(a_ref[...], woa_ref[...]) + _dot(s_ref[...], wos_ref[...])

    xw = wq_ref.shape[1]
    hd = xw // XATTN_HEADS
    q = (_dot(_rms(x1, xnw_ref[...]).astype(BF16), wq_ref[...]) * (hd ** -0.5)).astype(BF16)
    heads = []
    for h in range(XATTN_HEADS):
        s = _dot_nt(q[:, h * hd:(h + 1) * hd], kv_ref[:, h * hd:(h + 1) * hd])
        p = jnp.exp(s - jnp.max(s, axis=-1, keepdims=True))
        o = _dot(p.astype(BF16), kv_ref[:, xw + h * hd:xw + (h + 1) * hd])
        heads.append((o / jnp.sum(p, axis=-1, keepdims=True)).astype(BF16))
    x2 = x1 + _dot(jnp.concatenate(heads, axis=1), wo_ref[...])
    x2_ref[...] = x2

    h3 = _rms(x2, fnw_ref[...])
    h_hi = h3.astype(BF16)
    h_lo = (h3 - h_hi.astype(F32)).astype(BF16)
    t_hi = _dot(h_hi, wr_ref[...])
    logits = t_hi[:, :LANES] + (t_hi[:, LANES:] + _dot(h_lo, wr_ref[:, :LANES])) + br_ref[...]
    lane = lax.broadcasted_iota(jnp.int32, (tm, LANES), 1)
    big = jnp.int32(LANES)
    neg = -jnp.inf
    g_l = jnp.where((lane >= n_exp) & (lane < n_exp + N_GROUPS), logits, neg)
    g_max = jnp.max(g_l, axis=-1, keepdims=True)
    g_idx = jnp.min(jnp.where(g_l == g_max, lane - n_exp, big), axis=-1, keepdims=True)
    g_gate = 1.0 / jnp.sum(jnp.exp(g_l - g_max), axis=-1, keepdims=True)
    e_l = jnp.where((lane < n_exp) & ((lane // EXPERTS_PER_GROUP) == g_idx), logits, neg)
    m1 = jnp.max(e_l, axis=-1, keepdims=True)
    i1 = jnp.min(jnp.where(e_l == m1, lane, big), axis=-1, keepdims=True)
    e_l2 = jnp.where(lane == i1, neg, e_l)
    m2 = jnp.max(e_l2, axis=-1, keepdims=True)
    i2 = jnp.min(jnp.where(e_l2 == m2, lane, big), axis=-1, keepdims=True)
    e2 = jnp.exp(m2 - m1)
    w1 = (1.0 / (1.0 + e2)) * g_gate
    w2 = (e2 / (1.0 + e2)) * g_gate
    first_low = i1 < i2
    lo = jnp.where(first_low, i1, i2) % EXPERTS_PER_GROUP
    hi = jnp.where(first_low, i2, i1) % EXPERTS_PER_GROUP
    w_lo = jnp.where(first_low, w1, w2)
    w_hi = jnp.where(first_low, w2, w1)
    bucket = g_idx * (EXPERTS_PER_GROUP * EXPERTS_PER_GROUP) + lo * EXPERTS_PER_GROUP + hi

    blane = lax.broadcasted_iota(jnp.int32, (tm, N_BUCKETS), 1)
    onehot = (blane == bucket).astype(F32)
    before = _dot(tril_ref[...], onehot.astype(BF16)) + carry_ref[...]
    rank = jnp.sum(onehot * before, axis=-1, keepdims=True)
    carry_ref[...] = carry_ref[...] + jnp.sum(onehot, axis=0, keepdims=True)
    cnt_ref[...] = carry_ref[...]

    meta = jnp.where(lane == 0, w_lo, jnp.where(lane == 1, w_hi, jnp.where(
        lane == 2, bucket.astype(F32), jnp.where(lane == 3, rank, 0.0))))
    n_feat_rows = h3.shape[1] // LANES
    for c in range(n_feat_rows):
        pay_ref[pl.ds(c, tm, stride=SLAB_PITCH), :] = h3[:, c * LANES:(c + 1) * LANES]
    pay_ref[pl.ds(n_feat_rows, tm, stride=SLAB_PITCH), :] = meta
    meta_ref[...] = meta.T[:SUBLANES, :]


def _post(x2d, attn, ssd, woa, wos, xnw, wq, kv, wo, fnw, wr, br, tm, seq, mem_tokens):
    t, d = x2d.shape
    tiles_per_batch = seq // tm
    tril = jnp.asarray(np.tril(np.ones((tm, tm), np.float32), -1), dtype=BF16)
    full = lambda a: pl.BlockSpec(a.shape, lambda i: (0,) * a.ndim)
    row = lambda n: pl.BlockSpec((tm, n), lambda i: (i, 0))
    return pl.pallas_call(
        _post_kernel,
        grid=(t // tm,),
        in_specs=[row(d), row(attn.shape[1]), row(ssd.shape[1]), full(woa), full(wos), full(xnw), full(wq),
                  pl.BlockSpec((mem_tokens, kv.shape[1]), lambda i: (i // tiles_per_batch, 0)),
                  full(wo), full(fnw), full(wr), full(br), full(tril)],
        out_specs=[row(d), pl.BlockSpec((tm * SLAB_PITCH, LANES), lambda i: (i, 0)),
                   pl.BlockSpec((SUBLANES, tm), lambda i: (0, i)),
                   pl.BlockSpec((1, N_BUCKETS), lambda i: (0, 0))],
        out_shape=[jax.ShapeDtypeStruct((t, d), F32), jax.ShapeDtypeStruct((t * SLAB_PITCH, LANES), F32),
                   jax.ShapeDtypeStruct((SUBLANES, t), F32), jax.ShapeDtypeStruct((1, N_BUCKETS), F32)],
        scratch_shapes=[pltpu.VMEM((1, N_BUCKETS), F32)],
        compiler_params=_cparams(("arbitrary",)),
        name="post",
    )(x2d, attn, ssd, woa, wos, xnw, wq, kv, wo, fnw, wr, br, tril)


DMA_UNROLL = 8
DMA_PRIORITIES = 2


def _dispatch_kernel(dest_ref, pay_ref, xs_hbm, sem):
    rows = dest_ref.shape[-1]

    def copy(r):
        return pltpu.make_async_copy(pay_ref.at[pl.ds(r * SLAB_PITCH, SLAB_PITCH)],
                                     xs_hbm.at[pl.ds(dest_ref[0, r], SLAB_PITCH)], sem)

    def issue(g, carry):
        for u in range(DMA_UNROLL):
            copy(g * DMA_UNROLL + u).start(priority=u % DMA_PRIORITIES)
        return carry

    lax.fori_loop(0, rows // DMA_UNROLL, issue, 0)

    def drain(g, carry):
        for u in range(DMA_UNROLL):
            copy(g * DMA_UNROLL + u).wait()
        return carry

    lax.fori_loop(0, rows // DMA_UNROLL, drain, 0)


def _dispatch(dest3d, payload, n_sorted):
    n_steps, _, rows = dest3d.shape
    return pl.pallas_call(
        _dispatch_kernel,
        grid=(n_steps,),
        in_specs=[pl.BlockSpec((None, 1, rows), lambda i: (i, 0, 0), memory_space=pltpu.SMEM),
                  pl.BlockSpec((rows * SLAB_PITCH, LANES), lambda i: (i, 0))],
        out_specs=pl.BlockSpec(memory_space=pl.ANY),
        out_shape=jax.ShapeDtypeStruct((n_sorted * SLAB_PITCH, LANES), payload.dtype),
        scratch_shapes=[pltpu.SemaphoreType.DMA(())],
        compiler_params=_cparams(("arbitrary",)),
        name="dispatch",
    )(dest3d, payload)


def _experts_kernel(grp_ref, ea_ref, eb_ref, blk_ref, valid_ref, xs_ref, wgu_ref, wd_ref, y_ref):
    i = pl.program_id(0)
    ff = wd_ref.shape[1]
    d = wd_ref.shape[2]
    n_feat_rows = d // LANES
    token_row = lambda c: pl.ds(c, MOE_TILE, stride=SLAB_PITCH)

    @pl.when(valid_ref[i] == 1)
    def _():
        x = jnp.concatenate([xs_ref[token_row(c), :].astype(BF16) for c in range(n_feat_rows)], axis=1)
        gates = xs_ref[token_row(n_feat_rows), :]
        y = None
        for e_ref, lane in ((ea_ref, 0), (eb_ref, 1)):
            e = e_ref[i]
            gu = _dot(x, wgu_ref[e])
            hid = (jax.nn.silu(gu[:, :ff]) * gu[:, ff:]).astype(BF16)
            part = gates[:, lane:lane + 1] * _dot(hid, wd_ref[e])
            y = part if y is None else y + part
        for c in range(n_feat_rows):
            y_ref[token_row(c), :] = y[:, c * LANES:(c + 1) * LANES]
        y_ref[token_row(n_feat_rows), :] = jnp.zeros((MOE_TILE, LANES), F32)


def _experts(tile_group, tile_a, tile_b, tile_blk, tile_valid, xs, wgu, wd):
    n_sorted = xs.shape[0] // SLAB_PITCH
    d = wd.shape[2]
    nt = n_sorted // MOE_TILE
    slab = lambda: pl.BlockSpec((MOE_TILE * SLAB_PITCH, LANES), lambda i, g, a, b, k, v: (k[i], 0))
    grid_spec = pltpu.PrefetchScalarGridSpec(
        num_scalar_prefetch=5,
        grid=(nt,),
        in_specs=[slab(),
                  pl.BlockSpec((EXPERTS_PER_GROUP,) + wgu.shape[1:], lambda i, g, a, b, k, v: (g[i], 0, 0)),
                  pl.BlockSpec((EXPERTS_PER_GROUP,) + wd.shape[1:], lambda i, g, a, b, k, v: (g[i], 0, 0))],
        out_specs=slab(),
    )
    return pl.pallas_call(
        _experts_kernel,
        grid_spec=grid_spec,
        out_shape=jax.ShapeDtypeStruct(xs.shape, F32),
        compiler_params=_cparams(("arbitrary",)),
        name="experts",
    )(tile_group, tile_a, tile_b, tile_blk, tile_valid, xs, wgu, wd)


COMBINE_ROW_BLOCK = 64
COMBINE_UNROLL = 4


def _combine_kernel(dcur_ref, dnext_ref, x2_ref, nw_ref, ys_hbm, o_ref, buf_ref, sem):
    i = pl.program_id(0)
    n = pl.num_programs(0)
    rows, d = x2_ref.shape
    n_feat_rows = d // LANES
    slot = i % 2

    def gather(dref, s, start):
        def body(g, carry):
            for u in range(DMA_UNROLL):
                r = g * DMA_UNROLL + u
                cp = pltpu.make_async_copy(ys_hbm.at[pl.ds(dref[0, r], n_feat_rows)],
                                           buf_ref.at[s, pl.ds(r * SLAB_PITCH, n_feat_rows)], sem.at[s])
                if start:
                    cp.start(priority=u % DMA_PRIORITIES)
                else:
                    cp.wait()
            return carry

        lax.fori_loop(0, rows // DMA_UNROLL, body, 0)

    @pl.when(i == 0)
    def _():
        gather(dcur_ref, slot, True)

    @pl.when(i + 1 < n)
    def _():
        gather(dnext_ref, 1 - slot, True)

    gather(dcur_ref, slot, False)

    def block(b, carry):
        r0 = pl.multiple_of(b * COMBINE_ROW_BLOCK, COMBINE_ROW_BLOCK)
        rs = pl.ds(r0, COMBINE_ROW_BLOCK)
        chunks = [x2_ref[rs, c * LANES:(c + 1) * LANES]
                  + buf_ref[slot, pl.ds(r0 * SLAB_PITCH + c, COMBINE_ROW_BLOCK, stride=SLAB_PITCH), :]
                  for c in range(n_feat_rows)]
        ssq = chunks[0] * chunks[0]
        for ch in chunks[1:]:
            ssq = ssq + ch * ch
        scale = lax.rsqrt(jnp.sum(ssq, axis=-1, keepdims=True) / d + RMS_EPS)
        for c, ch in enumerate(chunks):
            o_ref[rs, c * LANES:(c + 1) * LANES] = ch * scale * nw_ref[:, c * LANES:(c + 1) * LANES]
        return carry

    lax.fori_loop(0, rows // COMBINE_ROW_BLOCK, block, 0, unroll=COMBINE_UNROLL)


def _combine(dest3d, x2, norm_w, ys):
    n_steps, _, rows = dest3d.shape
    t, d = x2.shape
    return pl.pallas_call(
        _combine_kernel,
        grid=(n_steps,),
        in_specs=[pl.BlockSpec((None, 1, rows), lambda i: (i, 0, 0), memory_space=pltpu.SMEM),
                  pl.BlockSpec((None, 1, rows), lambda i: (jnp.minimum(i + 1, n_steps - 1), 0, 0),
                               memory_space=pltpu.SMEM),
                  pl.BlockSpec((rows, d), lambda i: (i, 0)),
                  pl.BlockSpec((1, d), lambda i: (0, 0)),
                  pl.BlockSpec(memory_space=pl.ANY)],
        out_specs=pl.BlockSpec((rows, d), lambda i: (i, 0)),
        out_shape=jax.ShapeDtypeStruct((t, d), F32),
        scratch_shapes=[pltpu.VMEM((2, rows * SLAB_PITCH, LANES), F32), pltpu.SemaphoreType.DMA((2,))],
        compiler_params=_cparams(("arbitrary",)),
        name="combine",
    )(dest3d, dest3d, x2, norm_w, ys)


def _pad_cols(w, n):
    return jnp.pad(w, ((0, 0), (0, n - w.shape[1])))


def _tile_plan(counts, n_tiles):
    per_bucket = (counts + (MOE_TILE - 1)) // MOE_TILE
    tile_end = jnp.cumsum(per_bucket)
    tile_start = tile_end - per_bucket
    total = tile_end[-1]
    ids = jnp.arange(n_tiles, dtype=jnp.int32)
    blk = jnp.minimum(ids, total - 1)
    bucket = jnp.sum((tile_end[None, :] <= blk[:, None]).astype(jnp.int32), axis=1)
    pair = EXPERTS_PER_GROUP * EXPERTS_PER_GROUP
    group = bucket // pair
    slot_a = (bucket % pair) // EXPERTS_PER_GROUP
    slot_b = bucket % EXPERTS_PER_GROUP
    valid = (ids < total).astype(jnp.int32)
    return tile_start * MOE_TILE, group, slot_a, slot_b, blk, valid


def kernel(x, mem, positions, mix_norm_w, w_in, attn_sinks, ssd_conv_w, ssd_conv_b, ssd_dt_bias, ssd_a_log, ssd_d, attn_out_norm_w, ssd_out_norm_w, w_out, xattn_norm_w, mem_norm_w, xattn_w_q, xattn_w_kv, xattn_w_o, ffn_norm_w, router_group_w, router_group_b, router_expert_w, router_expert_b, expert_w_gate, expert_w_up, expert_w_down, final_norm_w):
    batch, seq, d = x.shape
    mem_tokens = mem.shape[1]
    depth = w_in.shape[0]
    t = batch * seq
    attn_w = attn_out_norm_w.shape[1]
    ssd_w = ssd_out_norm_w.shape[1]
    kv_w = N_KV_HEADS * HEAD_DIM
    conv_dim = ssd_conv_w.shape[2]
    n_exp = router_expert_w.shape[2]
    assert n_exp == N_GROUPS * EXPERTS_PER_GROUP and router_group_w.shape[2] == N_GROUPS
    assert seq % (SWA_SUB * ATTN_BLOCK) == 0 and seq % SSD_CHUNK == 0 and t % MOE_TILE == 0
    assert batch % SSD_BATCH == 0
    tm = min(512, seq)
    dispatch_rows = min(2048, t)
    combine_rows = min(1024, t)
    n_tiles = t // MOE_TILE + N_USED_BUCKETS
    n_sorted = n_tiles * MOE_TILE

    pos_row = positions.reshape(1, t)
    x2d = x.reshape(t, d)
    mem2d = mem.reshape(batch * mem_tokens, d)
    row = lambda v: v.reshape(1, -1)

    for l in range(depth):
        widths = (attn_w, kv_w, kv_w, ssd_w, conv_dim, LANES)
        q, k, v, z, xbc, dt_raw = _in_proj(pos_row, x2d, row(mix_norm_w[l]),
                                           _pad_cols(w_in[l], sum(widths)).astype(BF16), widths, tm)
        attn = _swa(attn_sinks[l], q, k, v, row(attn_out_norm_w[l]), batch, seq)
        ssd = _ssd(xbc, dt_raw, z, ssd_conv_w[l], row(ssd_conv_b[l]),
                   _pad_cols(row(ssd_dt_bias[l]), LANES), _pad_cols(row(ssd_a_log[l]), LANES),
                   row(jnp.repeat(ssd_d[l], SSD_HEAD_DIM)), row(ssd_out_norm_w[l]), batch, seq)
        kv = _mem_kv(mem2d, row(mem_norm_w[l]), xattn_w_kv[l].astype(BF16), mem_tokens)

        wo = w_out[l].astype(BF16)
        wr32 = _pad_cols(jnp.concatenate([router_expert_w[l], router_group_w[l]], axis=1), LANES)
        wr_hi = wr32.astype(BF16)
        wr = jnp.concatenate([wr_hi, (wr32 - wr_hi.astype(F32)).astype(BF16)], axis=1)
        br = _pad_cols(row(jnp.concatenate([router_expert_b[l], router_group_b[l]])), LANES)
        x2, payload, meta, counts = _post(
            x2d, attn, ssd, wo[:attn_w], wo[attn_w:], row(xattn_norm_w[l]), xattn_w_q[l].astype(BF16), kv,
            xattn_w_o[l].astype(BF16), row(ffn_norm_w[l]), wr, br, tm, seq, mem_tokens)

        row_start, tile_group, tile_a, tile_b, tile_blk, tile_valid = _tile_plan(
            counts[0].astype(jnp.int32), n_tiles)
        bucket = meta[2].astype(jnp.int32)
        hit = bucket[:, None] == jnp.arange(N_BUCKETS, dtype=jnp.int32)[None, :]
        dest = jnp.sum(jnp.where(hit, row_start[None, :], 0), axis=1) + meta[3].astype(jnp.int32)
        dest = dest * SLAB_PITCH
        xs = _dispatch(dest.reshape(t // dispatch_rows, 1, dispatch_rows), payload, n_sorted)
        wgu = jnp.concatenate([expert_w_gate[l], expert_w_up[l]], axis=2).astype(BF16)
        ys = _experts(tile_group, tile_a, tile_b, tile_blk, tile_valid, xs, wgu, expert_w_down[l].astype(BF16))
        assert depth == 1
        x2d = _combine(dest.reshape(t // combine_rows, 1, combine_rows), x2, row(final_norm_w), ys)

    return x2d.reshape(batch, seq, d)
```

```python
import functools

import numpy as np
import jax
import jax.numpy as jnp
from jax import lax
from jax.experimental import pallas as pl
from jax.experimental.pallas import tpu as pltpu

RMS_EPS = 1e-5
HEAD_DIM = 64
N_KV_HEADS = 2
WINDOW = 128
ATTN_BLOCK = 128
ROT_DIM = 16
ROT_HALF = ROT_DIM // 2
ROPE_THETA = 500000.0
SSD_HEAD_DIM = 64
SSD_GROUPS = 2
SSD_STATE = 128
SSD_CONV = 4
SSD_CHUNK = 128
XATTN_HEADS = 4
N_GROUPS = 4
EXPERTS_PER_GROUP = 8

LANES = 128
SUBLANES = 8
MOE_TILE = 256
SLAB_PITCH = SUBLANES + 1
N_BUCKETS = N_GROUPS * EXPERTS_PER_GROUP * EXPERTS_PER_GROUP
N_USED_BUCKETS = N_GROUPS * (EXPERTS_PER_GROUP * (EXPERTS_PER_GROUP - 1) // 2)
VMEM_LIMIT = 56 * 1024 * 1024

F32 = jnp.float32
BF16 = jnp.bfloat16


def _cparams(semantics):
    return pltpu.CompilerParams(dimension_semantics=semantics, vmem_limit_bytes=VMEM_LIMIT)


def _rms(x, w):
    return x * lax.rsqrt(jnp.mean(x * x, axis=-1, keepdims=True) + RMS_EPS) * w


def _dot(a, b):
    return jnp.dot(a, b, preferred_element_type=F32)


def _dot_nt(a, b):
    return lax.dot_general(a, b, (((1,), (1,)), ((), ())), preferred_element_type=F32)


def _in_proj_kernel(pos_ref, x_ref, nw_ref, w_ref, wdt_ref, place_ref, base_ref,
                    q_ref, k_ref, v_ref, z_ref, xbc_ref, dt_ref):
    x = x_ref[...]
    hb = _rms(x, nw_ref[...]).astype(BF16)

    j = lax.broadcasted_iota(jnp.int32, (ROT_HALF, 1), 0).astype(F32)
    inv_freq = jnp.power(jnp.float32(ROPE_THETA), -(2.0 * j) / ROT_DIM)
    ang = pos_ref[...].astype(F32) * inv_freq
    cs = jnp.concatenate([jnp.cos(ang), jnp.sin(ang)], axis=0)

    cs_hi = cs.astype(BF16)
    cs_lo = (cs - cs_hi.astype(F32)).astype(BF16)
    tn = (((0,), (0,)), ((), ()))
    tabs = (lax.dot_general(cs_hi, place_ref[...], tn, preferred_element_type=F32)
            + lax.dot_general(cs_lo, place_ref[...], tn, preferred_element_type=F32))
    c_tab = tabs[:, :LANES] + base_ref[...]
    s_tab = tabs[:, LANES:]
    first = (lax.broadcasted_iota(jnp.int32, s_tab.shape, 1) % HEAD_DIM) < ROT_HALF
    s_up = jnp.where(first, s_tab, 0.0)
    s_dn = jnp.where(first, 0.0, s_tab)

    def rope(t):
        n = t.shape[1]
        reps = n // LANES
        c = jnp.tile(c_tab, (1, reps)) if reps > 1 else c_tab
        su = jnp.tile(s_up, (1, reps)) if reps > 1 else s_up
        sd = jnp.tile(s_dn, (1, reps)) if reps > 1 else s_dn
        return t * c + pltpu.roll(t, n - ROT_HALF, 1) * su + pltpu.roll(t, ROT_HALF, 1) * sd

    proj = _dot(hb, w_ref[...])
    o = 0
    pieces = []
    for ref in (q_ref, k_ref, v_ref, z_ref, xbc_ref):
        pieces.append(proj[:, o:o + ref.shape[1]])
        o += ref.shape[1]
    q, k, v, z, xbc = pieces
    dt = _dot(hb, wdt_ref[...])
    q_ref[...] = (rope(q) * (HEAD_DIM ** -0.5)).astype(BF16)
    k_ref[...] = rope(k).astype(BF16)
    v_ref[...] = v.astype(BF16)
    z_ref[...] = z.astype(BF16)
    xbc_ref[...] = xbc
    dt_ref[...] = dt


def _rope_placement():
    d = np.arange(LANES) % HEAD_DIM
    jj = np.arange(ROT_HALF)[:, None]
    ec = ((d[None, :] < ROT_DIM) & ((d[None, :] % ROT_HALF) == jj)).astype(np.float32)
    eup = -((d[None, :] < ROT_HALF) & (d[None, :] == jj)).astype(np.float32)
    edn = ((d[None, :] >= ROT_HALF) & (d[None, :] < ROT_DIM) & ((d[None, :] - ROT_HALF) == jj)).astype(np.float32)
    zero = np.zeros_like(ec)
    place = np.concatenate([np.concatenate([ec, zero], axis=1),
                            np.concatenate([zero, eup + edn], axis=1)], axis=0)
    base = (d >= ROT_DIM).astype(np.float32)[None, :]
    return jnp.asarray(place, dtype=BF16), jnp.asarray(base)


def _in_proj(pos_row, x2d, norm_w, w_main, w_dt, widths, tm):
    t, d = x2d.shape
    place, base = _rope_placement()
    full = lambda a: pl.BlockSpec(a.shape, lambda i: (0,) * a.ndim)
    row = lambda n: pl.BlockSpec((tm, n), lambda i: (i, 0))
    dtypes = (BF16, BF16, BF16, BF16, F32, F32)
    return pl.pallas_call(
        _in_proj_kernel,
        grid=(t // tm,),
        in_specs=[pl.BlockSpec((1, tm), lambda i: (0, i)), row(d), full(norm_w), full(w_main), full(w_dt),
                  full(place), full(base)],
        out_specs=[row(n) for n in widths],
        out_shape=[jax.ShapeDtypeStruct((t, n), dt) for n, dt in zip(widths, dtypes)],
        compiler_params=_cparams(("parallel",)),
        name="in_proj",
    )(pos_row, x2d, norm_w, w_main, w_dt, place, base)


SWA_SUB = 4


def _swa_kernel(sink_ref, q_ref, kp_ref, kc_ref, vp_ref, vc_ref, nw_ref, o_ref,
                s_ref, p_ref, bias_ref, acc_ref):
    i = pl.program_id(1)
    blk = ATTN_BLOCK
    n_q_heads = q_ref.shape[1] // HEAD_DIM
    q_per_kv = n_q_heads // N_KV_HEADS
    qi = lax.broadcasted_iota(jnp.int32, (blk, 2 * blk), 0) + blk
    ki = lax.broadcasted_iota(jnp.int32, (blk, 2 * blk), 1)
    rel = qi - ki
    local = (rel >= 0) & (rel < WINDOW)
    bias_ref[0] = jnp.where(local & ((i > 0) | (ki >= blk)), 0.0, -jnp.inf)
    bias_ref[1] = jnp.where(local, 0.0, -jnp.inf)
    sink_col = ki == 0

    def band(prev_ref, cur_ref, j):
        if j == 0:
            return jnp.concatenate([prev_ref[...], cur_ref[0:blk, :]], axis=0)
        return cur_ref[(j - 1) * blk:(j + 1) * blk, :]

    for j in range(SWA_SUB):
        kband = band(kp_ref, kc_ref, j)
        for kv in range(N_KV_HEADS):
            kcat = kband[:, kv * HEAD_DIM:(kv + 1) * HEAD_DIM]
            for g in range(q_per_kv):
                h = kv * q_per_kv + g
                r0 = (j * n_q_heads + h) * blk
                s_ref[r0:r0 + blk, :] = _dot_nt(q_ref[j * blk:(j + 1) * blk, h * HEAD_DIM:(h + 1) * HEAD_DIM], kcat)

    for j in range(SWA_SUB):
        for h in range(n_q_heads):
            rows = slice((j * n_q_heads + h) * blk, (j * n_q_heads + h + 1) * blk)
            s = jnp.where(sink_col, sink_ref[h], s_ref[rows, :] + bias_ref[min(j, 1)])
            p_ref[rows, :] = jnp.exp(s - jnp.max(s, axis=-1, keepdims=True)).astype(BF16)

    ones = jnp.ones((2 * blk, LANES), BF16)
    low = lax.broadcasted_iota(jnp.int32, (blk, LANES), 1) < HEAD_DIM
    for j in range(SWA_SUB):
        vband = band(vp_ref, vc_ref, j)
        vband = jnp.where(lax.broadcasted_iota(jnp.int32, vband.shape, 0) == 0, jnp.zeros_like(vband), vband)
        for kv in range(N_KV_HEADS):
            v_kv = vband[:, kv * HEAD_DIM:(kv + 1) * HEAD_DIM]
            vaug = jnp.concatenate([v_kv, v_kv, ones], axis=1)
            for k in range(q_per_kv // 2):
                h = kv * q_per_kv + 2 * k
                r0 = (j * n_q_heads + h) * blk
                o_even = _dot(p_ref[r0:r0 + blk, :], vaug)
                o_odd = _dot(p_ref[r0 + blk:r0 + 2 * blk, :], vaug)
                num = jnp.where(low, o_even[:, :LANES], o_odd[:, :LANES])
                den = jnp.where(low, o_even[:, LANES:], o_odd[:, LANES:])
                acc_ref[j * blk:(j + 1) * blk, h * HEAD_DIM:(h + 2) * HEAD_DIM] = num * (1.0 / den)
    o_ref[...] = _rms(acc_ref[...], nw_ref[...]).astype(BF16)


def _swa(sinks, q, k, v, norm_w, batch, seq):
    t, qw = q.shape
    kw = k.shape[1]
    rows = SWA_SUB * ATTN_BLOCK
    nb = seq // rows
    n_heads = qw // HEAD_DIM
    cur = lambda b, i: (b * nb + i, 0)
    prev = lambda b, i: ((b * nb + i) * SWA_SUB - jnp.minimum(i, 1), 0)
    return pl.pallas_call(
        _swa_kernel,
        grid=(batch, nb),
        in_specs=[pl.BlockSpec(memory_space=pltpu.SMEM),
                  pl.BlockSpec((rows, qw), cur),
                  pl.BlockSpec((ATTN_BLOCK, kw), prev), pl.BlockSpec((rows, kw), cur),
                  pl.BlockSpec((ATTN_BLOCK, kw), prev), pl.BlockSpec((rows, kw), cur),
                  pl.BlockSpec((1, qw), lambda b, i: (0, 0))],
        out_specs=pl.BlockSpec((rows, qw), cur),
        out_shape=jax.ShapeDtypeStruct((t, qw), BF16),
        scratch_shapes=[pltpu.VMEM((SWA_SUB * n_heads * ATTN_BLOCK, 2 * ATTN_BLOCK), F32),
                        pltpu.VMEM((SWA_SUB * n_heads * ATTN_BLOCK, 2 * ATTN_BLOCK), BF16),
                        pltpu.VMEM((2, ATTN_BLOCK, 2 * ATTN_BLOCK), F32),
                        pltpu.VMEM((rows, qw), F32)],
        compiler_params=_cparams(("parallel", "parallel")),
        name="swa",
    )(sinks, q, k, k, v, v, norm_w)


SSD_TAIL = 16
SSD_BATCH = 4


def _split2(x):
    hi = x.astype(BF16)
    return hi, (x - hi.astype(F32)).astype(BF16)


def _ssd_kernel(xbc_ref, dt_ref, z_ref, cw_ref, cb_ref, dtb_ref, alog_ref, dskip_ref, nw_ref,
                tril_ref, shift_ref, expand_ref, o_ref, ext_ref, state_ref, y_ref):
    @pl.when(pl.program_id(1) == 0)
    def _():
        state_ref[...] = jnp.zeros_like(state_ref)
        ext_ref[:, 0:SSD_TAIL, :] = jnp.zeros((SSD_BATCH, SSD_TAIL, ext_ref.shape[2]), F32)

    for slot in range(SSD_BATCH):
        _ssd_chunk(xbc_ref.at[slot], dt_ref.at[slot], z_ref.at[slot], cw_ref, cb_ref, dtb_ref, alog_ref,
                   dskip_ref, nw_ref, tril_ref, shift_ref, expand_ref, o_ref.at[slot],
                   ext_ref.at[slot], state_ref.at[slot], y_ref.at[slot])


def _ssd_chunk(xbc_ref, dt_ref, z_ref, cw_ref, cb_ref, dtb_ref, alog_ref, dskip_ref, nw_ref,
               tril_ref, shift_ref, expand_ref, o_ref, ext_ref, state_ref, y_ref):
    L = SSD_CHUNK
    width = z_ref.shape[1]
    n_heads = width // SSD_HEAD_DIM
    heads_per_group = n_heads // SSD_GROUPS
    gw = width // SSD_GROUPS

    cur = xbc_ref[...]
    ext_ref[SSD_TAIL:SSD_TAIL + L, :] = cur
    shifted = _dot(shift_ref[...], ext_ref[...].astype(BF16))
    ext_ref[0:SSD_TAIL, :] = cur[L - SSD_TAIL:, :]
    acc = cb_ref[...] + cw_ref[SSD_CONV - 1:SSD_CONV, :] * cur
    for jj in range(SSD_CONV - 1):
        acc = acc + cw_ref[jj:jj + 1, :] * shifted[jj * L:(jj + 1) * L, :]
    u = jax.nn.silu(acc)
    xs = u[:, :width]
    bm = u[:, width:width + SSD_GROUPS * SSD_STATE]
    cm = u[:, width + SSD_GROUPS * SSD_STATE:]

    dt = jax.nn.softplus(dt_ref[...] + dtb_ref[...])
    da = dt * (-jnp.exp(alog_ref[...]))
    d1 = da.astype(BF16)
    r1 = da - d1.astype(F32)
    d2 = r1.astype(BF16)
    d3 = (r1 - d2.astype(F32)).astype(BF16)
    cum12 = _dot(tril_ref[...], jnp.concatenate([d1, d2], axis=1))
    cum = cum12[:, :LANES] + (cum12[:, LANES:] + _dot(tril_ref[...], d3))
    cum_t = cum.T
    cum_last = cum[L - 1:L, :]
    e_cum = jnp.exp(cum)
    w_end = dt * jnp.exp(cum_last - cum)

    s_hi, s_lo = _split2(jnp.concatenate([dt, e_cum, w_end], axis=0))
    spread = _dot(s_hi, expand_ref[...]) + _dot(s_lo, expand_ref[...])
    dt_x, ecum_x, wend_x = spread[:L], spread[L:2 * L], spread[2 * L:]
    xc = (xs * dt_x).astype(BF16)
    xw = (xs * wend_x).astype(BF16)
    skip = dskip_ref[...] * xs

    row = lax.broadcasted_iota(jnp.int32, (L, L), 0)
    col = lax.broadcasted_iota(jnp.int32, (L, L), 1)
    causal = row >= col
    first_half = lax.broadcasted_iota(jnp.int32, (L, LANES), 1) < SSD_HEAD_DIM

    for g in range(SSD_GROUPS):
        gs = slice(g * gw, (g + 1) * gw)
        bg = bm[:, g * SSD_STATE:(g + 1) * SSD_STATE]
        cg = cm[:, g * SSD_STATE:(g + 1) * SSD_STATE].astype(BF16)
        cb = _dot_nt(cg, bg.astype(BF16))
        st = state_ref[:, gs]
        y_off = _dot(cg, st.astype(BF16)) * ecum_x[:, gs]
        state_ref[:, gs] = st * ecum_x[L - 1:L, gs] + _dot(bg.T.astype(BF16), xw[:, gs])
        for k in range(heads_per_group // 2):
            ps = slice(g * gw + k * LANES, g * gw + (k + 1) * LANES)
            halves = []
            for sub in range(2):
                h = g * heads_per_group + 2 * k + sub
                seg = cum[:, h:h + 1] - cum_t[h:h + 1, :]
                decay = jnp.exp(jnp.where(causal, seg, -jnp.inf))
                halves.append(_dot((cb * decay).astype(BF16), xc[:, ps]))
            y_ref[:, ps] = (jnp.where(first_half, halves[0], halves[1])
                            + y_off[:, k * LANES:(k + 1) * LANES] + skip[:, ps])

    gated = y_ref[...] * jax.nn.silu(z_ref[...].astype(F32))
    parts = []
    for g in range(SSD_GROUPS):
        gg = gated[:, g * gw:(g + 1) * gw]
        parts.append(gg * lax.rsqrt(jnp.mean(gg * gg, axis=-1, keepdims=True) + RMS_EPS))
    o_ref[...] = (jnp.concatenate(parts, axis=1) * nw_ref[...]).astype(BF16)


def _ssd(xbc, dt_raw, z, conv_w, conv_b, dt_bias, a_log, d_skip, norm_w, batch, seq):
    t, cw = xbc.shape
    width = z.shape[1]
    nc = seq // SSD_CHUNK
    L = SSD_CHUNK
    tril = jnp.asarray(np.tril(np.ones((L, L), np.float32)), dtype=BF16)
    shift = np.zeros(((SSD_CONV - 1) * L, SSD_TAIL + L), np.float32)
    for jj in range(SSD_CONV - 1):
        shift[jj * L + np.arange(L), SSD_TAIL - (SSD_CONV - 1) + jj + np.arange(L)] = 1.0
    shift = jnp.asarray(shift, dtype=BF16)
    expand = np.zeros((LANES, width), np.float32)
    for h in range(width // SSD_HEAD_DIM):
        expand[h, h * SSD_HEAD_DIM:(h + 1) * SSD_HEAD_DIM] = 1.0
    expand = jnp.asarray(expand, dtype=BF16)
    cur = lambda b, c: (b, c, 0)
    full = lambda a: pl.BlockSpec(a.shape, lambda b, c: (0,) * a.ndim)
    per_seq = lambda a: a.reshape(batch, seq, a.shape[1])
    out = pl.pallas_call(
        _ssd_kernel,
        grid=(batch // SSD_BATCH, nc),
        in_specs=[pl.BlockSpec((SSD_BATCH, L, cw), cur), pl.BlockSpec((SSD_BATCH, L, LANES), cur),
                  pl.BlockSpec((SSD_BATCH, L, width), cur),
                  full(conv_w), full(conv_b), full(dt_bias), full(a_log), full(d_skip), full(norm_w),
                  full(tril), full(shift), full(expand)],
        out_specs=pl.BlockSpec((SSD_BATCH, L, width), cur),
        out_shape=jax.ShapeDtypeStruct((batch, seq, width), BF16),
        scratch_shapes=[pltpu.VMEM((SSD_BATCH, SSD_TAIL + L, cw), F32),
                        pltpu.VMEM((SSD_BATCH, SSD_STATE, width), F32),
                        pltpu.VMEM((SSD_BATCH, L, width), F32)],
        compiler_params=_cparams(("parallel", "arbitrary")),
        name="ssd",
    )(per_seq(xbc), per_seq(dt_raw), per_seq(z), conv_w, conv_b, dt_bias, a_log, d_skip, norm_w,
      tril, shift, expand)
    return out.reshape(t, width)


def _mem_kv_kernel(m_ref, nw_ref, w_ref, o_ref):
    o_ref[...] = _dot(_rms(m_ref[...], nw_ref[...]).astype(BF16), w_ref[...]).astype(BF16)


def _mem_kv(mem2d, norm_w, w_kv, rows):
    t, d = mem2d.shape
    n = w_kv.shape[1]
    return pl.pallas_call(
        _mem_kv_kernel,
        grid=(t // rows,),
        in_specs=[pl.BlockSpec((rows, d), lambda i: (i, 0)),
                  pl.BlockSpec((1, d), lambda i: (0, 0)),
                  pl.BlockSpec((d, n), lambda i: (0, 0))],
        out_specs=pl.BlockSpec((rows, n), lambda i: (i, 0)),
        out_shape=jax.ShapeDtypeStruct((t, n), BF16),
        compiler_params=_cparams(("parallel",)),
        name="mem_kv",
    )(mem2d, norm_w, w_kv)


POST_PIECES = 2


def _post_kernel(x_ref, a_ref, s_ref, woa_ref, wos_ref, xnw_ref, wq_ref, kv_ref, wo_ref,
                 fnw_ref, wr_ref, br_ref, tril_ref,
                 x2_ref, pay_ref, meta_ref, cnt_ref, carry_ref, h3_ref, logit_ref):
    step = pl.program_id(0)
    tm = x_ref.shape[0]
    n_exp = N_GROUPS * EXPERTS_PER_GROUP

    @pl.when(step == 0)
    def _():
        carry_ref[...] = jnp.zeros_like(carry_ref)
        h3_ref[...] = jnp.zeros_like(h3_ref)
        logit_ref[...] = jnp.zeros_like(logit_ref)

    h3_prev = h3_ref[...]
    logits = logit_ref[...]

    pm = tm // POST_PIECES
    pieces = [slice(p * pm, (p + 1) * pm) for p in range(POST_PIECES)]
    x1 = [x_ref[r, :] + _dot(a_ref[r, :], woa_ref[...]) + _dot(s_ref[r, :], wos_ref[...]) for r in pieces]

    xw = wq_ref.shape[1]
    hd = xw // XATTN_HEADS
    q = [(_dot(_rms(v, xnw_ref[...]).astype(BF16), wq_ref[...]) * (hd ** -0.5)).astype(BF16) for v in x1]
    attn = []
    for qp in q:
        heads = []
        for h in range(XATTN_HEADS):
            s = _dot_nt(qp[:, h * hd:(h + 1) * hd], kv_ref[:, h * hd:(h + 1) * hd])
            p = jnp.exp(s - jnp.max(s, axis=-1, keepdims=True))
            o = _dot(p.astype(BF16), kv_ref[:, xw + h * hd:xw + (h + 1) * hd])
            heads.append((o / jnp.sum(p, axis=-1, keepdims=True)).astype(BF16))
        attn.append(jnp.concatenate(heads, axis=1))
    x2 = [v + _dot(at, wo_ref[...]) for v, at in zip(x1, attn)]

    for r, v in zip(pieces, x2):
        x2_ref[r, :] = v
        h3 = _rms(v, fnw_ref[...])
        h_hi = h3.astype(BF16)
        h_lo = (h3 - h_hi.astype(F32)).astype(BF16)
        t_hi = _dot(h_hi, wr_ref[...])
        h3_ref[r, :] = h3
        logit_ref[r, :] = t_hi[:, :LANES] + (t_hi[:, LANES:] + _dot(h_lo, wr_ref[:, :LANES])) + br_ref[...]

    lane = lax.broadcasted_iota(jnp.int32, (tm, LANES), 1)
    big = jnp.int32(LANES)
    neg = -jnp.inf
    g_l = jnp.where((lane >= n_exp) & (lane < n_exp + N_GROUPS), logits, neg)
    g_max = jnp.max(g_l, axis=-1, keepdims=True)
    g_idx = jnp.min(jnp.where(g_l == g_max, lane - n_exp, big), axis=-1, keepdims=True)
    g_gate = 1.0 / jnp.sum(jnp.exp(g_l - g_max), axis=-1, keepdims=True)
    e_l = jnp.where((lane < n_exp) & ((lane // EXPERTS_PER_GROUP) == g_idx), logits, neg)
    m1 = jnp.max(e_l, axis=-1, keepdims=True)
    i1 = jnp.min(jnp.where(e_l == m1, lane, big), axis=-1, keepdims=True)
    e_l2 = jnp.where(lane == i1, neg, e_l)
    m2 = jnp.max(e_l2, axis=-1, keepdims=True)
    i2 = jnp.min(jnp.where(e_l2 == m2, lane, big), axis=-1, keepdims=True)
    e2 = jnp.exp(m2 - m1)
    w1 = (1.0 / (1.0 + e2)) * g_gate
    w2 = (e2 / (1.0 + e2)) * g_gate
    first_low = i1 < i2
    lo = jnp.where(first_low, i1, i2) % EXPERTS_PER_GROUP
    hi = jnp.where(first_low, i2, i1) % EXPERTS_PER_GROUP
    w_lo = jnp.where(first_low, w1, w2)
    w_hi = jnp.where(first_low, w2, w1)
    bucket = g_idx * (EXPERTS_PER_GROUP * EXPERTS_PER_GROUP) + lo * EXPERTS_PER_GROUP + hi

    blane = lax.broadcasted_iota(jnp.int32, (tm, N_BUCKETS), 1)
    onehot = (blane == bucket).astype(F32)
    before = _dot(tril_ref[...], onehot.astype(BF16)) + carry_ref[...]
    rank = jnp.sum(onehot * before, axis=-1, keepdims=True)
    carry_ref[...] = carry_ref[...] + jnp.where(step > 0, jnp.sum(onehot, axis=0, keepdims=True), 0.0)
    cnt_ref[...] = carry_ref[...]

    meta = jnp.where(lane == 0, w_lo, jnp.where(lane == 1, w_hi, jnp.where(
        lane == 2, bucket.astype(F32), jnp.where(lane == 3, rank, 0.0))))
    n_feat_rows = h3_prev.shape[1] // LANES
    for c in range(n_feat_rows):
        pay_ref[pl.ds(c, tm, stride=SLAB_PITCH), :] = h3_prev[:, c * LANES:(c + 1) * LANES]
    pay_ref[pl.ds(n_feat_rows, tm, stride=SLAB_PITCH), :] = meta
    meta_ref[...] = meta.T[:SUBLANES, :]


def _post(x2d, attn, ssd, woa, wos, xnw, wq, kv, wo, fnw, wr, br, tm, seq, mem_tokens):
    t, d = x2d.shape
    n = t // tm
    tiles_per_batch = seq // tm
    tril = jnp.asarray(np.tril(np.ones((tm, tm), np.float32), -1), dtype=BF16)
    full = lambda a: pl.BlockSpec(a.shape, lambda i: (0,) * a.ndim)
    this = lambda i: jnp.minimum(i, n - 1)
    prev = lambda i: jnp.maximum(i - 1, 0)
    row = lambda w: pl.BlockSpec((tm, w), lambda i: (this(i), 0))
    return pl.pallas_call(
        _post_kernel,
        grid=(n + 1,),
        in_specs=[row(d), row(attn.shape[1]), row(ssd.shape[1]), full(woa), full(wos), full(xnw), full(wq),
                  pl.BlockSpec((mem_tokens, kv.shape[1]), lambda i: (this(i) // tiles_per_batch, 0)),
                  full(wo), full(fnw), full(wr), full(br), full(tril)],
        out_specs=[row(d), pl.BlockSpec((tm * SLAB_PITCH, LANES), lambda i: (prev(i), 0)),
                   pl.BlockSpec((SUBLANES, tm), lambda i: (0, prev(i))),
                   pl.BlockSpec((1, N_BUCKETS), lambda i: (0, 0))],
        out_shape=[jax.ShapeDtypeStruct((t, d), F32), jax.ShapeDtypeStruct((t * SLAB_PITCH, LANES), F32),
                   jax.ShapeDtypeStruct((SUBLANES, t), F32), jax.ShapeDtypeStruct((1, N_BUCKETS), F32)],
        scratch_shapes=[pltpu.VMEM((1, N_BUCKETS), F32), pltpu.VMEM((tm, d), F32), pltpu.VMEM((tm, LANES), F32)],
        compiler_params=_cparams(("arbitrary",)),
        name="post",
    )(x2d, attn, ssd, woa, wos, xnw, wq, kv, wo, fnw, wr, br, tril)


DMA_UNROLL = 8
DMA_PRIORITIES = 2


def _dispatch_kernel(dest_ref, pay_ref, xs_hbm, sem):
    rows = dest_ref.shape[-1]

    def copy(r):
        return pltpu.make_async_copy(pay_ref.at[pl.ds(r * SLAB_PITCH, SLAB_PITCH)],
                                     xs_hbm.at[pl.ds(dest_ref[0, r], SLAB_PITCH)], sem)

    def issue(g, carry):
        for u in range(DMA_UNROLL):
            copy(g * DMA_UNROLL + u).start(priority=u % DMA_PRIORITIES)
        return carry

    lax.fori_loop(0, rows // DMA_UNROLL, issue, 0)

    def drain(g, carry):
        for u in range(DMA_UNROLL):
            copy(g * DMA_UNROLL + u).wait()
        return carry

    lax.fori_loop(0, rows // DMA_UNROLL, drain, 0)


def _dispatch(dest3d, payload, n_sorted):
    n_steps, _, rows = dest3d.shape
    return pl.pallas_call(
        _dispatch_kernel,
        grid=(n_steps,),
        in_specs=[pl.BlockSpec((None, 1, rows), lambda i: (i, 0, 0), memory_space=pltpu.SMEM),
                  pl.BlockSpec((rows * SLAB_PITCH, LANES), lambda i: (i, 0))],
        out_specs=pl.BlockSpec(memory_space=pl.ANY),
        out_shape=jax.ShapeDtypeStruct((n_sorted * SLAB_PITCH, LANES), payload.dtype),
        scratch_shapes=[pltpu.SemaphoreType.DMA(())],
        compiler_params=_cparams(("arbitrary",)),
        name="dispatch",
    )(dest3d, payload)


def _experts_kernel(grp_ref, ea_ref, eb_ref, blk_ref, valid_ref, xs_ref, wgu_ref, wd_ref, y_ref):
    i = pl.program_id(0)
    ff = wd_ref.shape[1]
    d = wd_ref.shape[2]
    n_feat_rows = d // LANES
    token_row = lambda c: pl.ds(c, MOE_TILE, stride=SLAB_PITCH)

    @pl.when(valid_ref[i] == 1)
    def _():
        x = jnp.concatenate([xs_ref[token_row(c), :].astype(BF16) for c in range(n_feat_rows)], axis=1)
        gates = xs_ref[token_row(n_feat_rows), :]
        experts = (ea_ref[i], eb_ref[i])
        gu = [_dot(x, wgu_ref[e]) for e in experts]
        hid = [(jax.nn.silu(v[:, :ff]) * v[:, ff:]).astype(BF16) for v in gu]
        out = [_dot(hv, wd_ref[e]) for hv, e in zip(hid, experts)]
        y = gates[:, 0:1] * out[0] + gates[:, 1:2] * out[1]
        for c in range(n_feat_rows):
            y_ref[token_row(c), :] = y[:, c * LANES:(c + 1) * LANES]
        y_ref[token_row(n_feat_rows), :] = jnp.zeros((MOE_TILE, LANES), F32)


def _experts(tile_group, tile_a, tile_b, tile_blk, tile_valid, xs, wgu, wd):
    n_sorted = xs.shape[0] // SLAB_PITCH
    d = wd.shape[2]
    nt = n_sorted // MOE_TILE
    slab = lambda: pl.BlockSpec((MOE_TILE * SLAB_PITCH, LANES), lambda i, g, a, b, k, v: (k[i], 0))
    grid_spec = pltpu.PrefetchScalarGridSpec(
        num_scalar_prefetch=5,
        grid=(nt,),
        in_specs=[slab(),
                  pl.BlockSpec((EXPERTS_PER_GROUP,) + wgu.shape[1:], lambda i, g, a, b, k, v: (g[i], 0, 0)),
                  pl.BlockSpec((EXPERTS_PER_GROUP,) + wd.shape[1:], lambda i, g, a, b, k, v: (g[i], 0, 0))],
        out_specs=slab(),
    )
    return pl.pallas_call(
        _experts_kernel,
        grid_spec=grid_spec,
        out_shape=jax.ShapeDtypeStruct(xs.shape, F32),
        compiler_params=_cparams(("arbitrary",)),
        name="experts",
    )(tile_group, tile_a, tile_b, tile_blk, tile_valid, xs, wgu, wd)


COMBINE_ROW_BLOCK = 64
COMBINE_UNROLL = 4


def _combine_kernel(dcur_ref, dnext_ref, x2_ref, nw_ref, ys_hbm, o_ref, buf_ref, sem):
    i = pl.program_id(0)
    n = pl.num_programs(0)
    rows, d = x2_ref.shape
    n_feat_rows = d // LANES
    slot = i % 2

    def gather(dref, s, start):
        def body(g, carry):
            for u in range(DMA_UNROLL):
                r = g * DMA_UNROLL + u
                cp = pltpu.make_async_copy(ys_hbm.at[pl.ds(dref[0, r], n_feat_rows)],
                                           buf_ref.at[s, pl.ds(r * SLAB_PITCH, n_feat_rows)], sem.at[s])
                if start:
                    cp.start(priority=u % DMA_PRIORITIES)
                else:
                    cp.wait()
            return carry

        lax.fori_loop(0, rows // DMA_UNROLL, body, 0)

    @pl.when(i == 0)
    def _():
        gather(dcur_ref, slot, True)

    @pl.when(i + 1 < n)
    def _():
        gather(dnext_ref, 1 - slot, True)

    gather(dcur_ref, slot, False)

    def block(b, carry):
        r0 = pl.multiple_of(b * COMBINE_ROW_BLOCK, COMBINE_ROW_BLOCK)
        rs = pl.ds(r0, COMBINE_ROW_BLOCK)
        chunks = [x2_ref[rs, c * LANES:(c + 1) * LANES]
                  + buf_ref[slot, pl.ds(r0 * SLAB_PITCH + c, COMBINE_ROW_BLOCK, stride=SLAB_PITCH), :]
                  for c in range(n_feat_rows)]
        ssq = chunks[0] * chunks[0]
        for ch in chunks[1:]:
            ssq = ssq + ch * ch
        scale = lax.rsqrt(jnp.sum(ssq, axis=-1, keepdims=True) / d + RMS_EPS)
        for c, ch in enumerate(chunks):
            o_ref[rs, c * LANES:(c + 1) * LANES] = ch * scale * nw_ref[:, c * LANES:(c + 1) * LANES]
        return carry

    lax.fori_loop(0, rows // COMBINE_ROW_BLOCK, block, 0, unroll=COMBINE_UNROLL)


def _combine(dest3d, x2, norm_w, ys):
    n_steps, _, rows = dest3d.shape
    t, d = x2.shape
    return pl.pallas_call(
        _combine_kernel,
        grid=(n_steps,),
        in_specs=[pl.BlockSpec((None, 1, rows), lambda i: (i, 0, 0), memory_space=pltpu.SMEM),
                  pl.BlockSpec((None, 1, rows), lambda i: (jnp.minimum(i + 1, n_steps - 1), 0, 0),
                               memory_space=pltpu.SMEM),
                  pl.BlockSpec((rows, d), lambda i: (i, 0)),
                  pl.BlockSpec((1, d), lambda i: (0, 0)),
                  pl.BlockSpec(memory_space=pl.ANY)],
        out_specs=pl.BlockSpec((rows, d), lambda i: (i, 0)),
        out_shape=jax.ShapeDtypeStruct((t, d), F32),
        scratch_shapes=[pltpu.VMEM((2, rows * SLAB_PITCH, LANES), F32), pltpu.SemaphoreType.DMA((2,))],
        compiler_params=_cparams(("arbitrary",)),
        name="combine",
    )(dest3d, dest3d, x2, norm_w, ys)


def _pad_cols(w, n):
    return jnp.pad(w, ((0, 0), (0, n - w.shape[1])))


def _tile_plan(counts, n_tiles):
    per_bucket = (counts + (MOE_TILE - 1)) // MOE_TILE
    tile_end = jnp.cumsum(per_bucket)
    tile_start = tile_end - per_bucket
    total = tile_end[-1]
    ids = jnp.arange(n_tiles, dtype=jnp.int32)
    blk = jnp.minimum(ids, total - 1)
    bucket = jnp.sum((tile_end[None, :] <= blk[:, None]).astype(jnp.int32), axis=1)
    pair = EXPERTS_PER_GROUP * EXPERTS_PER_GROUP
    group = bucket // pair
    slot_a = (bucket % pair) // EXPERTS_PER_GROUP
    slot_b = bucket % EXPERTS_PER_GROUP
    valid = (ids < total).astype(jnp.int32)
    return tile_start * MOE_TILE, group, slot_a, slot_b, blk, valid


def kernel(x, mem, positions, mix_norm_w, w_in, attn_sinks, ssd_conv_w, ssd_conv_b, ssd_dt_bias, ssd_a_log, ssd_d, attn_out_norm_w, ssd_out_norm_w, w_out, xattn_norm_w, mem_norm_w, xattn_w_q, xattn_w_kv, xattn_w_o, ffn_norm_w, router_group_w, router_group_b, router_expert_w, router_expert_b, expert_w_gate, expert_w_up, expert_w_down, final_norm_w):
    batch, seq, d = x.shape
    mem_tokens = mem.shape[1]
    depth = w_in.shape[0]
    t = batch * seq
    attn_w = attn_out_norm_w.shape[1]
    ssd_w = ssd_out_norm_w.shape[1]
    kv_w = N_KV_HEADS * HEAD_DIM
    conv_dim = ssd_conv_w.shape[2]
    n_exp = router_expert_w.shape[2]
    assert n_exp == N_GROUPS * EXPERTS_PER_GROUP and router_group_w.shape[2] == N_GROUPS
    assert seq % (SWA_SUB * ATTN_BLOCK) == 0 and seq % SSD_CHUNK == 0 and t % MOE_TILE == 0
    assert batch % SSD_BATCH == 0
    tm = min(512, seq)
    dispatch_rows = min(2048, t)
    combine_rows = min(1024, t)
    n_tiles = t // MOE_TILE + N_USED_BUCKETS
    n_sorted = n_tiles * MOE_TILE

    pos_row = positions.reshape(1, t)
    x2d = x.reshape(t, d)
    mem2d = mem.reshape(batch * mem_tokens, d)
    row = lambda v: v.reshape(1, -1)

    for l in range(depth):
        widths = (attn_w, kv_w, kv_w, ssd_w, conv_dim, LANES)
        n_main = sum(widths[:-1])
        q, k, v, z, xbc, dt_raw = _in_proj(pos_row, x2d, row(mix_norm_w[l]), w_in[l][:, :n_main].astype(BF16),
                                           _pad_cols(w_in[l][:, n_main:], LANES).astype(BF16), widths, tm)
        attn = _swa(attn_sinks[l], q, k, v, row(attn_out_norm_w[l]), batch, seq)
        ssd = _ssd(xbc, dt_raw, z, ssd_conv_w[l], row(ssd_conv_b[l]),
                   _pad_cols(row(ssd_dt_bias[l]), LANES), _pad_cols(row(ssd_a_log[l]), LANES),
                   row(jnp.repeat(ssd_d[l], SSD_HEAD_DIM)), row(ssd_out_norm_w[l]), batch, seq)
        kv = _mem_kv(mem2d, row(mem_norm_w[l]), xattn_w_kv[l].astype(BF16), mem_tokens)

        wo = w_out[l].astype(BF16)
        wr32 = _pad_cols(jnp.concatenate([router_expert_w[l], router_group_w[l]], axis=1), LANES)
        wr_hi = wr32.astype(BF16)
        wr = jnp.concatenate([wr_hi, (wr32 - wr_hi.astype(F32)).astype(BF16)], axis=1)
        br = _pad_cols(row(jnp.concatenate([router_expert_b[l], router_group_b[l]])), LANES)
        x2, payload, meta, counts = _post(
            x2d, attn, ssd, wo[:attn_w], wo[attn_w:], row(xattn_norm_w[l]), xattn_w_q[l].astype(BF16), kv,
            xattn_w_o[l].astype(BF16), row(ffn_norm_w[l]), wr, br, tm, seq, mem_tokens)

        row_start, tile_group, tile_a, tile_b, tile_blk, tile_valid = _tile_plan(
            counts[0].astype(jnp.int32), n_tiles)
        bucket = meta[2].astype(jnp.int32)
        hit = bucket[:, None] == jnp.arange(N_BUCKETS, dtype=jnp.int32)[None, :]
        dest = jnp.sum(jnp.where(hit, row_start[None, :], 0), axis=1) + meta[3].astype(jnp.int32)
        dest = dest * SLAB_PITCH
        xs = _dispatch(dest.reshape(t // dispatch_rows, 1, dispatch_rows), payload, n_sorted)
        wgu = jnp.concatenate([expert_w_gate[l], expert_w_up[l]], axis=2).astype(BF16)
        ys = _experts(tile_group, tile_a, tile_b, tile_blk, tile_valid, xs, wgu, expert_w_down[l].astype(BF16))
        assert depth == 1
        x2d = _combine(dest.reshape(t // combine_rows, 1, combine_rows), x2, row(final_norm_w), ys)

    return x2d.reshape(batch, seq, d)
```

```python
import functools

import numpy as np
import jax
import jax.numpy as jnp
from jax import lax
from jax.experimental import pallas as pl
from jax.experimental.pallas import tpu as pltpu

RMS_EPS = 1e-5
HEAD_DIM = 64
N_KV_HEADS = 2
WINDOW = 128
ATTN_BLOCK = 128
ROT_DIM = 16
ROT_HALF = ROT_DIM // 2
ROPE_THETA = 500000.0
SSD_HEAD_DIM = 64
SSD_GROUPS = 2
SSD_STATE = 128
SSD_CONV = 4
SSD_CHUNK = 128
XATTN_HEADS = 4
N_GROUPS = 4
EXPERTS_PER_GROUP = 8

LANES = 128
SUBLANES = 8
MOE_TILE = 256
SLAB_PITCH = SUBLANES + 1
N_BUCKETS = N_GROUPS * EXPERTS_PER_GROUP * EXPERTS_PER_GROUP
N_USED_BUCKETS = N_GROUPS * (EXPERTS_PER_GROUP * (EXPERTS_PER_GROUP - 1) // 2)
VMEM_LIMIT = 56 * 1024 * 1024

F32 = jnp.float32
BF16 = jnp.bfloat16


def _cparams(semantics):
    return pltpu.CompilerParams(dimension_semantics=semantics, vmem_limit_bytes=VMEM_LIMIT)


def _rms(x, w):
    return x * lax.rsqrt(jnp.mean(x * x, axis=-1, keepdims=True) + RMS_EPS) * w


def _dot(a, b):
    return jnp.dot(a, b, preferred_element_type=F32)


def _dot_nt(a, b):
    return lax.dot_general(a, b, (((1,), (1,)), ((), ())), preferred_element_type=F32)


def _in_proj_kernel(pos_ref, x_ref, nw_ref, w_ref, wdt_ref, place_ref, base_ref,
                    q_ref, k_ref, v_ref, z_ref, xbc_ref, dt_ref):
    x = x_ref[...]
    hb = _rms(x, nw_ref[...]).astype(BF16)

    j = lax.broadcasted_iota(jnp.int32, (ROT_HALF, 1), 0).astype(F32)
    inv_freq = jnp.power(jnp.float32(ROPE_THETA), -(2.0 * j) / ROT_DIM)
    ang = pos_ref[...].astype(F32) * inv_freq
    cs = jnp.concatenate([jnp.cos(ang), jnp.sin(ang)], axis=0)

    cs_hi = cs.astype(BF16)
    cs_lo = (cs - cs_hi.astype(F32)).astype(BF16)
    tn = (((0,), (0,)), ((), ()))
    tabs = (lax.dot_general(cs_hi, place_ref[...], tn, preferred_element_type=F32)
            + lax.dot_general(cs_lo, place_ref[...], tn, preferred_element_type=F32))
    c_tab = tabs[:, :LANES] + base_ref[...]
    s_tab = tabs[:, LANES:]
    first = (lax.broadcasted_iota(jnp.int32, s_tab.shape, 1) % HEAD_DIM) < ROT_HALF
    s_up = jnp.where(first, s_tab, 0.0)
    s_dn = jnp.where(first, 0.0, s_tab)

    def rope(t):
        n = t.shape[1]
        reps = n // LANES
        c = jnp.tile(c_tab, (1, reps)) if reps > 1 else c_tab
        su = jnp.tile(s_up, (1, reps)) if reps > 1 else s_up
        sd = jnp.tile(s_dn, (1, reps)) if reps > 1 else s_dn
        return t * c + pltpu.roll(t, n - ROT_HALF, 1) * su + pltpu.roll(t, ROT_HALF, 1) * sd

    proj = _dot(hb, w_ref[...])
    o = 0
    pieces = []
    for ref in (q_ref, k_ref, v_ref, z_ref, xbc_ref):
        pieces.append(proj[:, o:o + ref.shape[1]])
        o += ref.shape[1]
    q, k, v, z, xbc = pieces
    dt = _dot(hb, wdt_ref[...])
    q_ref[...] = (rope(q) * (HEAD_DIM ** -0.5)).astype(BF16)
    k_ref[...] = rope(k).astype(BF16)
    v_ref[...] = v.astype(BF16)
    z_ref[...] = z.astype(BF16)
    xbc_ref[...] = xbc
    dt_ref[...] = dt


def _rope_placement():
    d = np.arange(LANES) % HEAD_DIM
    jj = np.arange(ROT_HALF)[:, None]
    ec = ((d[None, :] < ROT_DIM) & ((d[None, :] % ROT_HALF) == jj)).astype(np.float32)
    eup = -((d[None, :] < ROT_HALF) & (d[None, :] == jj)).astype(np.float32)
    edn = ((d[None, :] >= ROT_HALF) & (d[None, :] < ROT_DIM) & ((d[None, :] - ROT_HALF) == jj)).astype(np.float32)
    zero = np.zeros_like(ec)
    place = np.concatenate([np.concatenate([ec, zero], axis=1),
                            np.concatenate([zero, eup + edn], axis=1)], axis=0)
    base = (d >= ROT_DIM).astype(np.float32)[None, :]
    return jnp.asarray(place, dtype=BF16), jnp.asarray(base)


def _in_proj(pos_row, x2d, norm_w, w_main, w_dt, widths, tm):
    t, d = x2d.shape
    place, base = _rope_placement()
    full = lambda a: pl.BlockSpec(a.shape, lambda i: (0,) * a.ndim)
    row = lambda n: pl.BlockSpec((tm, n), lambda i: (i, 0))
    dtypes = (BF16, BF16, BF16, BF16, F32, F32)
    return pl.pallas_call(
        _in_proj_kernel,
        grid=(t // tm,),
        in_specs=[pl.BlockSpec((1, tm), lambda i: (0, i)), row(d), full(norm_w), full(w_main), full(w_dt),
                  full(place), full(base)],
        out_specs=[row(n) for n in widths],
        out_shape=[jax.ShapeDtypeStruct((t, n), dt) for n, dt in zip(widths, dtypes)],
        compiler_params=_cparams(("parallel",)),
        name="in_proj",
    )(pos_row, x2d, norm_w, w_main, w_dt, place, base)


SWA_SUB = 4


def _swa_kernel(sink_ref, q_ref, kp_ref, kc_ref, vp_ref, vc_ref, nw_ref, o_ref,
                s_ref, p_ref, bias_ref, acc_ref):
    i = pl.program_id(1)
    blk = ATTN_BLOCK
    n_q_heads = q_ref.shape[1] // HEAD_DIM
    q_per_kv = n_q_heads // N_KV_HEADS
    qi = lax.broadcasted_iota(jnp.int32, (blk, 2 * blk), 0) + blk
    ki = lax.broadcasted_iota(jnp.int32, (blk, 2 * blk), 1)
    rel = qi - ki
    local = (rel >= 0) & (rel < WINDOW)
    bias_ref[0] = jnp.where(local & ((i > 0) | (ki >= blk)), 0.0, -jnp.inf)
    bias_ref[1] = jnp.where(local, 0.0, -jnp.inf)
    sink_col = ki == 0

    def band(prev_ref, cur_ref, j):
        if j == 0:
            return jnp.concatenate([prev_ref[...], cur_ref[0:blk, :]], axis=0)
        return cur_ref[(j - 1) * blk:(j + 1) * blk, :]

    for j in range(SWA_SUB):
        kband = band(kp_ref, kc_ref, j)
        for kv in range(N_KV_HEADS):
            kcat = kband[:, kv * HEAD_DIM:(kv + 1) * HEAD_DIM]
            for g in range(q_per_kv):
                h = kv * q_per_kv + g
                r0 = (j * n_q_heads + h) * blk
                s_ref[r0:r0 + blk, :] = _dot_nt(q_ref[j * blk:(j + 1) * blk, h * HEAD_DIM:(h + 1) * HEAD_DIM], kcat)

    for j in range(SWA_SUB):
        for h in range(n_q_heads):
            rows = slice((j * n_q_heads + h) * blk, (j * n_q_heads + h + 1) * blk)
            s = jnp.where(sink_col, sink_ref[h], s_ref[rows, :] + bias_ref[min(j, 1)])
            p_ref[rows, :] = jnp.exp(s - jnp.max(s, axis=-1, keepdims=True)).astype(BF16)

    ones = jnp.ones((2 * blk, LANES), BF16)
    low = lax.broadcasted_iota(jnp.int32, (blk, LANES), 1) < HEAD_DIM
    for j in range(SWA_SUB):
        vband = band(vp_ref, vc_ref, j)
        vband = jnp.where(lax.broadcasted_iota(jnp.int32, vband.shape, 0) == 0, jnp.zeros_like(vband), vband)
        for kv in range(N_KV_HEADS):
            v_kv = vband[:, kv * HEAD_DIM:(kv + 1) * HEAD_DIM]
            vaug = jnp.concatenate([v_kv, v_kv, ones], axis=1)
            for k in range(q_per_kv // 2):
                h = kv * q_per_kv + 2 * k
                r0 = (j * n_q_heads + h) * blk
                o_even = _dot(p_ref[r0:r0 + blk, :], vaug)
                o_odd = _dot(p_ref[r0 + blk:r0 + 2 * blk, :], vaug)
                num = jnp.where(low, o_even[:, :LANES], o_odd[:, :LANES])
                den = jnp.where(low, o_even[:, LANES:], o_odd[:, LANES:])
                acc_ref[j * blk:(j + 1) * blk, h * HEAD_DIM:(h + 2) * HEAD_DIM] = num * (1.0 / den)
    o_ref[...] = _rms(acc_ref[...], nw_ref[...]).astype(BF16)


def _swa(sinks, q, k, v, norm_w, batch, seq):
    t, qw = q.shape
    kw = k.shape[1]
    rows = SWA_SUB * ATTN_BLOCK
    nb = seq // rows
    n_heads = qw // HEAD_DIM
    cur = lambda b, i: (b * nb + i, 0)
    prev = lambda b, i: ((b * nb + i) * SWA_SUB - jnp.minimum(i, 1), 0)
    return pl.pallas_call(
        _swa_kernel,
        grid=(batch, nb),
        in_specs=[pl.BlockSpec(memory_space=pltpu.SMEM),
                  pl.BlockSpec((rows, qw), cur),
                  pl.BlockSpec((ATTN_BLOCK, kw), prev), pl.BlockSpec((rows, kw), cur),
                  pl.BlockSpec((ATTN_BLOCK, kw), prev), pl.BlockSpec((rows, kw), cur),
                  pl.BlockSpec((1, qw), lambda b, i: (0, 0))],
        out_specs=pl.BlockSpec((rows, qw), cur),
        out_shape=jax.ShapeDtypeStruct((t, qw), BF16),
        scratch_shapes=[pltpu.VMEM((SWA_SUB * n_heads * ATTN_BLOCK, 2 * ATTN_BLOCK), F32),
                        pltpu.VMEM((SWA_SUB * n_heads * ATTN_BLOCK, 2 * ATTN_BLOCK), BF16),
                        pltpu.VMEM((2, ATTN_BLOCK, 2 * ATTN_BLOCK), F32),
                        pltpu.VMEM((rows, qw), F32)],
        compiler_params=_cparams(("parallel", "parallel")),
        name="swa",
    )(sinks, q, k, k, v, v, norm_w)


SSD_TAIL = 16
SSD_BATCH = 4


def _split2(x):
    hi = x.astype(BF16)
    return hi, (x - hi.astype(F32)).astype(BF16)


def _ssd_kernel(xbc_ref, dt_ref, z_ref, cw_ref, cb_ref, dtb_ref, alog_ref, dskip_ref, nw_ref,
                tril_ref, shift_ref, expand_ref, o_ref, ext_ref, state_ref, y_ref):
    @pl.when(pl.program_id(1) == 0)
    def _():
        state_ref[...] = jnp.zeros_like(state_ref)
        ext_ref[:, 0:SSD_TAIL, :] = jnp.zeros((SSD_BATCH, SSD_TAIL, ext_ref.shape[2]), F32)

    for slot in range(SSD_BATCH):
        _ssd_chunk(xbc_ref.at[slot], dt_ref.at[slot], z_ref.at[slot], cw_ref, cb_ref, dtb_ref, alog_ref,
                   dskip_ref, nw_ref, tril_ref, shift_ref, expand_ref, o_ref.at[slot],
                   ext_ref.at[slot], state_ref.at[slot], y_ref.at[slot])


def _ssd_chunk(xbc_ref, dt_ref, z_ref, cw_ref, cb_ref, dtb_ref, alog_ref, dskip_ref, nw_ref,
               tril_ref, shift_ref, expand_ref, o_ref, ext_ref, state_ref, y_ref):
    L = SSD_CHUNK
    width = z_ref.shape[1]
    n_heads = width // SSD_HEAD_DIM
    heads_per_group = n_heads // SSD_GROUPS
    gw = width // SSD_GROUPS

    cur = xbc_ref[...]
    ext_ref[SSD_TAIL:SSD_TAIL + L, :] = cur
    shifted = _dot(shift_ref[...], ext_ref[...].astype(BF16))
    ext_ref[0:SSD_TAIL, :] = cur[L - SSD_TAIL:, :]
    acc = cb_ref[...] + cw_ref[SSD_CONV - 1:SSD_CONV, :] * cur
    for jj in range(SSD_CONV - 1):
        acc = acc + cw_ref[jj:jj + 1, :] * shifted[jj * L:(jj + 1) * L, :]
    u = jax.nn.silu(acc)
    xs = u[:, :width]
    bm = u[:, width:width + SSD_GROUPS * SSD_STATE]
    cm = u[:, width + SSD_GROUPS * SSD_STATE:]

    dt = jax.nn.softplus(dt_ref[...] + dtb_ref[...])
    da = dt * (-jnp.exp(alog_ref[...]))
    d1 = da.astype(BF16)
    r1 = da - d1.astype(F32)
    d2 = r1.astype(BF16)
    d3 = (r1 - d2.astype(F32)).astype(BF16)
    cum12 = _dot(tril_ref[...], jnp.concatenate([d1, d2], axis=1))
    cum = cum12[:, :LANES] + (cum12[:, LANES:] + _dot(tril_ref[...], d3))
    cum_t = cum.T
    cum_last = cum[L - 1:L, :]
    e_cum = jnp.exp(cum)
    w_end = dt * jnp.exp(cum_last - cum)

    s_hi, s_lo = _split2(jnp.concatenate([dt, e_cum, w_end], axis=0))
    spread = _dot(s_hi, expand_ref[...]) + _dot(s_lo, expand_ref[...])
    dt_x, ecum_x, wend_x = spread[:L], spread[L:2 * L], spread[2 * L:]
    xc = (xs * dt_x).astype(BF16)
    xw = (xs * wend_x).astype(BF16)
    skip = dskip_ref[...] * xs

    row = lax.broadcasted_iota(jnp.int32, (L, L), 0)
    col = lax.broadcasted_iota(jnp.int32, (L, L), 1)
    causal = row >= col
    first_half = lax.broadcasted_iota(jnp.int32, (L, LANES), 1) < SSD_HEAD_DIM

    for g in range(SSD_GROUPS):
        gs = slice(g * gw, (g + 1) * gw)
        bg = bm[:, g * SSD_STATE:(g + 1) * SSD_STATE]
        cg = cm[:, g * SSD_STATE:(g + 1) * SSD_STATE].astype(BF16)
        cb = _dot_nt(cg, bg.astype(BF16))
        st = state_ref[:, gs]
        y_off = _dot(cg, st.astype(BF16)) * ecum_x[:, gs]
        state_ref[:, gs] = st * ecum_x[L - 1:L, gs] + _dot(bg.T.astype(BF16), xw[:, gs])
        for k in range(heads_per_group // 2):
            ps = slice(g * gw + k * LANES, g * gw + (k + 1) * LANES)
            halves = []
            for sub in range(2):
                h = g * heads_per_group + 2 * k + sub
                seg = cum[:, h:h + 1] - cum_t[h:h + 1, :]
                decay = jnp.exp(jnp.where(causal, seg, -jnp.inf))
                halves.append(_dot((cb * decay).astype(BF16), xc[:, ps]))
            y_ref[:, ps] = (jnp.where(first_half, halves[0], halves[1])
                            + y_off[:, k * LANES:(k + 1) * LANES] + skip[:, ps])

    gated = y_ref[...] * jax.nn.silu(z_ref[...].astype(F32))
    parts = []
    for g in range(SSD_GROUPS):
        gg = gated[:, g * gw:(g + 1) * gw]
        parts.append(gg * lax.rsqrt(jnp.mean(gg * gg, axis=-1, keepdims=True) + RMS_EPS))
    o_ref[...] = (jnp.concatenate(parts, axis=1) * nw_ref[...]).astype(BF16)


def _ssd(xbc, dt_raw, z, conv_w, conv_b, dt_bias, a_log, d_skip, norm_w, batch, seq):
    t, cw = xbc.shape
    width = z.shape[1]
    nc = seq // SSD_CHUNK
    L = SSD_CHUNK
    tril = jnp.asarray(np.tril(np.ones((L, L), np.float32)), dtype=BF16)
    shift = np.zeros(((SSD_CONV - 1) * L, SSD_TAIL + L), np.float32)
    for jj in range(SSD_CONV - 1):
        shift[jj * L + np.arange(L), SSD_TAIL - (SSD_CONV - 1) + jj + np.arange(L)] = 1.0
    shift = jnp.asarray(shift, dtype=BF16)
    expand = np.zeros((LANES, width), np.float32)
    for h in range(width // SSD_HEAD_DIM):
        expand[h, h * SSD_HEAD_DIM:(h + 1) * SSD_HEAD_DIM] = 1.0
    expand = jnp.asarray(expand, dtype=BF16)
    cur = lambda b, c: (b, c, 0)
    full = lambda a: pl.BlockSpec(a.shape, lambda b, c: (0,) * a.ndim)
    per_seq = lambda a: a.reshape(batch, seq, a.shape[1])
    out = pl.pallas_call(
        _ssd_kernel,
        grid=(batch // SSD_BATCH, nc),
        in_specs=[pl.BlockSpec((SSD_BATCH, L, cw), cur), pl.BlockSpec((SSD_BATCH, L, LANES), cur),
                  pl.BlockSpec((SSD_BATCH, L, width), cur),
                  full(conv_w), full(conv_b), full(dt_bias), full(a_log), full(d_skip), full(norm_w),
                  full(tril), full(shift), full(expand)],
        out_specs=pl.BlockSpec((SSD_BATCH, L, width), cur),
        out_shape=jax.ShapeDtypeStruct((batch, seq, width), BF16),
        scratch_shapes=[pltpu.VMEM((SSD_BATCH, SSD_TAIL + L, cw), F32),
                        pltpu.VMEM((SSD_BATCH, SSD_STATE, width), F32),
                        pltpu.VMEM((SSD_BATCH, L, width), F32)],
        compiler_params=_cparams(("parallel", "arbitrary")),
        name="ssd",
    )(per_seq(xbc), per_seq(dt_raw), per_seq(z), conv_w, conv_b, dt_bias, a_log, d_skip, norm_w,
      tril, shift, expand)
    return out.reshape(t, width)


def _mem_kv_kernel(m_ref, nw_ref, w_ref, o_ref):
    o_ref[...] = _dot(_rms(m_ref[...], nw_ref[...]).astype(BF16), w_ref[...]).astype(BF16)


def _mem_kv(mem2d, norm_w, w_kv, rows):
    t, d = mem2d.shape
    n = w_kv.shape[1]
    return pl.pallas_call(
        _mem_kv_kernel,
        grid=(t // rows,),
        in_specs=[pl.BlockSpec((rows, d), lambda i: (i, 0)),
                  pl.BlockSpec((1, d), lambda i: (0, 0)),
                  pl.BlockSpec((d, n), lambda i: (0, 0))],
        out_specs=pl.BlockSpec((rows, n), lambda i: (i, 0)),
        out_shape=jax.ShapeDtypeStruct((t, n), BF16),
        compiler_params=_cparams(("parallel",)),
        name="mem_kv",
    )(mem2d, norm_w, w_kv)


POST_PIECES = 2


def _post_kernel(x_ref, a_ref, s_ref, woa_ref, wos_ref, xnw_ref, wq_ref, kv_ref, wo_ref,
                 fnw_ref, wr_ref, br_ref, tril_ref,
                 x2_ref, pay_ref, meta_ref, cnt_ref, carry_ref, h3_ref, logit_ref):
    step = pl.program_id(0)
    tm = x_ref.shape[0]
    n_exp = N_GROUPS * EXPERTS_PER_GROUP

    @pl.when(step == 0)
    def _():
        carry_ref[...] = jnp.zeros_like(carry_ref)
        h3_ref[...] = jnp.zeros_like(h3_ref)
        logit_ref[...] = jnp.zeros_like(logit_ref)

    h3_prev = h3_ref[...]
    logits = logit_ref[...]

    pm = tm // POST_PIECES
    pieces = [slice(p * pm, (p + 1) * pm) for p in range(POST_PIECES)]
    x1 = [x_ref[r, :] + _dot(a_ref[r, :], woa_ref[...]) + _dot(s_ref[r, :], wos_ref[...]) for r in pieces]

    xw = wq_ref.shape[1]
    hd = xw // XATTN_HEADS
    q = [(_dot(_rms(v, xnw_ref[...]).astype(BF16), wq_ref[...]) * (hd ** -0.5)).astype(BF16) for v in x1]
    attn = []
    for qp in q:
        heads = []
        for h in range(XATTN_HEADS):
            s = _dot_nt(qp[:, h * hd:(h + 1) * hd], kv_ref[:, h * hd:(h + 1) * hd])
            p = jnp.exp(s - jnp.max(s, axis=-1, keepdims=True))
            o = _dot(p.astype(BF16), kv_ref[:, xw + h * hd:xw + (h + 1) * hd])
            heads.append((o / jnp.sum(p, axis=-1, keepdims=True)).astype(BF16))
        attn.append(jnp.concatenate(heads, axis=1))
    x2 = [v + _dot(at, wo_ref[...]) for v, at in zip(x1, attn)]

    for r, v in zip(pieces, x2):
        x2_ref[r, :] = v
        h3 = _rms(v, fnw_ref[...])
        h_hi = h3.astype(BF16)
        h_lo = (h3 - h_hi.astype(F32)).astype(BF16)
        t_hi = _dot(h_hi, wr_ref[...])
        h3_ref[r, :] = h3
        logit_ref[r, :] = t_hi[:, :LANES] + (t_hi[:, LANES:] + _dot(h_lo, wr_ref[:, :LANES])) + br_ref[...]

    lane = lax.broadcasted_iota(jnp.int32, (tm, LANES), 1)
    big = jnp.int32(LANES)
    neg = -jnp.inf
    g_l = jnp.where((lane >= n_exp) & (lane < n_exp + N_GROUPS), logits, neg)
    g_max = jnp.max(g_l, axis=-1, keepdims=True)
    g_idx = jnp.min(jnp.where(g_l == g_max, lane - n_exp, big), axis=-1, keepdims=True)
    g_gate = 1.0 / jnp.sum(jnp.exp(g_l - g_max), axis=-1, keepdims=True)
    e_l = jnp.where((lane < n_exp) & ((lane // EXPERTS_PER_GROUP) == g_idx), logits, neg)
    m1 = jnp.max(e_l, axis=-1, keepdims=True)
    i1 = jnp.min(jnp.where(e_l == m1, lane, big), axis=-1, keepdims=True)
    e_l2 = jnp.where(lane == i1, neg, e_l)
    m2 = jnp.max(e_l2, axis=-1, keepdims=True)
    i2 = jnp.min(jnp.where(e_l2 == m2, lane, big), axis=-1, keepdims=True)
    e2 = jnp.exp(m2 - m1)
    w1 = (1.0 / (1.0 + e2)) * g_gate
    w2 = (e2 / (1.0 + e2)) * g_gate
    first_low = i1 < i2
    lo = jnp.where(first_low, i1, i2) % EXPERTS_PER_GROUP
    hi = jnp.where(first_low, i2, i1) % EXPERTS_PER_GROUP
    w_lo = jnp.where(first_low, w1, w2)
    w_hi = jnp.where(first_low, w2, w1)
    bucket = g_idx * (EXPERTS_PER_GROUP * EXPERTS_PER_GROUP) + lo * EXPERTS_PER_GROUP + hi

    blane = lax.broadcasted_iota(jnp.int32, (tm, N_BUCKETS), 1)
    onehot = (blane == bucket).astype(F32)
    before = _dot(tril_ref[...], onehot.astype(BF16)) + carry_ref[...]
    rank = jnp.sum(onehot * before, axis=-1, keepdims=True)
    carry_ref[...] = carry_ref[...] + jnp.where(step > 0, jnp.sum(onehot, axis=0, keepdims=True), 0.0)
    cnt_ref[...] = carry_ref[...]

    meta = jnp.where(lane == 0, w_lo, jnp.where(lane == 1, w_hi, jnp.where(
        lane == 2, bucket.astype(F32), jnp.where(lane == 3, rank, 0.0))))
    n_feat_rows = h3_prev.shape[1] // LANES
    for c in range(n_feat_rows):
        pay_ref[pl.ds(c, tm, stride=SLAB_PITCH), :] = h3_prev[:, c * LANES:(c + 1) * LANES]
    pay_ref[pl.ds(n_feat_rows, tm, stride=SLAB_PITCH), :] = meta
    meta_ref[...] = meta.T[:SUBLANES, :]


def _post(x2d, attn, ssd, woa, wos, xnw, wq, kv, wo, fnw, wr, br, tm, seq, mem_tokens):
    t, d = x2d.shape
    n = t // tm
    tiles_per_batch = seq // tm
    tril = jnp.asarray(np.tril(np.ones((tm, tm), np.float32), -1), dtype=BF16)
    full = lambda a: pl.BlockSpec(a.shape, lambda i: (0,) * a.ndim)
    this = lambda i: jnp.minimum(i, n - 1)
    prev = lambda i: jnp.maximum(i - 1, 0)
    row = lambda w: pl.BlockSpec((tm, w), lambda i: (this(i), 0))
    return pl.pallas_call(
        _post_kernel,
        grid=(n + 1,),
        in_specs=[row(d), row(attn.shape[1]), row(ssd.shape[1]), full(woa), full(wos), full(xnw), full(wq),
                  pl.BlockSpec((mem_tokens, kv.shape[1]), lambda i: (this(i) // tiles_per_batch, 0)),
                  full(wo), full(fnw), full(wr), full(br), full(tril)],
        out_specs=[row(d), pl.BlockSpec((tm * SLAB_PITCH, LANES), lambda i: (prev(i), 0)),
                   pl.BlockSpec((SUBLANES, tm), lambda i: (0, prev(i))),
                   pl.BlockSpec((1, N_BUCKETS), lambda i: (0, 0))],
        out_shape=[jax.ShapeDtypeStruct((t, d), F32), jax.ShapeDtypeStruct((t * SLAB_PITCH, LANES), F32),
                   jax.ShapeDtypeStruct((SUBLANES, t), F32), jax.ShapeDtypeStruct((1, N_BUCKETS), F32)],
        scratch_shapes=[pltpu.VMEM((1, N_BUCKETS), F32), pltpu.VMEM((tm, d), F32), pltpu.VMEM((tm, LANES), F32)],
        compiler_params=_cparams(("arbitrary",)),
        name="post",
    )(x2d, attn, ssd, woa, wos, xnw, wq, kv, wo, fnw, wr, br, tril)


DMA_UNROLL = 8
DMA_PRIORITIES = 2


def _dispatch_kernel(dest_ref, pay_ref, xs_hbm, sem):
    rows = dest_ref.shape[-1]

    def copy(r):
        return pltpu.make_async_copy(pay_ref.at[pl.ds(r * SLAB_PITCH, SLAB_PITCH)],
                                     xs_hbm.at[pl.ds(dest_ref[0, r], SLAB_PITCH)], sem)

    def issue(g, carry):
        for u in range(DMA_UNROLL):
            copy(g * DMA_UNROLL + u).start(priority=u % DMA_PRIORITIES)
        return carry

    lax.fori_loop(0, rows // DMA_UNROLL, issue, 0)

    def drain(g, carry):
        for u in range(DMA_UNROLL):
            copy(g * DMA_UNROLL + u).wait()
        return carry

    lax.fori_loop(0, rows // DMA_UNROLL, drain, 0)


def _dispatch(dest3d, payload, n_sorted):
    n_steps, _, rows = dest3d.shape
    return pl.pallas_call(
        _dispatch_kernel,
        grid=(n_steps,),
        in_specs=[pl.BlockSpec((None, 1, rows), lambda i: (i, 0, 0), memory_space=pltpu.SMEM),
                  pl.BlockSpec((rows * SLAB_PITCH, LANES), lambda i: (i, 0))],
        out_specs=pl.BlockSpec(memory_space=pl.ANY),
        out_shape=jax.ShapeDtypeStruct((n_sorted * SLAB_PITCH, LANES), payload.dtype),
        scratch_shapes=[pltpu.SemaphoreType.DMA(())],
        compiler_params=_cparams(("arbitrary",)),
        name="dispatch",
    )(dest3d, payload)


EXPERT_INPUT_SLOTS = 3


def _experts_kernel(grp_ref, ea_ref, eb_ref, blk_ref, valid_ref, xs_hbm, wg_ref, wu_ref, wd_ref, y_ref,
                    xbuf_ref, sem):
    i = pl.program_id(0)
    nt = pl.num_programs(0)
    d = wd_ref.shape[2]
    n_feat_rows = d // LANES
    tile_rows = MOE_TILE * SLAB_PITCH
    ahead = EXPERT_INPUT_SLOTS - 1

    def fetch(t):
        slot = t % EXPERT_INPUT_SLOTS
        return pltpu.make_async_copy(xs_hbm.at[pl.ds(blk_ref[t] * tile_rows, tile_rows)],
                                     xbuf_ref.at[slot], sem.at[slot])

    @pl.when(i == 0)
    def _():
        for t in range(ahead):
            @pl.when(valid_ref[t] == 1)
            def _():
                fetch(t).start()

    nxt = jnp.minimum(i + ahead, nt - 1)

    @pl.when((i + ahead < nt) & (valid_ref[nxt] == 1))
    def _():
        fetch(nxt).start()

    @pl.when(valid_ref[i] == 1)
    def _():
        fetch(i).wait()
        slot = i % EXPERT_INPUT_SLOTS
        token_row = lambda c: pl.ds(c, MOE_TILE, stride=SLAB_PITCH)
        x = jnp.concatenate([xbuf_ref[slot, token_row(c), :].astype(BF16) for c in range(n_feat_rows)], axis=1)
        gates = xbuf_ref[slot, token_row(n_feat_rows), :]
        experts = (ea_ref[i], eb_ref[i])
        gate = [_dot(x, wg_ref[e]) for e in experts]
        up = [_dot(x, wu_ref[e]) for e in experts]
        hid = [(jax.nn.silu(g) * u).astype(BF16) for g, u in zip(gate, up)]
        out = [_dot(hv, wd_ref[e]) for hv, e in zip(hid, experts)]
        y = gates[:, 0:1] * out[0] + gates[:, 1:2] * out[1]
        for c in range(n_feat_rows):
            y_ref[token_row(c), :] = y[:, c * LANES:(c + 1) * LANES]
        y_ref[token_row(n_feat_rows), :] = jnp.zeros((MOE_TILE, LANES), F32)


def _experts(tile_group, tile_a, tile_b, tile_blk, tile_valid, xs, wg, wu, wd):
    n_sorted = xs.shape[0] // SLAB_PITCH
    nt = n_sorted // MOE_TILE
    assert nt >= EXPERT_INPUT_SLOTS
    group = lambda w: pl.BlockSpec((EXPERTS_PER_GROUP,) + w.shape[1:], lambda i, g, a, b, k, v: (g[i], 0, 0))
    grid_spec = pltpu.PrefetchScalarGridSpec(
        num_scalar_prefetch=5,
        grid=(nt,),
        in_specs=[pl.BlockSpec(memory_space=pl.ANY), group(wg), group(wu), group(wd)],
        out_specs=pl.BlockSpec((MOE_TILE * SLAB_PITCH, LANES), lambda i, g, a, b, k, v: (k[i], 0)),
        scratch_shapes=[pltpu.VMEM((EXPERT_INPUT_SLOTS, MOE_TILE * SLAB_PITCH, LANES), F32),
                        pltpu.SemaphoreType.DMA((EXPERT_INPUT_SLOTS,))],
    )
    return pl.pallas_call(
        _experts_kernel,
        grid_spec=grid_spec,
        out_shape=jax.ShapeDtypeStruct(xs.shape, F32),
        compiler_params=_cparams(("arbitrary",)),
        name="experts",
    )(tile_group, tile_a, tile_b, tile_blk, tile_valid, xs, wg, wu, wd)


COMBINE_ROW_BLOCK = 64
COMBINE_UNROLL = 4


def _combine_kernel(dcur_ref, dnext_ref, x2_ref, nw_ref, ys_hbm, o_ref, buf_ref, sem):
    i = pl.program_id(0)
    n = pl.num_programs(0)
    rows, d = x2_ref.shape
    n_feat_rows = d // LANES
    slot = i % 2

    def gather(dref, s, start):
        def body(g, carry):
            for u in range(DMA_UNROLL):
                r = g * DMA_UNROLL + u
                cp = pltpu.make_async_copy(ys_hbm.at[pl.ds(dref[0, r], n_feat_rows)],
                                           buf_ref.at[s, pl.ds(r * SLAB_PITCH, n_feat_rows)], sem.at[s])
                if start:
                    cp.start(priority=u % DMA_PRIORITIES)
                else:
                    cp.wait()
            return carry

        lax.fori_loop(0, rows // DMA_UNROLL, body, 0)

    @pl.when(i == 0)
    def _():
        gather(dcur_ref, slot, True)

    @pl.when(i + 1 < n)
    def _():
        gather(dnext_ref, 1 - slot, True)

    gather(dcur_ref, slot, False)

    def block(b, carry):
        r0 = pl.multiple_of(b * COMBINE_ROW_BLOCK, COMBINE_ROW_BLOCK)
        rs = pl.ds(r0, COMBINE_ROW_BLOCK)
        chunks = [x2_ref[rs, c * LANES:(c + 1) * LANES]
                  + buf_ref[slot, pl.ds(r0 * SLAB_PITCH + c, COMBINE_ROW_BLOCK, stride=SLAB_PITCH), :]
                  for c in range(n_feat_rows)]
        ssq = chunks[0] * chunks[0]
        for ch in chunks[1:]:
            ssq = ssq + ch * ch
        scale = lax.rsqrt(jnp.sum(ssq, axis=-1, keepdims=True) / d + RMS_EPS)
        for c, ch in enumerate(chunks):
            o_ref[rs, c * LANES:(c + 1) * LANES] = ch * scale * nw_ref[:, c * LANES:(c + 1) * LANES]
        return carry

    lax.fori_loop(0, rows // COMBINE_ROW_BLOCK, block, 0, unroll=COMBINE_UNROLL)


def _combine(dest3d, x2, norm_w, ys):
    n_steps, _, rows = dest3d.shape
    t, d = x2.shape
    return pl.pallas_call(
        _combine_kernel,
        grid=(n_steps,),
        in_specs=[pl.BlockSpec((None, 1, rows), lambda i: (i, 0, 0), memory_space=pltpu.SMEM),
                  pl.BlockSpec((None, 1, rows), lambda i: (jnp.minimum(i + 1, n_steps - 1), 0, 0),
                               memory_space=pltpu.SMEM),
                  pl.BlockSpec((rows, d), lambda i: (i, 0)),
                  pl.BlockSpec((1, d), lambda i: (0, 0)),
                  pl.BlockSpec(memory_space=pl.ANY)],
        out_specs=pl.BlockSpec((rows, d), lambda i: (i, 0)),
        out_shape=jax.ShapeDtypeStruct((t, d), F32),
        scratch_shapes=[pltpu.VMEM((2, rows * SLAB_PITCH, LANES), F32), pltpu.SemaphoreType.DMA((2,))],
        compiler_params=_cparams(("arbitrary",)),
        name="combine",
    )(dest3d, dest3d, x2, norm_w, ys)


def _pad_cols(w, n):
    return jnp.pad(w, ((0, 0), (0, n - w.shape[1])))


def _tile_plan(counts, n_tiles):
    per_bucket = (counts + (MOE_TILE - 1)) // MOE_TILE
    tile_end = jnp.cumsum(per_bucket)
    tile_start = tile_end - per_bucket
    total = tile_end[-1]
    ids = jnp.arange(n_tiles, dtype=jnp.int32)
    blk = jnp.minimum(ids, total - 1)
    bucket = jnp.sum((tile_end[None, :] <= blk[:, None]).astype(jnp.int32), axis=1)
    pair = EXPERTS_PER_GROUP * EXPERTS_PER_GROUP
    group = bucket // pair
    slot_a = (bucket % pair) // EXPERTS_PER_GROUP
    slot_b = bucket % EXPERTS_PER_GROUP
    valid = (ids < total).astype(jnp.int32)
    return tile_start * MOE_TILE, group, slot_a, slot_b, blk, valid


def kernel(x, mem, positions, mix_norm_w, w_in, attn_sinks, ssd_conv_w, ssd_conv_b, ssd_dt_bias, ssd_a_log, ssd_d, attn_out_norm_w, ssd_out_norm_w, w_out, xattn_norm_w, mem_norm_w, xattn_w_q, xattn_w_kv, xattn_w_o, ffn_norm_w, router_group_w, router_group_b, router_expert_w, router_expert_b, expert_w_gate, expert_w_up, expert_w_down, final_norm_w):
    batch, seq, d = x.shape
    mem_tokens = mem.shape[1]
    depth = w_in.shape[0]
    t = batch * seq
    attn_w = attn_out_norm_w.shape[1]
    ssd_w = ssd_out_norm_w.shape[1]
    kv_w = N_KV_HEADS * HEAD_DIM
    conv_dim = ssd_conv_w.shape[2]
    n_exp = router_expert_w.shape[2]
    assert n_exp == N_GROUPS * EXPERTS_PER_GROUP and router_group_w.shape[2] == N_GROUPS
    assert seq % (SWA_SUB * ATTN_BLOCK) == 0 and seq % SSD_CHUNK == 0 and t % MOE_TILE == 0
    assert batch % SSD_BATCH == 0
    tm = min(512, seq)
    dispatch_rows = min(2048, t)
    combine_rows = min(1024, t)
    n_tiles = t // MOE_TILE + N_USED_BUCKETS
    n_sorted = n_tiles * MOE_TILE

    pos_row = positions.reshape(1, t)
    x2d = x.reshape(t, d)
    mem2d = mem.reshape(batch * mem_tokens, d)
    row = lambda v: v.reshape(1, -1)

    for l in range(depth):
        widths = (attn_w, kv_w, kv_w, ssd_w, conv_dim, LANES)
        n_main = sum(widths[:-1])
        q, k, v, z, xbc, dt_raw = _in_proj(pos_row, x2d, row(mix_norm_w[l]), w_in[l][:, :n_main].astype(BF16),
                                           _pad_cols(w_in[l][:, n_main:], LANES).astype(BF16), widths, tm)
        attn = _swa(attn_sinks[l], q, k, v, row(attn_out_norm_w[l]), batch, seq)
        ssd = _ssd(xbc, dt_raw, z, ssd_conv_w[l], row(ssd_conv_b[l]),
                   _pad_cols(row(ssd_dt_bias[l]), LANES), _pad_cols(row(ssd_a_log[l]), LANES),
                   row(jnp.repeat(ssd_d[l], SSD_HEAD_DIM)), row(ssd_out_norm_w[l]), batch, seq)
        kv = _mem_kv(mem2d, row(mem_norm_w[l]), xattn_w_kv[l].astype(BF16), mem_tokens)

        wo = w_out[l].astype(BF16)
        wr32 = _pad_cols(jnp.concatenate([router_expert_w[l], router_group_w[l]], axis=1), LANES)
        wr_hi = wr32.astype(BF16)
        wr = jnp.concatenate([wr_hi, (wr32 - wr_hi.astype(F32)).astype(BF16)], axis=1)
        br = _pad_cols(row(jnp.concatenate([router_expert_b[l], router_group_b[l]])), LANES)
        x2, payload, meta, counts = _post(
            x2d, attn, ssd, wo[:attn_w], wo[attn_w:], row(xattn_norm_w[l]), xattn_w_q[l].astype(BF16), kv,
            xattn_w_o[l].astype(BF16), row(ffn_norm_w[l]), wr, br, tm, seq, mem_tokens)

        row_start, tile_group, tile_a, tile_b, tile_blk, tile_valid = _tile_plan(
            counts[0].astype(jnp.int32), n_tiles)
        bucket = meta[2].astype(jnp.int32)
        hit = bucket[:, None] == jnp.arange(N_BUCKETS, dtype=jnp.int32)[None, :]
        dest = jnp.sum(jnp.where(hit, row_start[None, :], 0), axis=1) + meta[3].astype(jnp.int32)
        dest = dest * SLAB_PITCH
        xs = _dispatch(dest.reshape(t // dispatch_rows, 1, dispatch_rows), payload, n_sorted)
        ys = _experts(tile_group, tile_a, tile_b, tile_blk, tile_valid, xs, expert_w_gate[l].astype(BF16),
                      expert_w_up[l].astype(BF16), expert_w_down[l].astype(BF16))
        assert depth == 1
        x2d = _combine(dest.reshape(t // combine_rows, 1, combine_rows), x2, row(final_norm_w), ys)

    return x2d.reshape(batch, seq, d)
```

```python
import functools

import numpy as np
import jax
import jax.numpy as jnp
from jax import lax
from jax.experimental import pallas as pl
from jax.experimental.pallas import tpu as pltpu

RMS_EPS = 1e-5
HEAD_DIM = 64
N_KV_HEADS = 2
WINDOW = 128
ATTN_BLOCK = 128
ROT_DIM = 16
ROT_HALF = ROT_DIM // 2
ROPE_THETA = 500000.0
SSD_HEAD_DIM = 64
SSD_GROUPS = 2
SSD_STATE = 128
SSD_CONV = 4
SSD_CHUNK = 128
XATTN_HEADS = 4
N_GROUPS = 4
EXPERTS_PER_GROUP = 8

LANES = 128
SUBLANES = 8
MOE_TILE = 256
SLAB_PITCH = SUBLANES + 1
N_BUCKETS = N_GROUPS * EXPERTS_PER_GROUP * EXPERTS_PER_GROUP
N_USED_BUCKETS = N_GROUPS * (EXPERTS_PER_GROUP * (EXPERTS_PER_GROUP - 1) // 2)
VMEM_LIMIT = 56 * 1024 * 1024

F32 = jnp.float32
BF16 = jnp.bfloat16


def _cparams(semantics):
    return pltpu.CompilerParams(dimension_semantics=semantics, vmem_limit_bytes=VMEM_LIMIT)


def _rms(x, w):
    return x * lax.rsqrt(jnp.mean(x * x, axis=-1, keepdims=True) + RMS_EPS) * w


def _dot(a, b):
    return jnp.dot(a, b, preferred_element_type=F32)


def _dot_nt(a, b):
    return lax.dot_general(a, b, (((1,), (1,)), ((), ())), preferred_element_type=F32)


def _in_proj_kernel(pos_ref, x_ref, nw_ref, w_ref, wdt_ref, place_ref, base_ref,
                    q_ref, k_ref, v_ref, z_ref, xbc_ref, dt_ref):
    x = x_ref[...]
    hb = _rms(x, nw_ref[...]).astype(BF16)

    j = lax.broadcasted_iota(jnp.int32, (ROT_HALF, 1), 0).astype(F32)
    inv_freq = jnp.power(jnp.float32(ROPE_THETA), -(2.0 * j) / ROT_DIM)
    ang = pos_ref[...].astype(F32) * inv_freq
    cs = jnp.concatenate([jnp.cos(ang), jnp.sin(ang)], axis=0)

    cs_hi = cs.astype(BF16)
    cs_lo = (cs - cs_hi.astype(F32)).astype(BF16)
    tn = (((0,), (0,)), ((), ()))
    tabs = (lax.dot_general(cs_hi, place_ref[...], tn, preferred_element_type=F32)
            + lax.dot_general(cs_lo, place_ref[...], tn, preferred_element_type=F32))
    c_tab = tabs[:, :LANES] + base_ref[...]
    s_tab = tabs[:, LANES:]
    first = (lax.broadcasted_iota(jnp.int32, s_tab.shape, 1) % HEAD_DIM) < ROT_HALF
    s_up = jnp.where(first, s_tab, 0.0)
    s_dn = jnp.where(first, 0.0, s_tab)

    def rope(t):
        n = t.shape[1]
        reps = n // LANES
        c = jnp.tile(c_tab, (1, reps)) if reps > 1 else c_tab
        su = jnp.tile(s_up, (1, reps)) if reps > 1 else s_up
        sd = jnp.tile(s_dn, (1, reps)) if reps > 1 else s_dn
        return t * c + pltpu.roll(t, n - ROT_HALF, 1) * su + pltpu.roll(t, ROT_HALF, 1) * sd

    proj = _dot(hb, w_ref[...])
    o = 0
    pieces = []
    for ref in (q_ref, k_ref, v_ref, z_ref, xbc_ref):
        pieces.append(proj[:, o:o + ref.shape[1]])
        o += ref.shape[1]
    q, k, v, z, xbc = pieces
    dt = _dot(hb, wdt_ref[...])
    q_ref[...] = (rope(q) * (HEAD_DIM ** -0.5)).astype(BF16)
    k_ref[...] = rope(k).astype(BF16)
    v_ref[...] = v.astype(BF16)
    z_ref[...] = z.astype(BF16)
    xbc_ref[...] = xbc
    dt_ref[...] = dt


def _rope_placement():
    d = np.arange(LANES) % HEAD_DIM
    jj = np.arange(ROT_HALF)[:, None]
    ec = ((d[None, :] < ROT_DIM) & ((d[None, :] % ROT_HALF) == jj)).astype(np.float32)
    eup = -((d[None, :] < ROT_HALF) & (d[None, :] == jj)).astype(np.float32)
    edn = ((d[None, :] >= ROT_HALF) & (d[None, :] < ROT_DIM) & ((d[None, :] - ROT_HALF) == jj)).astype(np.float32)
    zero = np.zeros_like(ec)
    place = np.concatenate([np.concatenate([ec, zero], axis=1),
                            np.concatenate([zero, eup + edn], axis=1)], axis=0)
    base = (d >= ROT_DIM).astype(np.float32)[None, :]
    return jnp.asarray(place, dtype=BF16), jnp.asarray(base)


def _in_proj(pos_row, x2d, norm_w, w_main, w_dt, widths, tm):
    t, d = x2d.shape
    place, base = _rope_placement()
    full = lambda a: pl.BlockSpec(a.shape, lambda i: (0,) * a.ndim)
    row = lambda n: pl.BlockSpec((tm, n), lambda i: (i, 0))
    dtypes = (BF16, BF16, BF16, BF16, F32, F32)
    return pl.pallas_call(
        _in_proj_kernel,
        grid=(t // tm,),
        in_specs=[pl.BlockSpec((1, tm), lambda i: (0, i)), row(d), full(norm_w), full(w_main), full(w_dt),
                  full(place), full(base)],
        out_specs=[row(n) for n in widths],
        out_shape=[jax.ShapeDtypeStruct((t, n), dt) for n, dt in zip(widths, dtypes)],
        compiler_params=_cparams(("parallel",)),
        name="in_proj",
    )(pos_row, x2d, norm_w, w_main, w_dt, place, base)


SWA_SUB = 4


def _swa_kernel(sink_ref, q_ref, kp_ref, kc_ref, vp_ref, vc_ref, nw_ref, o_ref,
                s_ref, p_ref, bias_ref, acc_ref):
    i = pl.program_id(1)
    blk = ATTN_BLOCK
    n_q_heads = q_ref.shape[1] // HEAD_DIM
    q_per_kv = n_q_heads // N_KV_HEADS
    qi = lax.broadcasted_iota(jnp.int32, (blk, 2 * blk), 0) + blk
    ki = lax.broadcasted_iota(jnp.int32, (blk, 2 * blk), 1)
    rel = qi - ki
    local = (rel >= 0) & (rel < WINDOW)
    bias_ref[0] = jnp.where(local & ((i > 0) | (ki >= blk)), 0.0, -jnp.inf)
    bias_ref[1] = jnp.where(local, 0.0, -jnp.inf)
    sink_col = ki == 0

    def band(prev_ref, cur_ref, j):
        if j == 0:
            return jnp.concatenate([prev_ref[...], cur_ref[0:blk, :]], axis=0)
        return cur_ref[(j - 1) * blk:(j + 1) * blk, :]

    for j in range(SWA_SUB):
        kband = band(kp_ref, kc_ref, j)
        for kv in range(N_KV_HEADS):
            kcat = kband[:, kv * HEAD_DIM:(kv + 1) * HEAD_DIM]
            for g in range(q_per_kv):
                h = kv * q_per_kv + g
                r0 = (j * n_q_heads + h) * blk
                s_ref[r0:r0 + blk, :] = _dot_nt(q_ref[j * blk:(j + 1) * blk, h * HEAD_DIM:(h + 1) * HEAD_DIM], kcat)

    for j in range(SWA_SUB):
        for h in range(n_q_heads):
            rows = slice((j * n_q_heads + h) * blk, (j * n_q_heads + h + 1) * blk)
            s = jnp.where(sink_col, sink_ref[h], s_ref[rows, :] + bias_ref[min(j, 1)])
            p_ref[rows, :] = jnp.exp(s - jnp.max(s, axis=-1, keepdims=True)).astype(BF16)

    ones = jnp.ones((2 * blk, LANES), BF16)
    low = lax.broadcasted_iota(jnp.int32, (blk, LANES), 1) < HEAD_DIM
    for j in range(SWA_SUB):
        vband = band(vp_ref, vc_ref, j)
        vband = jnp.where(lax.broadcasted_iota(jnp.int32, vband.shape, 0) == 0, jnp.zeros_like(vband), vband)
        for kv in range(N_KV_HEADS):
            v_kv = vband[:, kv * HEAD_DIM:(kv + 1) * HEAD_DIM]
            vaug = jnp.concatenate([v_kv, v_kv, ones], axis=1)
            for k in range(q_per_kv // 2):
                h = kv * q_per_kv + 2 * k
                r0 = (j * n_q_heads + h) * blk
                o_even = _dot(p_ref[r0:r0 + blk, :], vaug)
                o_odd = _dot(p_ref[r0 + blk:r0 + 2 * blk, :], vaug)
                num = jnp.where(low, o_even[:, :LANES], o_odd[:, :LANES])
                den = jnp.where(low, o_even[:, LANES:], o_odd[:, LANES:])
                acc_ref[j * blk:(j + 1) * blk, h * HEAD_DIM:(h + 2) * HEAD_DIM] = num * (1.0 / den)
    o_ref[...] = _rms(acc_ref[...], nw_ref[...]).astype(BF16)


def _swa(sinks, q, k, v, norm_w, batch, seq):
    t, qw = q.shape
    kw = k.shape[1]
    rows = SWA_SUB * ATTN_BLOCK
    nb = seq // rows
    n_heads = qw // HEAD_DIM
    cur = lambda b, i: (b * nb + i, 0)
    prev = lambda b, i: ((b * nb + i) * SWA_SUB - jnp.minimum(i, 1), 0)
    return pl.pallas_call(
        _swa_kernel,
        grid=(batch, nb),
        in_specs=[pl.BlockSpec(memory_space=pltpu.SMEM),
                  pl.BlockSpec((rows, qw), cur),
                  pl.BlockSpec((ATTN_BLOCK, kw), prev), pl.BlockSpec((rows, kw), cur),
                  pl.BlockSpec((ATTN_BLOCK, kw), prev), pl.BlockSpec((rows, kw), cur),
                  pl.BlockSpec((1, qw), lambda b, i: (0, 0))],
        out_specs=pl.BlockSpec((rows, qw), cur),
        out_shape=jax.ShapeDtypeStruct((t, qw), BF16),
        scratch_shapes=[pltpu.VMEM((SWA_SUB * n_heads * ATTN_BLOCK, 2 * ATTN_BLOCK), F32),
                        pltpu.VMEM((SWA_SUB * n_heads * ATTN_BLOCK, 2 * ATTN_BLOCK), BF16),
                        pltpu.VMEM((2, ATTN_BLOCK, 2 * ATTN_BLOCK), F32),
                        pltpu.VMEM((rows, qw), F32)],
        compiler_params=_cparams(("parallel", "parallel")),
        name="swa",
    )(sinks, q, k, k, v, v, norm_w)


SSD_TAIL = 16
SSD_BATCH = 4


def _ssd_kernel(xbc_ref, dt_ref, z_ref, cw_ref, cb_ref, dtb_ref, alog_ref, dskip_ref, nw_ref,
                tril_ref, shift_ref, expand_ref, o_ref, ext_ref, state_ref, y_ref):
    @pl.when(pl.program_id(1) == 0)
    def _():
        state_ref[...] = jnp.zeros_like(state_ref)
        ext_ref[:, 0:SSD_TAIL, :] = jnp.zeros((SSD_BATCH, SSD_TAIL, ext_ref.shape[2]), F32)

    for slot in range(SSD_BATCH):
        _ssd_chunk(xbc_ref.at[slot], dt_ref.at[slot], z_ref.at[slot], cw_ref, cb_ref, dtb_ref, alog_ref,
                   dskip_ref, nw_ref, tril_ref, shift_ref, expand_ref, o_ref.at[slot],
                   ext_ref.at[slot], state_ref.at[slot], y_ref.at[slot])


def _ssd_chunk(xbc_ref, dt_ref, z_ref, cw_ref, cb_ref, dtb_ref, alog_ref, dskip_ref, nw_ref,
               tril_ref, shift_ref, expand_ref, o_ref, ext_ref, state_ref, y_ref):
    L = SSD_CHUNK
    width = z_ref.shape[1]
    n_heads = width // SSD_HEAD_DIM
    heads_per_group = n_heads // SSD_GROUPS
    gw = width // SSD_GROUPS

    cur = xbc_ref[...]
    ext_ref[SSD_TAIL:SSD_TAIL + L, :] = cur
    shifted = _dot(shift_ref[...], ext_ref[...].astype(BF16))
    ext_ref[0:SSD_TAIL, :] = cur[L - SSD_TAIL:, :]
    acc = cb_ref[...] + cw_ref[SSD_CONV - 1:SSD_CONV, :] * cur
    for jj in range(SSD_CONV - 1):
        acc = acc + cw_ref[jj:jj + 1, :] * shifted[jj * L:(jj + 1) * L, :]
    u = jax.nn.silu(acc)
    xs = u[:, :width]
    bm = u[:, width:width + SSD_GROUPS * SSD_STATE]
    cm = u[:, width + SSD_GROUPS * SSD_STATE:]

    dt = jax.nn.softplus(dt_ref[...] + dtb_ref[...])
    da = dt * (-jnp.exp(alog_ref[...]))
    d1 = da.astype(BF16)
    r1 = da - d1.astype(F32)
    d2 = r1.astype(BF16)
    d3 = (r1 - d2.astype(F32)).astype(BF16)
    cum12 = _dot(tril_ref[...], jnp.concatenate([d1, d2], axis=1))
    cum = cum12[:, :LANES] + (cum12[:, LANES:] + _dot(tril_ref[...], d3))
    cum_t = cum.T
    cum_last = cum[L - 1:L, :]
    e_cum = jnp.exp(cum)
    w_end = dt * jnp.exp(cum_last - cum)

    s_hi = jnp.concatenate([dt, w_end, e_cum], axis=0).astype(BF16)
    spread = _dot(s_hi, expand_ref[...])
    e_lo = (e_cum - s_hi[2 * L:].astype(F32)).astype(BF16)
    dt_x, wend_x, ecum_x = spread[:L], spread[L:2 * L], spread[2 * L:] + _dot(e_lo, expand_ref[...])
    xc = (xs * dt_x).astype(BF16)
    xw = (xs * wend_x).astype(BF16)
    skip = dskip_ref[...] * xs

    row = lax.broadcasted_iota(jnp.int32, (L, L), 0)
    col = lax.broadcasted_iota(jnp.int32, (L, L), 1)
    causal = row >= col
    first_half = lax.broadcasted_iota(jnp.int32, (L, LANES), 1) < SSD_HEAD_DIM

    for g in range(SSD_GROUPS):
        gs = slice(g * gw, (g + 1) * gw)
        bg = bm[:, g * SSD_STATE:(g + 1) * SSD_STATE]
        cg = cm[:, g * SSD_STATE:(g + 1) * SSD_STATE].astype(BF16)
        cb = _dot_nt(cg, bg.astype(BF16))
        st = state_ref[:, gs]
        y_off = _dot(cg, st.astype(BF16)) * ecum_x[:, gs]
        state_ref[:, gs] = st * ecum_x[L - 1:L, gs] + _dot(bg.T.astype(BF16), xw[:, gs])
        for k in range(heads_per_group // 2):
            ps = slice(g * gw + k * LANES, g * gw + (k + 1) * LANES)
            halves = []
            for sub in range(2):
                h = g * heads_per_group + 2 * k + sub
                seg = cum[:, h:h + 1] - cum_t[h:h + 1, :]
                decay = jnp.exp(jnp.where(causal, seg, -jnp.inf))
                halves.append(_dot((cb * decay).astype(BF16), xc[:, ps]))
            y_ref[:, ps] = (jnp.where(first_half, halves[0], halves[1])
                            + y_off[:, k * LANES:(k + 1) * LANES] + skip[:, ps])

    gated = y_ref[...] * jax.nn.silu(z_ref[...].astype(F32))
    parts = []
    for g in range(SSD_GROUPS):
        gg = gated[:, g * gw:(g + 1) * gw]
        parts.append(gg * lax.rsqrt(jnp.mean(gg * gg, axis=-1, keepdims=True) + RMS_EPS))
    o_ref[...] = (jnp.concatenate(parts, axis=1) * nw_ref[...]).astype(BF16)


def _ssd(xbc, dt_raw, z, conv_w, conv_b, dt_bias, a_log, d_skip, norm_w, batch, seq):
    t, cw = xbc.shape
    width = z.shape[1]
    nc = seq // SSD_CHUNK
    L = SSD_CHUNK
    tril = jnp.asarray(np.tril(np.ones((L, L), np.float32)), dtype=BF16)
    shift = np.zeros(((SSD_CONV - 1) * L, SSD_TAIL + L), np.float32)
    for jj in range(SSD_CONV - 1):
        shift[jj * L + np.arange(L), SSD_TAIL - (SSD_CONV - 1) + jj + np.arange(L)] = 1.0
    shift = jnp.asarray(shift, dtype=BF16)
    expand = np.zeros((LANES, width), np.float32)
    for h in range(width // SSD_HEAD_DIM):
        expand[h, h * SSD_HEAD_DIM:(h + 1) * SSD_HEAD_DIM] = 1.0
    expand = jnp.asarray(expand, dtype=BF16)
    cur = lambda b, c: (b, c, 0)
    full = lambda a: pl.BlockSpec(a.shape, lambda b, c: (0,) * a.ndim)
    per_seq = lambda a: a.reshape(batch, seq, a.shape[1])
    out = pl.pallas_call(
        _ssd_kernel,
        grid=(batch // SSD_BATCH, nc),
        in_specs=[pl.BlockSpec((SSD_BATCH, L, cw), cur), pl.BlockSpec((SSD_BATCH, L, LANES), cur),
                  pl.BlockSpec((SSD_BATCH, L, width), cur),
                  full(conv_w), full(conv_b), full(dt_bias), full(a_log), full(d_skip), full(norm_w),
                  full(tril), full(shift), full(expand)],
        out_specs=pl.BlockSpec((SSD_BATCH, L, width), cur),
        out_shape=jax.ShapeDtypeStruct((batch, seq, width), BF16),
        scratch_shapes=[pltpu.VMEM((SSD_BATCH, SSD_TAIL + L, cw), F32),
                        pltpu.VMEM((SSD_BATCH, SSD_STATE, width), F32),
                        pltpu.VMEM((SSD_BATCH, L, width), F32)],
        compiler_params=_cparams(("parallel", "arbitrary")),
        name="ssd",
    )(per_seq(xbc), per_seq(dt_raw), per_seq(z), conv_w, conv_b, dt_bias, a_log, d_skip, norm_w,
      tril, shift, expand)
    return out.reshape(t, width)


def _mem_kv_kernel(m_ref, nw_ref, w_ref, o_ref):
    o_ref[...] = _dot(_rms(m_ref[...], nw_ref[...]).astype(BF16), w_ref[...]).astype(BF16)


def _mem_kv(mem2d, norm_w, w_kv, rows):
    t, d = mem2d.shape
    n = w_kv.shape[1]
    return pl.pallas_call(
        _mem_kv_kernel,
        grid=(t // rows,),
        in_specs=[pl.BlockSpec((rows, d), lambda i: (i, 0)),
                  pl.BlockSpec((1, d), lambda i: (0, 0)),
                  pl.BlockSpec((d, n), lambda i: (0, 0))],
        out_specs=pl.BlockSpec((rows, n), lambda i: (i, 0)),
        out_shape=jax.ShapeDtypeStruct((t, n), BF16),
        compiler_params=_cparams(("parallel",)),
        name="mem_kv",
    )(mem2d, norm_w, w_kv)


POST_PIECES = 2


def _post_kernel(x_ref, a_ref, s_ref, woa_ref, wos_ref, xnw_ref, wq_ref, kv_ref, wo_ref,
                 fnw_ref, wr_ref, br_ref, tril_ref,
                 x2_ref, pay_ref, meta_ref, cnt_ref, carry_ref, h3_ref, logit_ref):
    step = pl.program_id(0)
    tm = x_ref.shape[0]
    n_exp = N_GROUPS * EXPERTS_PER_GROUP

    @pl.when(step == 0)
    def _():
        carry_ref[...] = jnp.zeros_like(carry_ref)
        h3_ref[...] = jnp.zeros_like(h3_ref)
        logit_ref[...] = jnp.zeros_like(logit_ref)

    h3_prev = h3_ref[...]
    logits = logit_ref[...]

    pm = tm // POST_PIECES
    pieces = [slice(p * pm, (p + 1) * pm) for p in range(POST_PIECES)]
    x1 = [x_ref[r, :] + _dot(a_ref[r, :], woa_ref[...]) + _dot(s_ref[r, :], wos_ref[...]) for r in pieces]

    xw = wq_ref.shape[1]
    hd = xw // XATTN_HEADS
    q = [(_dot(_rms(v, xnw_ref[...]).astype(BF16), wq_ref[...]) * (hd ** -0.5)).astype(BF16) for v in x1]
    attn = []
    for qp in q:
        heads = []
        for h in range(XATTN_HEADS):
            s = _dot_nt(qp[:, h * hd:(h + 1) * hd], kv_ref[:, h * hd:(h + 1) * hd])
            p = jnp.exp(s - jnp.max(s, axis=-1, keepdims=True))
            o = _dot(p.astype(BF16), kv_ref[:, xw + h * hd:xw + (h + 1) * hd])
            heads.append((o / jnp.sum(p, axis=-1, keepdims=True)).astype(BF16))
        attn.append(jnp.concatenate(heads, axis=1))
    x2 = [v + _dot(at, wo_ref[...]) for v, at in zip(x1, attn)]

    for r, v in zip(pieces, x2):
        x2_ref[r, :] = v
        h3 = _rms(v, fnw_ref[...])
        h_hi = h3.astype(BF16)
        h_lo = (h3 - h_hi.astype(F32)).astype(BF16)
        t_hi = _dot(h_hi, wr_ref[...])
        h3_ref[r, :] = h3
        logit_ref[r, :] = t_hi[:, :LANES] + (t_hi[:, LANES:] + _dot(h_lo, wr_ref[:, :LANES])) + br_ref[...]

    lane = lax.broadcasted_iota(jnp.int32, (tm, LANES), 1)
    big = jnp.int32(LANES)
    neg = -jnp.inf
    g_l = jnp.where((lane >= n_exp) & (lane < n_exp + N_GROUPS), logits, neg)
    g_max = jnp.max(g_l, axis=-1, keepdims=True)
    g_idx = jnp.min(jnp.where(g_l == g_max, lane - n_exp, big), axis=-1, keepdims=True)
    g_gate = 1.0 / jnp.sum(jnp.exp(g_l - g_max), axis=-1, keepdims=True)
    e_l = jnp.where((lane < n_exp) & ((lane // EXPERTS_PER_GROUP) == g_idx), logits, neg)
    m1 = jnp.max(e_l, axis=-1, keepdims=True)
    i1 = jnp.min(jnp.where(e_l == m1, lane, big), axis=-1, keepdims=True)
    e_l2 = jnp.where(lane == i1, neg, e_l)
    m2 = jnp.max(e_l2, axis=-1, keepdims=True)
    i2 = jnp.min(jnp.where(e_l2 == m2, lane, big), axis=-1, keepdims=True)
    e2 = jnp.exp(m2 - m1)
    w1 = (1.0 / (1.0 + e2)) * g_gate
    w2 = (e2 / (1.0 + e2)) * g_gate
    first_low = i1 < i2
    lo = jnp.where(first_low, i1, i2) % EXPERTS_PER_GROUP
    hi = jnp.where(first_low, i2, i1) % EXPERTS_PER_GROUP
    w_lo = jnp.where(first_low, w1, w2)
    w_hi = jnp.where(first_low, w2, w1)
    bucket = g_idx * (EXPERTS_PER_GROUP * EXPERTS_PER_GROUP) + lo * EXPERTS_PER_GROUP + hi

    blane = lax.broadcasted_iota(jnp.int32, (tm, N_BUCKETS), 1)
    onehot = (blane == bucket).astype(F32)
    before = _dot(tril_ref[...], onehot.astype(BF16)) + carry_ref[...]
    rank = jnp.sum(onehot * before, axis=-1, keepdims=True)
    carry_ref[...] = carry_ref[...] + jnp.where(step > 0, jnp.sum(onehot, axis=0, keepdims=True), 0.0)
    cnt_ref[...] = carry_ref[...]

    meta = jnp.where(lane == 0, w_lo, jnp.where(lane == 1, w_hi, jnp.where(
        lane == 2, bucket.astype(F32), jnp.where(lane == 3, rank, 0.0))))
    n_feat_rows = h3_prev.shape[1] // LANES
    for c in range(n_feat_rows):
        pay_ref[pl.ds(c, tm, stride=SLAB_PITCH), :] = h3_prev[:, c * LANES:(c + 1) * LANES]
    pay_ref[pl.ds(n_feat_rows, tm, stride=SLAB_PITCH), :] = meta
    meta_ref[...] = meta.T[:SUBLANES, :]


def _post(x2d, attn, ssd, woa, wos, xnw, wq, kv, wo, fnw, wr, br, tm, seq, mem_tokens):
    t, d = x2d.shape
    n = t // tm
    tiles_per_batch = seq // tm
    tril = jnp.asarray(np.tril(np.ones((tm, tm), np.float32), -1), dtype=BF16)
    full = lambda a: pl.BlockSpec(a.shape, lambda i: (0,) * a.ndim)
    this = lambda i: jnp.minimum(i, n - 1)
    prev = lambda i: jnp.maximum(i - 1, 0)
    row = lambda w: pl.BlockSpec((tm, w), lambda i: (this(i), 0))
    return pl.pallas_call(
        _post_kernel,
        grid=(n + 1,),
        in_specs=[row(d), row(attn.shape[1]), row(ssd.shape[1]), full(woa), full(wos), full(xnw), full(wq),
                  pl.BlockSpec((mem_tokens, kv.shape[1]), lambda i: (this(i) // tiles_per_batch, 0)),
                  full(wo), full(fnw), full(wr), full(br), full(tril)],
        out_specs=[row(d), pl.BlockSpec((tm * SLAB_PITCH, LANES), lambda i: (prev(i), 0)),
                   pl.BlockSpec((SUBLANES, tm), lambda i: (0, prev(i))),
                   pl.BlockSpec((1, N_BUCKETS), lambda i: (0, 0))],
        out_shape=[jax.ShapeDtypeStruct((t, d), F32), jax.ShapeDtypeStruct((t * SLAB_PITCH, LANES), F32),
                   jax.ShapeDtypeStruct((SUBLANES, t), F32), jax.ShapeDtypeStruct((1, N_BUCKETS), F32)],
        scratch_shapes=[pltpu.VMEM((1, N_BUCKETS), F32), pltpu.VMEM((tm, d), F32), pltpu.VMEM((tm, LANES), F32)],
        compiler_params=_cparams(("arbitrary",)),
        name="post",
    )(x2d, attn, ssd, woa, wos, xnw, wq, kv, wo, fnw, wr, br, tril)


DMA_UNROLL = 8
DMA_PRIORITIES = 2


def _dispatch_kernel(dest_ref, pay_ref, xs_hbm, sem):
    rows = dest_ref.shape[-1]

    def copy(r):
        return pltpu.make_async_copy(pay_ref.at[pl.ds(r * SLAB_PITCH, SLAB_PITCH)],
                                     xs_hbm.at[pl.ds(dest_ref[0, r], SLAB_PITCH)], sem)

    def issue(g, carry):
        for u in range(DMA_UNROLL):
            copy(g * DMA_UNROLL + u).start(priority=u % DMA_PRIORITIES)
        return carry

    lax.fori_loop(0, rows // DMA_UNROLL, issue, 0)

    def drain(g, carry):
        for u in range(DMA_UNROLL):
            copy(g * DMA_UNROLL + u).wait()
        return carry

    lax.fori_loop(0, rows // DMA_UNROLL, drain, 0)


def _dispatch(dest3d, payload, n_sorted):
    n_steps, _, rows = dest3d.shape
    return pl.pallas_call(
        _dispatch_kernel,
        grid=(n_steps,),
        in_specs=[pl.BlockSpec((None, 1, rows), lambda i: (i, 0, 0), memory_space=pltpu.SMEM),
                  pl.BlockSpec((rows * SLAB_PITCH, LANES), lambda i: (i, 0))],
        out_specs=pl.BlockSpec(memory_space=pl.ANY),
        out_shape=jax.ShapeDtypeStruct((n_sorted * SLAB_PITCH, LANES), payload.dtype),
        scratch_shapes=[pltpu.SemaphoreType.DMA(())],
        compiler_params=_cparams(("arbitrary",)),
        name="dispatch",
    )(dest3d, payload)


EXPERT_INPUT_SLOTS = 3


def _experts_kernel(grp_ref, ea_ref, eb_ref, blk_ref, valid_ref, xs_hbm, wg_ref, wu_ref, wd_ref, y_ref,
                    xbuf_ref, sem):
    i = pl.program_id(0)
    nt = pl.num_programs(0)
    d = wd_ref.shape[2]
    n_feat_rows = d // LANES
    tile_rows = MOE_TILE * SLAB_PITCH
    ahead = EXPERT_INPUT_SLOTS - 1

    def fetch(t):
        slot = t % EXPERT_INPUT_SLOTS
        return pltpu.make_async_copy(xs_hbm.at[pl.ds(blk_ref[t] * tile_rows, tile_rows)],
                                     xbuf_ref.at[slot], sem.at[slot])

    @pl.when(i == 0)
    def _():
        for t in range(ahead):
            @pl.when(valid_ref[t] == 1)
            def _():
                fetch(t).start()

    nxt = jnp.minimum(i + ahead, nt - 1)

    @pl.when((i + ahead < nt) & (valid_ref[nxt] == 1))
    def _():
        fetch(nxt).start()

    @pl.when(valid_ref[i] == 1)
    def _():
        fetch(i).wait()
        slot = i % EXPERT_INPUT_SLOTS
        token_row = lambda c: pl.ds(c, MOE_TILE, stride=SLAB_PITCH)
        x = jnp.concatenate([xbuf_ref[slot, token_row(c), :].astype(BF16) for c in range(n_feat_rows)], axis=1)
        gates = xbuf_ref[slot, token_row(n_feat_rows), :]
        experts = (ea_ref[i], eb_ref[i])
        gate = [_dot(x, wg_ref[e]) for e in experts]
        up = [_dot(x, wu_ref[e]) for e in experts]
        hid = [(jax.nn.silu(g) * u).astype(BF16) for g, u in zip(gate, up)]
        out = [_dot(hv, wd_ref[e]) for hv, e in zip(hid, experts)]
        y = gates[:, 0:1] * out[0] + gates[:, 1:2] * out[1]
        for c in range(n_feat_rows):
            y_ref[token_row(c), :] = y[:, c * LANES:(c + 1) * LANES]
        y_ref[token_row(n_feat_rows), :] = jnp.zeros((MOE_TILE, LANES), F32)


def _experts(tile_group, tile_a, tile_b, tile_blk, tile_valid, xs, wg, wu, wd):
    n_sorted = xs.shape[0] // SLAB_PITCH
    nt = n_sorted // MOE_TILE
    assert nt >= EXPERT_INPUT_SLOTS
    group = lambda w: pl.BlockSpec((EXPERTS_PER_GROUP,) + w.shape[1:], lambda i, g, a, b, k, v: (g[i], 0, 0))
    grid_spec = pltpu.PrefetchScalarGridSpec(
        num_scalar_prefetch=5,
        grid=(nt,),
        in_specs=[pl.BlockSpec(memory_space=pl.ANY), group(wg), group(wu), group(wd)],
        out_specs=pl.BlockSpec((MOE_TILE * SLAB_PITCH, LANES), lambda i, g, a, b, k, v: (k[i], 0)),
        scratch_shapes=[pltpu.VMEM((EXPERT_INPUT_SLOTS, MOE_TILE * SLAB_PITCH, LANES), F32),
                        pltpu.SemaphoreType.DMA((EXPERT_INPUT_SLOTS,))],
    )
    return pl.pallas_call(
        _experts_kernel,
        grid_spec=grid_spec,
        out_shape=jax.ShapeDtypeStruct(xs.shape, F32),
        compiler_params=_cparams(("arbitrary",)),
        name="experts",
    )(tile_group, tile_a, tile_b, tile_blk, tile_valid, xs, wg, wu, wd)


COMBINE_ROW_BLOCK = 64
COMBINE_UNROLL = 4


def _combine_kernel(dcur_ref, dnext_ref, x2_ref, nw_ref, ys_hbm, o_ref, buf_ref, sem):
    i = pl.program_id(0)
    n = pl.num_programs(0)
    rows, d = x2_ref.shape
    n_feat_rows = d // LANES
    slot = i % 2

    def gather(dref, s, start):
        def body(g, carry):
            for u in range(DMA_UNROLL):
                r = g * DMA_UNROLL + u
                cp = pltpu.make_async_copy(ys_hbm.at[pl.ds(dref[0, r], n_feat_rows)],
                                           buf_ref.at[s, pl.ds(r * SLAB_PITCH, n_feat_rows)], sem.at[s])
                if start:
                    cp.start(priority=u % DMA_PRIORITIES)
                else:
                    cp.wait()
            return carry

        lax.fori_loop(0, rows // DMA_UNROLL, body, 0)

    @pl.when(i == 0)
    def _():
        gather(dcur_ref, slot, True)

    @pl.when(i + 1 < n)
    def _():
        gather(dnext_ref, 1 - slot, True)

    gather(dcur_ref, slot, False)

    def block(b, carry):
        r0 = pl.multiple_of(b * COMBINE_ROW_BLOCK, COMBINE_ROW_BLOCK)
        rs = pl.ds(r0, COMBINE_ROW_BLOCK)
        chunks = [x2_ref[rs, c * LANES:(c + 1) * LANES]
                  + buf_ref[slot, pl.ds(r0 * SLAB_PITCH + c, COMBINE_ROW_BLOCK, stride=SLAB_PITCH), :]
                  for c in range(n_feat_rows)]
        ssq = chunks[0] * chunks[0]
        for ch in chunks[1:]:
            ssq = ssq + ch * ch
        scale = lax.rsqrt(jnp.sum(ssq, axis=-1, keepdims=True) / d + RMS_EPS)
        for c, ch in enumerate(chunks):
            o_ref[rs, c * LANES:(c + 1) * LANES] = ch * scale * nw_ref[:, c * LANES:(c + 1) * LANES]
        return carry

    lax.fori_loop(0, rows // COMBINE_ROW_BLOCK, block, 0, unroll=COMBINE_UNROLL)


def _combine(dest3d, x2, norm_w, ys):
    n_steps, _, rows = dest3d.shape
    t, d = x2.shape
    return pl.pallas_call(
        _combine_kernel,
        grid=(n_steps,),
        in_specs=[pl.BlockSpec((None, 1, rows), lambda i: (i, 0, 0), memory_space=pltpu.SMEM),
                  pl.BlockSpec((None, 1, rows), lambda i: (jnp.minimum(i + 1, n_steps - 1), 0, 0),
                               memory_space=pltpu.SMEM),
                  pl.BlockSpec((rows, d), lambda i: (i, 0)),
                  pl.BlockSpec((1, d), lambda i: (0, 0)),
                  pl.BlockSpec(memory_space=pl.ANY)],
        out_specs=pl.BlockSpec((rows, d), lambda i: (i, 0)),
        out_shape=jax.ShapeDtypeStruct((t, d), F32),
        scratch_shapes=[pltpu.VMEM((2, rows * SLAB_PITCH, LANES), F32), pltpu.SemaphoreType.DMA((2,))],
        compiler_params=_cparams(("arbitrary",)),
        name="combine",
    )(dest3d, dest3d, x2, norm_w, ys)


def _pad_cols(w, n):
    return jnp.pad(w, ((0, 0), (0, n - w.shape[1])))


def _tile_plan(counts, n_tiles):
    per_bucket = (counts + (MOE_TILE - 1)) // MOE_TILE
    tile_end = jnp.cumsum(per_bucket)
    tile_start = tile_end - per_bucket
    total = tile_end[-1]
    ids = jnp.arange(n_tiles, dtype=jnp.int32)
    blk = jnp.minimum(ids, total - 1)
    bucket = jnp.sum((tile_end[None, :] <= blk[:, None]).astype(jnp.int32), axis=1)
    pair = EXPERTS_PER_GROUP * EXPERTS_PER_GROUP
    group = bucket // pair
    slot_a = (bucket % pair) // EXPERTS_PER_GROUP
    slot_b = bucket % EXPERTS_PER_GROUP
    valid = (ids < total).astype(jnp.int32)
    return tile_start * MOE_TILE, group, slot_a, slot_b, blk, valid


def kernel(x, mem, positions, mix_norm_w, w_in, attn_sinks, ssd_conv_w, ssd_conv_b, ssd_dt_bias, ssd_a_log, ssd_d, attn_out_norm_w, ssd_out_norm_w, w_out, xattn_norm_w, mem_norm_w, xattn_w_q, xattn_w_kv, xattn_w_o, ffn_norm_w, router_group_w, router_group_b, router_expert_w, router_expert_b, expert_w_gate, expert_w_up, expert_w_down, final_norm_w):
    batch, seq, d = x.shape
    mem_tokens = mem.shape[1]
    depth = w_in.shape[0]
    t = batch * seq
    attn_w = attn_out_norm_w.shape[1]
    ssd_w = ssd_out_norm_w.shape[1]
    kv_w = N_KV_HEADS * HEAD_DIM
    conv_dim = ssd_conv_w.shape[2]
    n_exp = router_expert_w.shape[2]
    assert n_exp == N_GROUPS * EXPERTS_PER_GROUP and router_group_w.shape[2] == N_GROUPS
    assert seq % (SWA_SUB * ATTN_BLOCK) == 0 and seq % SSD_CHUNK == 0 and t % MOE_TILE == 0
    assert batch % SSD_BATCH == 0
    tm = min(512, seq)
    dispatch_rows = min(2048, t)
    combine_rows = min(1024, t)
    n_tiles = t // MOE_TILE + N_USED_BUCKETS
    n_sorted = n_tiles * MOE_TILE

    pos_row = positions.reshape(1, t)
    x2d = x.reshape(t, d)
    mem2d = mem.reshape(batch * mem_tokens, d)
    row = lambda v: v.reshape(1, -1)

    for l in range(depth):
        widths = (attn_w, kv_w, kv_w, ssd_w, conv_dim, LANES)
        n_main = sum(widths[:-1])
        q, k, v, z, xbc, dt_raw = _in_proj(pos_row, x2d, row(mix_norm_w[l]), w_in[l][:, :n_main].astype(BF16),
                                           _pad_cols(w_in[l][:, n_main:], LANES).astype(BF16), widths, tm)
        attn = _swa(attn_sinks[l], q, k, v, row(attn_out_norm_w[l]), batch, seq)
        ssd = _ssd(xbc, dt_raw, z, ssd_conv_w[l], row(ssd_conv_b[l]),
                   _pad_cols(row(ssd_dt_bias[l]), LANES), _pad_cols(row(ssd_a_log[l]), LANES),
                   row(jnp.repeat(ssd_d[l], SSD_HEAD_DIM)), row(ssd_out_norm_w[l]), batch, seq)
        kv = _mem_kv(mem2d, row(mem_norm_w[l]), xattn_w_kv[l].astype(BF16), mem_tokens)

        wo = w_out[l].astype(BF16)
        wr32 = _pad_cols(jnp.concatenate([router_expert_w[l], router_group_w[l]], axis=1), LANES)
        wr_hi = wr32.astype(BF16)
        wr = jnp.concatenate([wr_hi, (wr32 - wr_hi.astype(F32)).astype(BF16)], axis=1)
        br = _pad_cols(row(jnp.concatenate([router_expert_b[l], router_group_b[l]])), LANES)
        x2, payload, meta, counts = _post(
            x2d, attn, ssd, wo[:attn_w], wo[attn_w:], row(xattn_norm_w[l]), xattn_w_q[l].astype(BF16), kv,
            xattn_w_o[l].astype(BF16), row(ffn_norm_w[l]), wr, br, tm, seq, mem_tokens)

        row_start, tile_group, tile_a, tile_b, tile_blk, tile_valid = _tile_plan(
            counts[0].astype(jnp.int32), n_tiles)
        bucket = meta[2].astype(jnp.int32)
        hit = bucket[:, None] == jnp.arange(N_BUCKETS, dtype=jnp.int32)[None, :]
        dest = jnp.sum(jnp.where(hit, row_start[None, :], 0), axis=1) + meta[3].astype(jnp.int32)
        dest = dest * SLAB_PITCH
        xs = _dispatch(dest.reshape(t // dispatch_rows, 1, dispatch_rows), payload, n_sorted)
        ys = _experts(tile_group, tile_a, tile_b, tile_blk, tile_valid, xs, expert_w_gate[l].astype(BF16),
                      expert_w_up[l].astype(BF16), expert_w_down[l].astype(BF16))
        assert depth == 1
        x2d = _combine(dest.reshape(t // combine_rows, 1, combine_rows), x2, row(final_norm_w), ys)

    return x2d.reshape(batch, seq, d)
```

```python
import functools

import numpy as np
import jax
import jax.numpy as jnp
from jax import lax
from jax.experimental import pallas as pl
from jax.experimental.pallas import tpu as pltpu

RMS_EPS = 1e-5
HEAD_DIM = 64
N_KV_HEADS = 2
WINDOW = 128
ATTN_BLOCK = 128
ROT_DIM = 16
ROT_HALF = ROT_DIM // 2
ROPE_THETA = 500000.0
SSD_HEAD_DIM = 64
SSD_GROUPS = 2
SSD_STATE = 128
SSD_CONV = 4
SSD_CHUNK = 128
XATTN_HEADS = 4
N_GROUPS = 4
EXPERTS_PER_GROUP = 8

LANES = 128
SUBLANES = 8
MOE_TILE = 256
SLAB_PITCH = SUBLANES + 1
N_BUCKETS = N_GROUPS * EXPERTS_PER_GROUP * EXPERTS_PER_GROUP
N_USED_BUCKETS = N_GROUPS * (EXPERTS_PER_GROUP * (EXPERTS_PER_GROUP - 1) // 2)
VMEM_LIMIT = 56 * 1024 * 1024

F32 = jnp.float32
BF16 = jnp.bfloat16


def _cparams(semantics):
    return pltpu.CompilerParams(dimension_semantics=semantics, vmem_limit_bytes=VMEM_LIMIT)


def _rms(x, w):
    return x * lax.rsqrt(jnp.mean(x * x, axis=-1, keepdims=True) + RMS_EPS) * w


def _dot(a, b):
    return jnp.dot(a, b, preferred_element_type=F32)


def _dot_nt(a, b):
    return lax.dot_general(a, b, (((1,), (1,)), ((), ())), preferred_element_type=F32)


def _in_proj_kernel(pos_ref, x_ref, nw_ref, w_ref, wdt_ref, place_ref, base_ref,
                    q_ref, k_ref, v_ref, z_ref, xbc_ref, dt_ref):
    x = x_ref[...]
    hb = _rms(x, nw_ref[...]).astype(BF16)

    j = lax.broadcasted_iota(jnp.int32, (ROT_HALF, 1), 0).astype(F32)
    inv_freq = jnp.power(jnp.float32(ROPE_THETA), -(2.0 * j) / ROT_DIM)
    ang = pos_ref[...].astype(F32) * inv_freq
    cs = jnp.concatenate([jnp.cos(ang), jnp.sin(ang)], axis=0)

    cs_hi = cs.astype(BF16)
    cs_lo = (cs - cs_hi.astype(F32)).astype(BF16)
    tn = (((0,), (0,)), ((), ()))
    tabs = (lax.dot_general(cs_hi, place_ref[...], tn, preferred_element_type=F32)
            + lax.dot_general(cs_lo, place_ref[...], tn, preferred_element_type=F32))
    c_tab = tabs[:, :LANES] + base_ref[...]
    s_tab = tabs[:, LANES:]
    first = (lax.broadcasted_iota(jnp.int32, s_tab.shape, 1) % HEAD_DIM) < ROT_HALF
    s_up = jnp.where(first, s_tab, 0.0)
    s_dn = jnp.where(first, 0.0, s_tab)

    def rope(t):
        n = t.shape[1]
        reps = n // LANES
        c = jnp.tile(c_tab, (1, reps)) if reps > 1 else c_tab
        su = jnp.tile(s_up, (1, reps)) if reps > 1 else s_up
        sd = jnp.tile(s_dn, (1, reps)) if reps > 1 else s_dn
        return t * c + pltpu.roll(t, n - ROT_HALF, 1) * su + pltpu.roll(t, ROT_HALF, 1) * sd

    proj = _dot(hb, w_ref[...])
    o = 0
    pieces = []
    for ref in (q_ref, k_ref, v_ref, z_ref, xbc_ref):
        pieces.append(proj[:, o:o + ref.shape[1]])
        o += ref.shape[1]
    q, k, v, z, xbc = pieces
    dt = _dot(hb, wdt_ref[...])
    q_ref[...] = (rope(q) * (HEAD_DIM ** -0.5)).astype(BF16)
    k_ref[...] = rope(k).astype(BF16)
    v_ref[...] = v.astype(BF16)
    z_ref[...] = z.astype(BF16)
    xbc_ref[...] = xbc
    dt_ref[...] = dt


def _rope_placement():
    d = np.arange(LANES) % HEAD_DIM
    jj = np.arange(ROT_HALF)[:, None]
    ec = ((d[None, :] < ROT_DIM) & ((d[None, :] % ROT_HALF) == jj)).astype(np.float32)
    eup = -((d[None, :] < ROT_HALF) & (d[None, :] == jj)).astype(np.float32)
    edn = ((d[None, :] >= ROT_HALF) & (d[None, :] < ROT_DIM) & ((d[None, :] - ROT_HALF) == jj)).astype(np.float32)
    zero = np.zeros_like(ec)
    place = np.concatenate([np.concatenate([ec, zero], axis=1),
                            np.concatenate([zero, eup + edn], axis=1)], axis=0)
    base = (d >= ROT_DIM).astype(np.float32)[None, :]
    return jnp.asarray(place, dtype=BF16), jnp.asarray(base)


def _in_proj(pos_row, x2d, norm_w, w_main, w_dt, widths, tm):
    t, d = x2d.shape
    place, base = _rope_placement()
    full = lambda a: pl.BlockSpec(a.shape, lambda i: (0,) * a.ndim)
    row = lambda n: pl.BlockSpec((tm, n), lambda i: (i, 0))
    dtypes = (BF16, BF16, BF16, BF16, F32, F32)
    return pl.pallas_call(
        _in_proj_kernel,
        grid=(t // tm,),
        in_specs=[pl.BlockSpec((1, tm), lambda i: (0, i)), row(d), full(norm_w), full(w_main), full(w_dt),
                  full(place), full(base)],
        out_specs=[row(n) for n in widths],
        out_shape=[jax.ShapeDtypeStruct((t, n), dt) for n, dt in zip(widths, dtypes)],
        compiler_params=_cparams(("parallel",)),
        name="in_proj",
    )(pos_row, x2d, norm_w, w_main, w_dt, place, base)


SWA_SUB = 4


def _swa_kernel(sink_ref, q_ref, kp_ref, kc_ref, vp_ref, vc_ref, nw_ref, o_ref,
                s_ref, p_ref, bias_ref, acc_ref):
    i = pl.program_id(1)
    blk = ATTN_BLOCK
    n_q_heads = q_ref.shape[1] // HEAD_DIM
    q_per_kv = n_q_heads // N_KV_HEADS
    qi = lax.broadcasted_iota(jnp.int32, (blk, 2 * blk), 0) + blk
    ki = lax.broadcasted_iota(jnp.int32, (blk, 2 * blk), 1)
    rel = qi - ki
    local = (rel >= 0) & (rel < WINDOW)
    bias_ref[0] = jnp.where(local & ((i > 0) | (ki >= blk)), 0.0, -jnp.inf)
    bias_ref[1] = jnp.where(local, 0.0, -jnp.inf)
    sink_col = ki == 0

    def band(prev_ref, cur_ref, j):
        if j == 0:
            return jnp.concatenate([prev_ref[...], cur_ref[0:blk, :]], axis=0)
        return cur_ref[(j - 1) * blk:(j + 1) * blk, :]

    for j in range(SWA_SUB):
        kband = band(kp_ref, kc_ref, j)
        for kv in range(N_KV_HEADS):
            kcat = kband[:, kv * HEAD_DIM:(kv + 1) * HEAD_DIM]
            for g in range(q_per_kv):
                h = kv * q_per_kv + g
                r0 = (j * n_q_heads + h) * blk
                s_ref[r0:r0 + blk, :] = _dot_nt(q_ref[j * blk:(j + 1) * blk, h * HEAD_DIM:(h + 1) * HEAD_DIM], kcat)

    for j in range(SWA_SUB):
        for h in range(n_q_heads):
            rows = slice((j * n_q_heads + h) * blk, (j * n_q_heads + h + 1) * blk)
            s = jnp.where(sink_col, sink_ref[h], s_ref[rows, :] + bias_ref[min(j, 1)])
            p_ref[rows, :] = jnp.exp(s - jnp.max(s, axis=-1, keepdims=True)).astype(BF16)

    ones = jnp.ones((2 * blk, LANES), BF16)
    low = lax.broadcasted_iota(jnp.int32, (blk, LANES), 1) < HEAD_DIM
    for j in range(SWA_SUB):
        vband = band(vp_ref, vc_ref, j)
        vband = jnp.where(lax.broadcasted_iota(jnp.int32, vband.shape, 0) == 0, jnp.zeros_like(vband), vband)
        for kv in range(N_KV_HEADS):
            v_kv = vband[:, kv * HEAD_DIM:(kv + 1) * HEAD_DIM]
            vaug = jnp.concatenate([v_kv, v_kv, ones], axis=1)
            for k in range(q_per_kv // 2):
                h = kv * q_per_kv + 2 * k
                r0 = (j * n_q_heads + h) * blk
                o_even = _dot(p_ref[r0:r0 + blk, :], vaug)
                o_odd = _dot(p_ref[r0 + blk:r0 + 2 * blk, :], vaug)
                num = jnp.where(low, o_even[:, :LANES], o_odd[:, :LANES])
                den = jnp.where(low, o_even[:, LANES:], o_odd[:, LANES:])
                acc_ref[j * blk:(j + 1) * blk, h * HEAD_DIM:(h + 2) * HEAD_DIM] = num * (1.0 / den)
    o_ref[...] = _rms(acc_ref[...], nw_ref[...]).astype(BF16)


def _swa(sinks, q, k, v, norm_w, batch, seq):
    t, qw = q.shape
    kw = k.shape[1]
    rows = SWA_SUB * ATTN_BLOCK
    nb = seq // rows
    n_heads = qw // HEAD_DIM
    cur = lambda b, i: (b * nb + i, 0)
    prev = lambda b, i: ((b * nb + i) * SWA_SUB - jnp.minimum(i, 1), 0)
    return pl.pallas_call(
        _swa_kernel,
        grid=(batch, nb),
        in_specs=[pl.BlockSpec(memory_space=pltpu.SMEM),
                  pl.BlockSpec((rows, qw), cur),
                  pl.BlockSpec((ATTN_BLOCK, kw), prev), pl.BlockSpec((rows, kw), cur),
                  pl.BlockSpec((ATTN_BLOCK, kw), prev), pl.BlockSpec((rows, kw), cur),
                  pl.BlockSpec((1, qw), lambda b, i: (0, 0))],
        out_specs=pl.BlockSpec((rows, qw), cur),
        out_shape=jax.ShapeDtypeStruct((t, qw), BF16),
        scratch_shapes=[pltpu.VMEM((SWA_SUB * n_heads * ATTN_BLOCK, 2 * ATTN_BLOCK), F32),
                        pltpu.VMEM((SWA_SUB * n_heads * ATTN_BLOCK, 2 * ATTN_BLOCK), BF16),
                        pltpu.VMEM((2, ATTN_BLOCK, 2 * ATTN_BLOCK), F32),
                        pltpu.VMEM((rows, qw), F32)],
        compiler_params=_cparams(("parallel", "parallel")),
        name="swa",
    )(sinks, q, k, k, v, v, norm_w)


SSD_TAIL = 16
SSD_BATCH = 4


def _ssd_kernel(xbc_ref, dt_ref, z_ref, cw_ref, cb_ref, dtb_ref, alog_ref, dskip_ref, nw_ref,
                tril_ref, shift_ref, expand_ref, o_ref, ext_ref, state_ref, y_ref):
    @pl.when(pl.program_id(1) == 0)
    def _():
        state_ref[...] = jnp.zeros_like(state_ref)
        ext_ref[:, 0:SSD_TAIL, :] = jnp.zeros((SSD_BATCH, SSD_TAIL, ext_ref.shape[2]), F32)

    for slot in range(SSD_BATCH):
        _ssd_chunk(xbc_ref.at[slot], dt_ref.at[slot], z_ref.at[slot], cw_ref, cb_ref, dtb_ref, alog_ref,
                   dskip_ref, nw_ref, tril_ref, shift_ref, expand_ref, o_ref.at[slot],
                   ext_ref.at[slot], state_ref.at[slot], y_ref.at[slot])


def _ssd_chunk(xbc_ref, dt_ref, z_ref, cw_ref, cb_ref, dtb_ref, alog_ref, dskip_ref, nw_ref,
               tril_ref, shift_ref, expand_ref, o_ref, ext_ref, state_ref, y_ref):
    L = SSD_CHUNK
    width = z_ref.shape[1]
    n_heads = width // SSD_HEAD_DIM
    heads_per_group = n_heads // SSD_GROUPS
    gw = width // SSD_GROUPS

    cur = xbc_ref[...]
    ext_ref[SSD_TAIL:SSD_TAIL + L, :] = cur
    shifted = _dot(shift_ref[...], ext_ref[...].astype(BF16))
    ext_ref[0:SSD_TAIL, :] = cur[L - SSD_TAIL:, :]
    acc = cb_ref[...] + cw_ref[SSD_CONV - 1:SSD_CONV, :] * cur
    for jj in range(SSD_CONV - 1):
        acc = acc + cw_ref[jj:jj + 1, :] * shifted[jj * L:(jj + 1) * L, :]
    u = jax.nn.silu(acc)
    xs = u[:, :width]
    bm = u[:, width:width + SSD_GROUPS * SSD_STATE]
    cm = u[:, width + SSD_GROUPS * SSD_STATE:]

    dt = jax.nn.softplus(dt_ref[...] + dtb_ref[...])
    da = dt * (-jnp.exp(alog_ref[...]))
    d1 = da.astype(BF16)
    r1 = da - d1.astype(F32)
    d2 = r1.astype(BF16)
    d3 = (r1 - d2.astype(F32)).astype(BF16)
    cum12 = _dot(tril_ref[...], jnp.concatenate([d1, d2], axis=1))
    cum = cum12[:, :LANES] + (cum12[:, LANES:] + _dot(tril_ref[...], d3))
    cum_t = cum.T
    cum_last = cum[L - 1:L, :]
    e_cum = jnp.exp(cum)
    w_end = dt * jnp.exp(cum_last - cum)

    s_hi = jnp.concatenate([dt, w_end, e_cum], axis=0).astype(BF16)
    spread = _dot(s_hi, expand_ref[...])
    e_lo = (e_cum - s_hi[2 * L:].astype(F32)).astype(BF16)
    dt_x, wend_x, ecum_x = spread[:L], spread[L:2 * L], spread[2 * L:] + _dot(e_lo, expand_ref[...])
    xc = (xs * dt_x).astype(BF16)
    xw = (xs * wend_x).astype(BF16)
    skip = dskip_ref[...] * xs

    row = lax.broadcasted_iota(jnp.int32, (L, L), 0)
    col = lax.broadcasted_iota(jnp.int32, (L, L), 1)
    causal = row >= col
    first_half = lax.broadcasted_iota(jnp.int32, (L, LANES), 1) < SSD_HEAD_DIM

    for g in range(SSD_GROUPS):
        gs = slice(g * gw, (g + 1) * gw)
        bg = bm[:, g * SSD_STATE:(g + 1) * SSD_STATE]
        cg = cm[:, g * SSD_STATE:(g + 1) * SSD_STATE].astype(BF16)
        cb = _dot_nt(cg, bg.astype(BF16))
        st = state_ref[:, gs]
        y_off = _dot(cg, st.astype(BF16)) * ecum_x[:, gs]
        state_ref[:, gs] = st * ecum_x[L - 1:L, gs] + _dot(bg.T.astype(BF16), xw[:, gs])
        for k in range(heads_per_group // 2):
            ps = slice(g * gw + k * LANES, g * gw + (k + 1) * LANES)
            halves = []
            for sub in range(2):
                h = g * heads_per_group + 2 * k + sub
                seg = cum[:, h:h + 1] - cum_t[h:h + 1, :]
                decay = jnp.exp(jnp.where(causal, seg, -jnp.inf))
                halves.append(_dot((cb * decay).astype(BF16), xc[:, ps]))
            y_ref[:, ps] = (jnp.where(first_half, halves[0], halves[1])
                            + y_off[:, k * LANES:(k + 1) * LANES] + skip[:, ps])

    gated = y_ref[...] * jax.nn.silu(z_ref[...].astype(F32))
    parts = []
    for g in range(SSD_GROUPS):
        gg = gated[:, g * gw:(g + 1) * gw]
        parts.append(gg * lax.rsqrt(jnp.mean(gg * gg, axis=-1, keepdims=True) + RMS_EPS))
    o_ref[...] = (jnp.concatenate(parts, axis=1) * nw_ref[...]).astype(BF16)


def _ssd(xbc, dt_raw, z, conv_w, conv_b, dt_bias, a_log, d_skip, norm_w, batch, seq):
    t, cw = xbc.shape
    width = z.shape[1]
    nc = seq // SSD_CHUNK
    L = SSD_CHUNK
    tril = jnp.asarray(np.tril(np.ones((L, L), np.float32)), dtype=BF16)
    shift = np.zeros(((SSD_CONV - 1) * L, SSD_TAIL + L), np.float32)
    for jj in range(SSD_CONV - 1):
        shift[jj * L + np.arange(L), SSD_TAIL - (SSD_CONV - 1) + jj + np.arange(L)] = 1.0
    shift = jnp.asarray(shift, dtype=BF16)
    expand = np.zeros((LANES, width), np.float32)
    for h in range(width // SSD_HEAD_DIM):
        expand[h, h * SSD_HEAD_DIM:(h + 1) * SSD_HEAD_DIM] = 1.0
    expand = jnp.asarray(expand, dtype=BF16)
    cur = lambda b, c: (b, c, 0)
    full = lambda a: pl.BlockSpec(a.shape, lambda b, c: (0,) * a.ndim)
    per_seq = lambda a: a.reshape(batch, seq, a.shape[1])
    out = pl.pallas_call(
        _ssd_kernel,
        grid=(batch // SSD_BATCH, nc),
        in_specs=[pl.BlockSpec((SSD_BATCH, L, cw), cur), pl.BlockSpec((SSD_BATCH, L, LANES), cur),
                  pl.BlockSpec((SSD_BATCH, L, width), cur),
                  full(conv_w), full(conv_b), full(dt_bias), full(a_log), full(d_skip), full(norm_w),
                  full(tril), full(shift), full(expand)],
        out_specs=pl.BlockSpec((SSD_BATCH, L, width), cur),
        out_shape=jax.ShapeDtypeStruct((batch, seq, width), BF16),
        scratch_shapes=[pltpu.VMEM((SSD_BATCH, SSD_TAIL + L, cw), F32),
                        pltpu.VMEM((SSD_BATCH, SSD_STATE, width), F32),
                        pltpu.VMEM((SSD_BATCH, L, width), F32)],
        compiler_params=_cparams(("parallel", "arbitrary")),
        name="ssd",
    )(per_seq(xbc), per_seq(dt_raw), per_seq(z), conv_w, conv_b, dt_bias, a_log, d_skip, norm_w,
      tril, shift, expand)
    return out.reshape(t, width)


def _mem_kv_kernel(m_ref, nw_ref, w_ref, o_ref):
    o_ref[...] = _dot(_rms(m_ref[...], nw_ref[...]).astype(BF16), w_ref[...]).astype(BF16)


def _mem_kv(mem2d, norm_w, w_kv, rows):
    t, d = mem2d.shape
    n = w_kv.shape[1]
    return pl.pallas_call(
        _mem_kv_kernel,
        grid=(t // rows,),
        in_specs=[pl.BlockSpec((rows, d), lambda i: (i, 0)),
                  pl.BlockSpec((1, d), lambda i: (0, 0)),
                  pl.BlockSpec((d, n), lambda i: (0, 0))],
        out_specs=pl.BlockSpec((rows, n), lambda i: (i, 0)),
        out_shape=jax.ShapeDtypeStruct((t, n), BF16),
        compiler_params=_cparams(("parallel",)),
        name="mem_kv",
    )(mem2d, norm_w, w_kv)


POST_PIECES = 2


def _post_kernel(x_ref, a_ref, s_ref, woa_ref, wos_ref, xnw_ref, wq_ref, kv_ref, wo_ref,
                 fnw_ref, wr_ref, br_ref, tril_ref,
                 x2_ref, pay_ref, meta_ref, cnt_ref, carry_ref, h3_ref, logit_ref):
    step = pl.program_id(0)
    tm = x_ref.shape[0]
    n_exp = N_GROUPS * EXPERTS_PER_GROUP

    @pl.when(step == 0)
    def _():
        carry_ref[...] = jnp.zeros_like(carry_ref)
        h3_ref[...] = jnp.zeros_like(h3_ref)
        logit_ref[...] = jnp.zeros_like(logit_ref)

    h3_prev = h3_ref[...]
    logits = logit_ref[...]

    pm = tm // POST_PIECES
    pieces = [slice(p * pm, (p + 1) * pm) for p in range(POST_PIECES)]
    x1 = [x_ref[r, :] + _dot(a_ref[r, :], woa_ref[...]) + _dot(s_ref[r, :], wos_ref[...]) for r in pieces]

    xw = wq_ref.shape[1]
    hd = xw // XATTN_HEADS
    q = [(_dot(_rms(v, xnw_ref[...]).astype(BF16), wq_ref[...]) * (hd ** -0.5)).astype(BF16) for v in x1]
    attn = []
    for qp in q:
        heads = []
        for h in range(XATTN_HEADS):
            s = _dot_nt(qp[:, h * hd:(h + 1) * hd], kv_ref[:, h * hd:(h + 1) * hd])
            p = jnp.exp(s - jnp.max(s, axis=-1, keepdims=True))
            o = _dot(p.astype(BF16), kv_ref[:, xw + h * hd:xw + (h + 1) * hd])
            heads.append((o / jnp.sum(p, axis=-1, keepdims=True)).astype(BF16))
        attn.append(jnp.concatenate(heads, axis=1))
    x2 = [v + _dot(at, wo_ref[...]) for v, at in zip(x1, attn)]

    for r, v in zip(pieces, x2):
        x2_ref[r, :] = v
        h3 = _rms(v, fnw_ref[...])
        h_hi = h3.astype(BF16)
        h_lo = (h3 - h_hi.astype(F32)).astype(BF16)
        t_hi = _dot(h_hi, wr_ref[...])
        h3_ref[r, :] = h3
        logit_ref[r, :] = t_hi[:, :LANES] + (t_hi[:, LANES:] + _dot(h_lo, wr_ref[:, :LANES])) + br_ref[...]

    lane = lax.broadcasted_iota(jnp.int32, (tm, LANES), 1)
    big = jnp.int32(LANES)
    neg = -jnp.inf
    g_l = jnp.where((lane >= n_exp) & (lane < n_exp + N_GROUPS), logits, neg)
    g_max = jnp.max(g_l, axis=-1, keepdims=True)
    g_idx = jnp.min(jnp.where(g_l == g_max, lane - n_exp, big), axis=-1, keepdims=True)
    g_gate = 1.0 / jnp.sum(jnp.exp(g_l - g_max), axis=-1, keepdims=True)
    e_l = jnp.where((lane < n_exp) & ((lane // EXPERTS_PER_GROUP) == g_idx), logits, neg)
    m1 = jnp.max(e_l, axis=-1, keepdims=True)
    i1 = jnp.min(jnp.where(e_l == m1, lane, big), axis=-1, keepdims=True)
    e_l2 = jnp.where(lane == i1, neg, e_l)
    m2 = jnp.max(e_l2, axis=-1, keepdims=True)
    i2 = jnp.min(jnp.where(e_l2 == m2, lane, big), axis=-1, keepdims=True)
    e2 = jnp.exp(m2 - m1)
    w1 = (1.0 / (1.0 + e2)) * g_gate
    w2 = (e2 / (1.0 + e2)) * g_gate
    first_low = i1 < i2
    lo = jnp.where(first_low, i1, i2) % EXPERTS_PER_GROUP
    hi = jnp.where(first_low, i2, i1) % EXPERTS_PER_GROUP
    w_lo = jnp.where(first_low, w1, w2)
    w_hi = jnp.where(first_low, w2, w1)
    bucket = g_idx * (EXPERTS_PER_GROUP * EXPERTS_PER_GROUP) + lo * EXPERTS_PER_GROUP + hi

    blane = lax.broadcasted_iota(jnp.int32, (tm, N_BUCKETS), 1)
    onehot = (blane == bucket).astype(F32)
    before = _dot(tril_ref[...], onehot.astype(BF16)) + carry_ref[...]
    rank = jnp.sum(onehot * before, axis=-1, keepdims=True)
    carry_ref[...] = carry_ref[...] + jnp.where(step > 0, jnp.sum(onehot, axis=0, keepdims=True), 0.0)
    cnt_ref[...] = carry_ref[...]

    meta = jnp.where(lane == 0, w_lo, jnp.where(lane == 1, w_hi, jnp.where(
        lane == 2, bucket.astype(F32), jnp.where(lane == 3, rank, 0.0))))
    n_feat_rows = h3_prev.shape[1] // LANES
    for c in range(n_feat_rows):
        pay_ref[pl.ds(c, tm, stride=SLAB_PITCH), :] = h3_prev[:, c * LANES:(c + 1) * LANES]
    pay_ref[pl.ds(n_feat_rows, tm, stride=SLAB_PITCH), :] = meta
    meta_ref[...] = meta.T[:SUBLANES, :]


def _post(x2d, attn, ssd, woa, wos, xnw, wq, kv, wo, fnw, wr, br, tm, seq, mem_tokens):
    t, d = x2d.shape
    n = t // tm
    tiles_per_batch = seq // tm
    tril = jnp.asarray(np.tril(np.ones((tm, tm), np.float32), -1), dtype=BF16)
    full = lambda a: pl.BlockSpec(a.shape, lambda i: (0,) * a.ndim)
    this = lambda i: jnp.minimum(i, n - 1)
    prev = lambda i: jnp.maximum(i - 1, 0)
    row = lambda w: pl.BlockSpec((tm, w), lambda i: (this(i), 0))
    return pl.pallas_call(
        _post_kernel,
        grid=(n + 1,),
        in_specs=[row(d), row(attn.shape[1]), row(ssd.shape[1]), full(woa), full(wos), full(xnw), full(wq),
                  pl.BlockSpec((mem_tokens, kv.shape[1]), lambda i: (this(i) // tiles_per_batch, 0)),
                  full(wo), full(fnw), full(wr), full(br), full(tril)],
        out_specs=[row(d), pl.BlockSpec((tm * SLAB_PITCH, LANES), lambda i: (prev(i), 0)),
                   pl.BlockSpec((SUBLANES, tm), lambda i: (0, prev(i))),
                   pl.BlockSpec((1, N_BUCKETS), lambda i: (0, 0))],
        out_shape=[jax.ShapeDtypeStruct((t, d), F32), jax.ShapeDtypeStruct((t * SLAB_PITCH, LANES), F32),
                   jax.ShapeDtypeStruct((SUBLANES, t), F32), jax.ShapeDtypeStruct((1, N_BUCKETS), F32)],
        scratch_shapes=[pltpu.VMEM((1, N_BUCKETS), F32), pltpu.VMEM((tm, d), F32), pltpu.VMEM((tm, LANES), F32)],
        compiler_params=_cparams(("arbitrary",)),
        name="post",
    )(x2d, attn, ssd, woa, wos, xnw, wq, kv, wo, fnw, wr, br, tril)


DMA_UNROLL = 8
DMA_PRIORITIES = 2


def _dispatch_kernel(dest_ref, pay_ref, xs_hbm, sem):
    rows = dest_ref.shape[-1]

    def copy(r):
        return pltpu.make_async_copy(pay_ref.at[pl.ds(r * SLAB_PITCH, SLAB_PITCH)],
                                     xs_hbm.at[pl.ds(dest_ref[0, r], SLAB_PITCH)], sem)

    def issue(g, carry):
        for u in range(DMA_UNROLL):
            copy(g * DMA_UNROLL + u).start(priority=u % DMA_PRIORITIES)
        return carry

    lax.fori_loop(0, rows // DMA_UNROLL, issue, 0)

    def drain(g, carry):
        for u in range(DMA_UNROLL):
            copy(g * DMA_UNROLL + u).wait()
        return carry

    lax.fori_loop(0, rows // DMA_UNROLL, drain, 0)


def _dispatch(dest3d, payload, n_sorted):
    n_steps, _, rows = dest3d.shape
    return pl.pallas_call(
        _dispatch_kernel,
        grid=(n_steps,),
        in_specs=[pl.BlockSpec((None, 1, rows), lambda i: (i, 0, 0), memory_space=pltpu.SMEM),
                  pl.BlockSpec((rows * SLAB_PITCH, LANES), lambda i: (i, 0))],
        out_specs=pl.BlockSpec(memory_space=pl.ANY),
        out_shape=jax.ShapeDtypeStruct((n_sorted * SLAB_PITCH, LANES), payload.dtype),
        scratch_shapes=[pltpu.SemaphoreType.DMA(())],
        compiler_params=_cparams(("arbitrary",)),
        name="dispatch",
    )(dest3d, payload)


EXPERT_INPUT_SLOTS = 3


def _experts_kernel(grp_ref, ea_ref, eb_ref, blk_ref, valid_ref, xs_hbm, wg_ref, wu_ref, wd_ref, y_ref,
                    xbuf_ref, sem):
    i = pl.program_id(0)
    nt = pl.num_programs(0)
    d = wd_ref.shape[2]
    n_feat_rows = d // LANES
    tile_rows = MOE_TILE * SLAB_PITCH
    ahead = EXPERT_INPUT_SLOTS - 1

    def fetch(t):
        slot = t % EXPERT_INPUT_SLOTS
        return pltpu.make_async_copy(xs_hbm.at[pl.ds(blk_ref[t] * tile_rows, tile_rows)],
                                     xbuf_ref.at[slot], sem.at[slot])

    @pl.when(i == 0)
    def _():
        for t in range(ahead):
            @pl.when(valid_ref[t] == 1)
            def _():
                fetch(t).start()

    nxt = jnp.minimum(i + ahead, nt - 1)

    @pl.when((i + ahead < nt) & (valid_ref[nxt] == 1))
    def _():
        fetch(nxt).start()

    @pl.when(valid_ref[i] == 1)
    def _():
        fetch(i).wait()
        slot = i % EXPERT_INPUT_SLOTS
        token_row = lambda c: pl.ds(c, MOE_TILE, stride=SLAB_PITCH)
        x = jnp.concatenate([xbuf_ref[slot, token_row(c), :].astype(BF16) for c in range(n_feat_rows)], axis=1)
        gates = xbuf_ref[slot, token_row(n_feat_rows), :]
        experts = (ea_ref[i], eb_ref[i])
        gate_up = [(_dot(x, wg_ref[e]), _dot(x, wu_ref[e])) for e in experts]
        hid = [(jax.nn.silu(g) * u).astype(BF16) for g, u in gate_up]
        out = [_dot(hv, wd_ref[e]) for hv, e in zip(hid, experts)]
        y = gates[:, 0:1] * out[0] + gates[:, 1:2] * out[1]
        for c in range(n_feat_rows):
            y_ref[token_row(c), :] = y[:, c * LANES:(c + 1) * LANES]
        y_ref[token_row(n_feat_rows), :] = jnp.zeros((MOE_TILE, LANES), F32)


def _experts(tile_group, tile_a, tile_b, tile_blk, tile_valid, xs, wg, wu, wd):
    n_sorted = xs.shape[0] // SLAB_PITCH
    nt = n_sorted // MOE_TILE
    assert nt >= EXPERT_INPUT_SLOTS
    group = lambda w: pl.BlockSpec((EXPERTS_PER_GROUP,) + w.shape[1:], lambda i, g, a, b, k, v: (g[i], 0, 0))
    grid_spec = pltpu.PrefetchScalarGridSpec(
        num_scalar_prefetch=5,
        grid=(nt,),
        in_specs=[pl.BlockSpec(memory_space=pl.ANY), group(wg), group(wu), group(wd)],
        out_specs=pl.BlockSpec((MOE_TILE * SLAB_PITCH, LANES), lambda i, g, a, b, k, v: (k[i], 0)),
        scratch_shapes=[pltpu.VMEM((EXPERT_INPUT_SLOTS, MOE_TILE * SLAB_PITCH, LANES), F32),
                        pltpu.SemaphoreType.DMA((EXPERT_INPUT_SLOTS,))],
    )
    return pl.pallas_call(
        _experts_kernel,
        grid_spec=grid_spec,
        out_shape=jax.ShapeDtypeStruct(xs.shape, F32),
        compiler_params=_cparams(("arbitrary",)),
        name="experts",
    )(tile_group, tile_a, tile_b, tile_blk, tile_valid, xs, wg, wu, wd)


COMBINE_ROW_BLOCK = 64
COMBINE_UNROLL = 4


def _combine_kernel(dcur_ref, dnext_ref, x2_ref, nw_ref, ys_hbm, o_ref, buf_ref, sem):
    i = pl.program_id(0)
    n = pl.num_programs(0)
    rows, d = x2_ref.shape
    n_feat_rows = d // LANES
    slot = i % 2

    def gather(dref, s, start):
        def body(g, carry):
            for u in range(DMA_UNROLL):
                r = g * DMA_UNROLL + u
                cp = pltpu.make_async_copy(ys_hbm.at[pl.ds(dref[0, r], n_feat_rows)],
                                           buf_ref.at[s, pl.ds(r * SLAB_PITCH, n_feat_rows)], sem.at[s])
                if start:
                    cp.start(priority=u % DMA_PRIORITIES)
                else:
                    cp.wait()
            return carry

        lax.fori_loop(0, rows // DMA_UNROLL, body, 0)

    @pl.when(i == 0)
    def _():
        gather(dcur_ref, slot, True)

    @pl.when(i + 1 < n)
    def _():
        gather(dnext_ref, 1 - slot, True)

    gather(dcur_ref, slot, False)

    def block(b, carry):
        r0 = pl.multiple_of(b * COMBINE_ROW_BLOCK, COMBINE_ROW_BLOCK)
        rs = pl.ds(r0, COMBINE_ROW_BLOCK)
        chunks = [x2_ref[rs, c * LANES:(c + 1) * LANES]
                  + buf_ref[slot, pl.ds(r0 * SLAB_PITCH + c, COMBINE_ROW_BLOCK, stride=SLAB_PITCH), :]
                  for c in range(n_feat_rows)]
        ssq = chunks[0] * chunks[0]
        for ch in chunks[1:]:
            ssq = ssq + ch * ch
        scale = lax.rsqrt(jnp.sum(ssq, axis=-1, keepdims=True) / d + RMS_EPS)
        for c, ch in enumerate(chunks):
            o_ref[rs, c * LANES:(c + 1) * LANES] = ch * scale * nw_ref[:, c * LANES:(c + 1) * LANES]
        return carry

    lax.fori_loop(0, rows // COMBINE_ROW_BLOCK, block, 0, unroll=COMBINE_UNROLL)


def _combine(dest3d, x2, norm_w, ys):
    n_steps, _, rows = dest3d.shape
    t, d = x2.shape
    return pl.pallas_call(
        _combine_kernel,
        grid=(n_steps,),
        in_specs=[pl.BlockSpec((None, 1, rows), lambda i: (i, 0, 0), memory_space=pltpu.SMEM),
                  pl.BlockSpec((None, 1, rows), lambda i: (jnp.minimum(i + 1, n_steps - 1), 0, 0),
                               memory_space=pltpu.SMEM),
                  pl.BlockSpec((rows, d), lambda i: (i, 0)),
                  pl.BlockSpec((1, d), lambda i: (0, 0)),
                  pl.BlockSpec(memory_space=pl.ANY)],
        out_specs=pl.BlockSpec((rows, d), lambda i: (i, 0)),
        out_shape=jax.ShapeDtypeStruct((t, d), F32),
        scratch_shapes=[pltpu.VMEM((2, rows * SLAB_PITCH, LANES), F32), pltpu.SemaphoreType.DMA((2,))],
        compiler_params=_cparams(("arbitrary",)),
        name="combine",
    )(dest3d, dest3d, x2, norm_w, ys)


def _pad_cols(w, n):
    return jnp.pad(w, ((0, 0), (0, n - w.shape[1])))


def _tile_plan(counts, n_tiles):
    per_bucket = (counts + (MOE_TILE - 1)) // MOE_TILE
    tile_end = jnp.cumsum(per_bucket)
    tile_start = tile_end - per_bucket
    total = tile_end[-1]
    ids = jnp.arange(n_tiles, dtype=jnp.int32)
    blk = jnp.minimum(ids, total - 1)
    bucket = jnp.sum((tile_end[None, :] <= blk[:, None]).astype(jnp.int32), axis=1)
    pair = EXPERTS_PER_GROUP * EXPERTS_PER_GROUP
    group = bucket // pair
    slot_a = (bucket % pair) // EXPERTS_PER_GROUP
    slot_b = bucket % EXPERTS_PER_GROUP
    valid = (ids < total).astype(jnp.int32)
    return tile_start * MOE_TILE, group, slot_a, slot_b, blk, valid


def kernel(x, mem, positions, mix_norm_w, w_in, attn_sinks, ssd_conv_w, ssd_conv_b, ssd_dt_bias, ssd_a_log, ssd_d, attn_out_norm_w, ssd_out_norm_w, w_out, xattn_norm_w, mem_norm_w, xattn_w_q, xattn_w_kv, xattn_w_o, ffn_norm_w, router_group_w, router_group_b, router_expert_w, router_expert_b, expert_w_gate, expert_w_up, expert_w_down, final_norm_w):
    batch, seq, d = x.shape
    mem_tokens = mem.shape[1]
    depth = w_in.shape[0]
    t = batch * seq
    attn_w = attn_out_norm_w.shape[1]
    ssd_w = ssd_out_norm_w.shape[1]
    kv_w = N_KV_HEADS * HEAD_DIM
    conv_dim = ssd_conv_w.shape[2]
    n_exp = router_expert_w.shape[2]
    assert n_exp == N_GROUPS * EXPERTS_PER_GROUP and router_group_w.shape[2] == N_GROUPS
    assert seq % (SWA_SUB * ATTN_BLOCK) == 0 and seq % SSD_CHUNK == 0 and t % MOE_TILE == 0
    assert batch % SSD_BATCH == 0
    tm = min(512, seq)
    dispatch_rows = min(2048, t)
    combine_rows = min(1024, t)
    n_tiles = t // MOE_TILE + N_USED_BUCKETS
    n_sorted = n_tiles * MOE_TILE

    pos_row = positions.reshape(1, t)
    x2d = x.reshape(t, d)
    mem2d = mem.reshape(batch * mem_tokens, d)
    row = lambda v: v.reshape(1, -1)

    for l in range(depth):
        widths = (attn_w, kv_w, kv_w, ssd_w, conv_dim, LANES)
        n_main = sum(widths[:-1])
        q, k, v, z, xbc, dt_raw = _in_proj(pos_row, x2d, row(mix_norm_w[l]), w_in[l][:, :n_main].astype(BF16),
                                           _pad_cols(w_in[l][:, n_main:], LANES).astype(BF16), widths, tm)
        attn = _swa(attn_sinks[l], q, k, v, row(attn_out_norm_w[l]), batch, seq)
        ssd = _ssd(xbc, dt_raw, z, ssd_conv_w[l], row(ssd_conv_b[l]),
                   _pad_cols(row(ssd_dt_bias[l]), LANES), _pad_cols(row(ssd_a_log[l]), LANES),
                   row(jnp.repeat(ssd_d[l], SSD_HEAD_DIM)), row(ssd_out_norm_w[l]), batch, seq)
        kv = _mem_kv(mem2d, row(mem_norm_w[l]), xattn_w_kv[l].astype(BF16), mem_tokens)

        wo = w_out[l].astype(BF16)
        wr32 = _pad_cols(jnp.concatenate([router_expert_w[l], router_group_w[l]], axis=1), LANES)
        wr_hi = wr32.astype(BF16)
        wr = jnp.concatenate([wr_hi, (wr32 - wr_hi.astype(F32)).astype(BF16)], axis=1)
        br = _pad_cols(row(jnp.concatenate([router_expert_b[l], router_group_b[l]])), LANES)
        x2, payload, meta, counts = _post(
            x2d, attn, ssd, wo[:attn_w], wo[attn_w:], row(xattn_norm_w[l]), xattn_w_q[l].astype(BF16), kv,
            xattn_w_o[l].astype(BF16), row(ffn_norm_w[l]), wr, br, tm, seq, mem_tokens)

        row_start, tile_group, tile_a, tile_b, tile_blk, tile_valid = _tile_plan(
            counts[0].astype(jnp.int32), n_tiles)
        bucket = meta[2].astype(jnp.int32)
        hit = bucket[:, None] == jnp.arange(N_BUCKETS, dtype=jnp.int32)[None, :]
        dest = jnp.sum(jnp.where(hit, row_start[None, :], 0), axis=1) + meta[3].astype(jnp.int32)
        dest = dest * SLAB_PITCH
        xs = _dispatch(dest.reshape(t // dispatch_rows, 1, dispatch_rows), payload, n_sorted)
        ys = _experts(tile_group, tile_a, tile_b, tile_blk, tile_valid, xs, expert_w_gate[l].astype(BF16),
                      expert_w_up[l].astype(BF16), expert_w_down[l].astype(BF16))
        assert depth == 1
        x2d = _combine(dest.reshape(t // combine_rows, 1, combine_rows), x2, row(final_norm_w), ys)

    return x2d.reshape(batch, seq, d)
```
